```python
import math
import jax, jax.numpy as jnp
from jax import lax
import numpy as np

D_MODEL = 2048
BATCH = 4
SEQ = 2048
DEPTH = 1
DEC_BATCH = 128
DEC_SEQ = 1
PAST_LEN = 2048
PAGE_SIZE = 128

ATTN_WIDTH = D_MODEL // 2
POOL_WIDTH = D_MODEL - ATTN_WIDTH
N_HEADS = 8
HEAD_DIM = ATTN_WIDTH // N_HEADS // 2
N_QK_HEADS = 2 * N_HEADS
V_DIM = 2 * HEAD_DIM
POOL_WINDOWS = (2, 4, 8, 16)
N_POOL_GROUPS = len(POOL_WINDOWS)
POOL_GROUP = POOL_WIDTH // N_POOL_GROUPS
POOL_STATE = max(POOL_WINDOWS) - 1
PROJ_WIDTH = 3 * ATTN_WIDTH + POOL_WIDTH
D_FF = -(-(8 * D_MODEL) // (3 * 256)) * 256
PLE_DIM = 256
ROPE_THETA = 10000.0
Q_BLOCK = 128
EPS = 1e-6

kernel_name = "hybrid_diffattn_multiscale_pool_decoder_step"


def rmsnorm(x, g):
    xf = x.astype(jnp.float32)
    y = xf * lax.rsqrt(jnp.mean(xf * xf, axis=-1, keepdims=True) + EPS)
    return (y * g.astype(jnp.float32)).astype(x.dtype)


def rope(x, pos):
    half = HEAD_DIM // 2
    inv = ROPE_THETA ** (-jnp.arange(half, dtype=jnp.float32) / half)
    ang = pos.astype(jnp.float32)[:, None] * inv[None, :]
    cos = jnp.cos(ang)[:, None, :]
    sin = jnp.sin(ang)[:, None, :]
    xf = x.astype(jnp.float32)
    x1, x2 = xf[..., :half], xf[..., half:]
    return jnp.concatenate([x1 * cos - x2 * sin, x2 * cos + x1 * sin], axis=-1).astype(x.dtype)


def diff_lambda(lq1, lk1, lq2, lk2, lam_init):
    f = jnp.float32
    return (jnp.exp(jnp.sum(lq1.astype(f) * lk1.astype(f)))
            - jnp.exp(jnp.sum(lq2.astype(f) * lk2.astype(f))) + lam_init)


def prompt_diff_attention(q, k, v, lam):
    B, S = q.shape[0], q.shape[1]
    nb = S // Q_BLOCK
    scale = HEAD_DIM ** -0.5
    qb = q.reshape(B, nb, Q_BLOCK, N_HEADS, 2, HEAD_DIM).swapaxes(0, 1)
    kpos = jnp.arange(S)

    def one_block(args):
        qi, bi = args
        s = jnp.einsum('bqhcd,bkhcd->bhcqk', qi, k, preferred_element_type=jnp.float32) * scale
        qpos = bi * Q_BLOCK + jnp.arange(Q_BLOCK)
        mask = kpos[None, :] <= qpos[:, None]
        a = jax.nn.softmax(jnp.where(mask, s, -jnp.inf), axis=-1)
        w = a[:, :, 0] - lam * a[:, :, 1]
        o = jnp.einsum('bhqk,bkhe->bqhe', w.astype(v.dtype), v, preferred_element_type=jnp.float32)
        return o.astype(v.dtype)

    o = lax.map(one_block, (qb, jnp.arange(nb)))
    return o.swapaxes(0, 1).reshape(B, S, N_HEADS, V_DIM)


def sample_diff_attention(q, k_new, v_new, k_past, v_past, lam):
    DS = q.shape[1]
    P = k_past.shape[1]
    scale = HEAD_DIM ** -0.5
    s_past = jnp.einsum('bqhcd,bkhcd->bhcqk', q, k_past, preferred_element_type=jnp.float32) * scale
    s_new = jnp.einsum('bqhcd,bkhcd->bhcqk', q, k_new, preferred_element_type=jnp.float32) * scale
    causal = jnp.arange(DS)[None, :] <= jnp.arange(DS)[:, None]
    s = jnp.concatenate([s_past, jnp.where(causal, s_new, -jnp.inf)], axis=-1)
    a = jax.nn.softmax(s, axis=-1)
    w = (a[:, :, 0] - lam * a[:, :, 1]).astype(v_new.dtype)
    o = (jnp.einsum('bhqk,bkhe->bqhe', w[..., :P], v_past, preferred_element_type=jnp.float32)
         + jnp.einsum('bhqk,bkhe->bqhe', w[..., P:], v_new, preferred_element_type=jnp.float32))
    return o.astype(v_new.dtype)


def pool_mixer(u, prefix, pos0, w_pool, pool_scale):
    B, S, C = u.shape
    ext = jnp.concatenate([prefix, u], axis=1)
    cs = jnp.cumsum(ext.astype(jnp.float32), axis=1)
    cs = jnp.concatenate([jnp.zeros((B, 1, C), jnp.float32), cs], axis=1)
    L = POOL_STATE + 1
    pos = pos0 + jnp.arange(S)
    outs = []
    for g, win in enumerate(POOL_WINDOWS):
        sl = slice(g * POOL_GROUP, (g + 1) * POOL_GROUP)
        wsum = cs[:, L:L + S, sl] - cs[:, L - win:L - win + S, sl]
        cnt = jnp.minimum(pos + 1, win).astype(jnp.float32)[None, :, None]
        pooled = wsum / cnt - u[..., sl].astype(jnp.float32)
        outs.append(jnp.einsum('bsc,cd->bsd', pooled.astype(u.dtype), w_pool[g]))
    out = jnp.concatenate(outs, axis=-1) * pool_scale
    return out, ext[:, -POOL_STATE:]


def run_layer(h, p_i, pos0, pool_prefix, attend, lam_init,
              g_mix, w_in, w_out, lq1, lk1, lq2, lk2, g_sub, w_pool, pool_scale,
              g_ffn, w_gate_up, w_down, g_ple, w_ple_gate, w_ple_proj):
    B, S, _ = h.shape
    xn = rmsnorm(h, g_mix)
    proj = xn @ w_in
    q, k, v, u = jnp.split(proj, [ATTN_WIDTH, 2 * ATTN_WIDTH, 3 * ATTN_WIDTH], axis=-1)
    pos = pos0 + jnp.arange(S)
    q = rope(q.reshape(B, S, N_QK_HEADS, HEAD_DIM), pos)
    k = rope(k.reshape(B, S, N_QK_HEADS, HEAD_DIM), pos)
    v = v.reshape(B, S, N_HEADS, V_DIM)
    lam = diff_lambda(lq1, lk1, lq2, lk2, lam_init)
    o = attend(q.reshape(B, S, N_HEADS, 2, HEAD_DIM), k.reshape(B, S, N_HEADS, 2, HEAD_DIM), v, lam)
    o = (rmsnorm(o, g_sub) * (1.0 - lam_init)).reshape(B, S, ATTN_WIDTH)
    pool_out, pool_state = pool_mixer(u, pool_prefix, pos0, w_pool, pool_scale)
    h = h + jnp.concatenate([o, pool_out], axis=-1) @ w_out
    gate, up = jnp.split(rmsnorm(h, g_ffn) @ w_gate_up, 2, axis=-1)
    h = h + (jax.nn.silu(gate) * up) @ w_down
    h = h + jax.nn.sigmoid(rmsnorm(h, g_ple) @ w_ple_gate) * (p_i @ w_ple_proj)
    return h, k, v, pool_state


def setup_inputs(seed: int = 0) -> dict:
    key = jax.random.key(seed)
    ks = jax.random.split(key, 32)
    f = jnp.float32
    n_pages = PAST_LEN // PAGE_SIZE
    n_used = DEC_BATCH * n_pages
    n_pool = n_used + n_used // 4
    nrm = lambda i, shape, s: jax.random.normal(ks[i], shape, f) * s
    page_table = jax.random.permutation(ks[7], n_pool)[:n_used].reshape(DEC_BATCH, n_pages).astype(jnp.int32)
    return {
        "x_prompt": nrm(0, (BATCH, SEQ, D_MODEL), 1.0),
        "x_sample": nrm(1, (DEC_BATCH, DEC_SEQ, D_MODEL), 1.0),
        "p_prompt": nrm(2, (DEPTH, BATCH, SEQ, PLE_DIM), 1.0),
        "p_sample": nrm(3, (DEPTH, DEC_BATCH, DEC_SEQ, PLE_DIM), 1.0),
        "cache_k": nrm(4, (DEPTH, n_pool, PAGE_SIZE, N_QK_HEADS, HEAD_DIM), 1.0),
        "cache_v": nrm(5, (DEPTH, n_pool, PAGE_SIZE, N_HEADS, V_DIM), 1.0),
        "state_pool": nrm(6, (DEPTH, DEC_BATCH, POOL_STATE, POOL_WIDTH), 1.0),
        "page_table": page_table,
        "g_mix": 1.0 + nrm(8, (DEPTH, D_MODEL), 0.02),
        "w_in": nrm(9, (DEPTH, D_MODEL, PROJ_WIDTH), D_MODEL ** -0.5),
        "w_out": nrm(10, (DEPTH, D_MODEL, D_MODEL), D_MODEL ** -0.5),
        "lambda_q1": nrm(11, (DEPTH, HEAD_DIM), 0.1),
        "lambda_k1": nrm(12, (DEPTH, HEAD_DIM), 0.1),
        "lambda_q2": nrm(13, (DEPTH, HEAD_DIM), 0.1),
        "lambda_k2": nrm(14, (DEPTH, HEAD_DIM), 0.1),
        "g_sub": 1.0 + nrm(15, (DEPTH, V_DIM), 0.02),
        "w_pool": nrm(16, (DEPTH, N_POOL_GROUPS, POOL_GROUP, POOL_GROUP), POOL_GROUP ** -0.5),
        "pool_scale": 1.0 + nrm(17, (DEPTH, POOL_WIDTH), 0.02),
        "g_ffn": 1.0 + nrm(18, (DEPTH, D_MODEL), 0.02),
        "w_gate_up": nrm(19, (DEPTH, D_MODEL, 2 * D_FF), D_MODEL ** -0.5),
        "w_down": nrm(20, (DEPTH, D_FF, D_MODEL), D_FF ** -0.5),
        "g_ple": 1.0 + nrm(21, (DEPTH, D_MODEL), 0.02),
        "w_ple_gate": nrm(22, (DEPTH, D_MODEL, D_MODEL), D_MODEL ** -0.5),
        "w_ple_proj": nrm(23, (DEPTH, PLE_DIM, D_MODEL), PLE_DIM ** -0.5),
        "g_final": 1.0 + nrm(24, (D_MODEL,), 0.02),
    }


def reference(x_prompt, x_sample, p_prompt, p_sample, cache_k, cache_v, state_pool, page_table,
              g_mix, w_in, w_out, lambda_q1, lambda_k1, lambda_q2, lambda_k2, g_sub, w_pool, pool_scale,
              g_ffn, w_gate_up, w_down, g_ple, w_ple_gate, w_ple_proj, g_final):
    n_dec = x_sample.shape[0]
    past_len = page_table.shape[1] * cache_k.shape[2]
    hp, hs = x_prompt, x_sample
    kp_l, vp_l, pp_l, ks_l, vs_l, ps_l = [], [], [], [], [], []
    for li in range(DEPTH):
        lam_init = 0.8 - 0.6 * math.exp(-0.3 * li)
        params = (g_mix[li], w_in[li], w_out[li], lambda_q1[li], lambda_k1[li], lambda_q2[li],
                  lambda_k2[li], g_sub[li], w_pool[li], pool_scale[li], g_ffn[li], w_gate_up[li],
                  w_down[li], g_ple[li], w_ple_gate[li], w_ple_proj[li])
        prefix0 = jnp.zeros((hp.shape[0], POOL_STATE, POOL_WIDTH), hp.dtype)
        hp, k_p, v_p, pool_p = run_layer(hp, p_prompt[li], 0, prefix0, prompt_diff_attention,
                                         lam_init, *params)
        k_past = cache_k[li][page_table].reshape(n_dec, past_len, N_HEADS, 2, HEAD_DIM)
        v_past = cache_v[li][page_table].reshape(n_dec, past_len, N_HEADS, V_DIM)
        attend_s = lambda q, k, v, lam, kp=k_past, vp=v_past: sample_diff_attention(q, k, v, kp, vp, lam)
        hs, k_s, v_s, pool_s = run_layer(hs, p_sample[li], past_len, state_pool[li], attend_s,
                                         lam_init, *params)
        kp_l.append(k_p); vp_l.append(v_p); pp_l.append(pool_p)
        ks_l.append(k_s); vs_l.append(v_s); ps_l.append(pool_s)
    y_prompt = rmsnorm(hp, g_final)
    y_sample = rmsnorm(hs, g_final)
    return (y_prompt, y_sample, jnp.stack(kp_l), jnp.stack(vp_l), jnp.stack(pp_l),
            jnp.stack(ks_l), jnp.stack(vs_l), jnp.stack(ps_l))
```

```python
import functools
import math

import jax
import jax.numpy as jnp
from jax import lax
from jax.experimental import pallas as pl
from jax.experimental.pallas import tpu as pltpu

F32 = jnp.float32
BF16 = jnp.bfloat16

N_HEADS = 8
HEAD_DIM = 64
V_DIM = 2 * HEAD_DIM
N_QK_HEADS = 2 * N_HEADS
ATTN_WIDTH = N_HEADS * V_DIM
POOL_WINDOWS = (2, 4, 8, 16)
POOL_STATE = max(POOL_WINDOWS) - 1
POOL_HALO = 16
ROPE_THETA = 10000.0
EPS = 1e-6
LANES = 128
VMEM_LIMIT = 56 * 1024 * 1024


def _cparams(sem):
    return pltpu.CompilerParams(dimension_semantics=sem, vmem_limit_bytes=VMEM_LIMIT)


def _rms(x, g):
    ms = jnp.mean(x * x, axis=-1, keepdims=True)
    return x * lax.rsqrt(ms + EPS) * g


def _rope_chunk(x, cos, sin_signed, first_half):
    swapped = jnp.where(first_half, pltpu.roll(x, 96, 1), pltpu.roll(x, 32, 1))
    return x * cos + swapped * sin_signed


def _in_proj_kernel(x_ref, g_ref, w_ref, cos_ref, sin_ref,
                    q_ref, kf_ref, kb_ref, vf_ref, vb_ref, u_ref, xn_ref):
    n = pl.program_id(1)

    @pl.when(n == 0)
    def _():
        xn_ref[...] = _rms(x_ref[...], g_ref[...]).astype(BF16)

    acc = jnp.dot(xn_ref[...], w_ref[...], preferred_element_type=F32)

    def rope(a):
        cos = cos_ref[...]
        sin = sin_ref[...]
        lane = lax.broadcasted_iota(jnp.int32, cos.shape, 1)
        first_half = (lane % HEAD_DIM) < (HEAD_DIM // 2)
        parts = [_rope_chunk(a[:, c * LANES:(c + 1) * LANES], cos, sin, first_half)
                 for c in range(a.shape[1] // LANES)]
        return jnp.concatenate(parts, axis=1)

    @pl.when(n == 0)
    def _():
        q_ref[...] = (rope(acc) * (HEAD_DIM ** -0.5)).astype(BF16)

    @pl.when(n == 1)
    def _():
        k = rope(acc)
        kf_ref[...] = k
        kb_ref[...] = k.astype(BF16)

    @pl.when(n == 2)
    def _():
        vf_ref[...] = acc
        vb_ref[...] = acc.astype(BF16)

    @pl.when(n == 3)
    def _():
        u_ref[...] = acc


def _in_proj(x, g_mix, w_in_bf, cos_t, sin_t, tm, table_blocks):
    m, d = x.shape
    width = ATTN_WIDTH
    row = lambda i, n: (i, 0)
    tab = (lambda i, n: (i % table_blocks, 0))
    outs = [jax.ShapeDtypeStruct((m, width), BF16),
            jax.ShapeDtypeStruct((m, width), F32),
            jax.ShapeDtypeStruct((m, width), BF16),
            jax.ShapeDtypeStruct((m, width), F32),
            jax.ShapeDtypeStruct((m, width), BF16),
            jax.ShapeDtypeStruct((m, width), F32)]
    return pl.pallas_call(
        _in_proj_kernel,
        out_shape=outs,
        grid=(m // tm, 4),
        in_specs=[pl.BlockSpec((tm, d), row),
                  pl.BlockSpec((1, d), lambda i, n: (0, 0)),
                  pl.BlockSpec((d, width), lambda i, n: (0, n)),
                  pl.BlockSpec((tm, LANES), tab),
                  pl.BlockSpec((tm, LANES), tab)],
        out_specs=[pl.BlockSpec((tm, width), row)] * 6,
        scratch_shapes=[pltpu.VMEM((tm, d), BF16)],
        compiler_params=_cparams(("arbitrary", "arbitrary")),
        name="in_proj",
    )(x, g_mix, w_in_bf, cos_t, sin_t)


def _lambda_value(lq1, lk1, lq2, lk2, lam_init):
    a = jnp.sum(lq1 * lk1, axis=-1, keepdims=True)
    b = jnp.sum(lq2 * lk2, axis=-1, keepdims=True)
    return jnp.exp(a) - jnp.exp(b) + lam_init


def _sub_norm(o, g_sub, lam_init):
    return _rms(o, g_sub) * (1.0 - lam_init)


def _prompt_attn_kernel(lq1_ref, lk1_ref, lq2_ref, lk2_ref, gs_ref, q_ref, k_ref, v_ref, o_ref,
                        m0_ref, l0_ref, a0_ref, m1_ref, l1_ref, a1_ref, *, tq, lam_init):
    qi = pl.program_id(2)
    q = q_ref[...]
    lane = lax.broadcasted_iota(jnp.int32, q.shape, 1)
    zero = jnp.zeros_like(q)
    qs = (jnp.where(lane < HEAD_DIM, q, zero), jnp.where(lane >= HEAD_DIM, q, zero))
    stats = ((m0_ref, l0_ref, a0_ref), (m1_ref, l1_ref, a1_ref))

    for m_ref, l_ref, a_ref in stats:
        m_ref[...] = jnp.full(m_ref.shape, -jnp.inf, F32)
        l_ref[...] = jnp.zeros(l_ref.shape, F32)
        a_ref[...] = jnp.zeros(a_ref.shape, F32)

    def block(j, masked):
        start = pl.multiple_of(j * tq, tq)
        kj = k_ref[pl.ds(start, tq), :]
        vj = v_ref[pl.ds(start, tq), :]
        if masked:
            r = lax.broadcasted_iota(jnp.int32, (tq, tq), 0)
            c = lax.broadcasted_iota(jnp.int32, (tq, tq), 1)
            keep = c <= r
        for qc, (m_ref, l_ref, a_ref) in zip(qs, stats):
            s = lax.dot_general(qc, kj, (((1,), (1,)), ((), ())), preferred_element_type=F32)
            if masked:
                s = jnp.where(keep, s, -jnp.inf)
            m_old = m_ref[...]
            m_new = jnp.maximum(m_old, jnp.max(s, axis=-1, keepdims=True))
            p = jnp.exp(s - m_new)
            alpha = jnp.exp(m_old - m_new)
            l_ref[...] = alpha * l_ref[...] + jnp.sum(p, axis=-1, keepdims=True)
            a_ref[...] = alpha * a_ref[...] + jnp.dot(p.astype(BF16), vj, preferred_element_type=F32)
            m_ref[...] = m_new

    def body(j, carry):
        block(j, False)
        return carry

    lax.fori_loop(0, qi, body, 0)
    block(qi, True)

    lam = _lambda_value(lq1_ref[...], lk1_ref[...], lq2_ref[...], lk2_ref[...], lam_init)
    o = a0_ref[...] / l0_ref[...] - lam * (a1_ref[...] / l1_ref[...])
    o_ref[...] = _sub_norm(o, gs_ref[...], lam_init).astype(o_ref.dtype)


def _prompt_attention(q, k, v, lams, g_sub, batch, seq, tq, lam_init):
    nq = seq // tq
    small = lambda shape: pl.BlockSpec(shape, lambda b, h, i: (0, 0))
    kernel = functools.partial(_prompt_attn_kernel, tq=tq, lam_init=lam_init)
    return pl.pallas_call(
        kernel,
        out_shape=jax.ShapeDtypeStruct(q.shape, BF16),
        grid=(batch, N_HEADS, nq),
        in_specs=[small((1, HEAD_DIM))] * 4 + [small((1, V_DIM))] + [
            pl.BlockSpec((tq, V_DIM), lambda b, h, i: (b * nq + i, h)),
            pl.BlockSpec((seq, V_DIM), lambda b, h, i: (b, h)),
            pl.BlockSpec((seq, V_DIM), lambda b, h, i: (b, h))],
        out_specs=pl.BlockSpec((tq, V_DIM), lambda b, h, i: (b * nq + i, h)),
        scratch_shapes=[pltpu.VMEM((tq, 1), F32), pltpu.VMEM((tq, 1), F32), pltpu.VMEM((tq, V_DIM), F32)] * 2,
        compiler_params=_cparams(("arbitrary", "arbitrary", "arbitrary")),
        name="prompt_attention",
    )(*lams, g_sub, q, k, v)


def _sample_scores_kernel(pt_ref, qrows_ref, *rest, pages_per_step):
    k_refs = rest[:pages_per_step]
    s_ref = rest[pages_per_step]
    qrows = qrows_ref[0]
    page = k_refs[0].shape[1]
    for i, k_ref in enumerate(k_refs):
        s = lax.dot_general(qrows, k_ref[0], (((1,), (1,)), ((), ())), preferred_element_type=F32)
        s_ref[0, :, i * page:(i + 1) * page] = s


def _sample_scores(page_table, qrows, cache_k, pages_per_step):
    db, n_pages = page_table.shape
    page, width = cache_k.shape[1], cache_k.shape[2]
    steps = n_pages // pages_per_step

    def k_spec(i):
        return pl.BlockSpec((1, page, width), lambda b, c, pt: (pt[b, c * pages_per_step + i], 0, 0))

    grid_spec = pltpu.PrefetchScalarGridSpec(
        num_scalar_prefetch=1,
        grid=(db, steps),
        in_specs=[pl.BlockSpec((1, N_QK_HEADS, width), lambda b, c, pt: (b, 0, 0))]
                 + [k_spec(i) for i in range(pages_per_step)],
        out_specs=pl.BlockSpec((1, N_QK_HEADS, pages_per_step * page), lambda b, c, pt: (b, 0, c)),
    )
    return pl.pallas_call(
        functools.partial(_sample_scores_kernel, pages_per_step=pages_per_step),
        out_shape=jax.ShapeDtypeStruct((db, N_QK_HEADS, n_pages * page), F32),
        grid_spec=grid_spec,
        compiler_params=_cparams(("arbitrary", "arbitrary")),
        name="sample_scores",
    )(page_table, qrows, *([cache_k] * pages_per_step))


def _sample_pv_kernel(pt_ref, lq1_ref, lk1_ref, lq2_ref, lk2_ref, gs_ref, s_ref, qrows_ref, kn_ref, vn_ref,
                      *rest, pages_per_step, lam_init):
    v_refs = rest[:pages_per_step]
    o_ref = rest[pages_per_step]
    w_ref, wn_ref, acc_ref = rest[pages_per_step + 1:]
    c = pl.program_id(1)
    steps = pl.num_programs(1)
    page = v_refs[0].shape[1]

    @pl.when(c == 0)
    def _():
        s = s_ref[0]
        s_new = jnp.sum(qrows_ref[0] * kn_ref[0], axis=-1, keepdims=True)
        m = jnp.maximum(jnp.max(s, axis=-1, keepdims=True), s_new)
        p = jnp.exp(s - m)
        p_new = jnp.exp(s_new - m)
        l = jnp.sum(p, axis=-1, keepdims=True) + p_new
        lam = _lambda_value(lq1_ref[...], lk1_ref[...], lq2_ref[...], lk2_ref[...], lam_init)
        a = p / l
        a_new = p_new / l
        w = a[:N_HEADS] - lam * a[N_HEADS:]
        for i in range(w_ref.shape[0]):
            w_ref[i] = w[:, i * w_ref.shape[2]:(i + 1) * w_ref.shape[2]]
        wn_ref[...] = a_new[:N_HEADS] - lam * a_new[N_HEADS:]
        acc_ref[...] = jnp.zeros(acc_ref.shape, F32)

    w = w_ref[c]
    acc = acc_ref[...]
    for i, v_ref in enumerate(v_refs):
        acc += jnp.dot(w[:, i * page:(i + 1) * page], v_ref[0], preferred_element_type=F32)
    acc_ref[...] = acc

    @pl.when(c == steps - 1)
    def _():
        full = acc_ref[...] + wn_ref[...] * vn_ref[0]
        row = lax.broadcasted_iota(jnp.int32, full.shape, 0)
        col = lax.broadcasted_iota(jnp.int32, full.shape, 1)
        diag = jnp.sum(jnp.where(col // V_DIM == row, full, 0.0), axis=0, keepdims=True)
        gs = gs_ref[...]
        parts = [_sub_norm(diag[:, h * V_DIM:(h + 1) * V_DIM], gs, lam_init) for h in range(N_HEADS)]
        o_ref[0] = jnp.concatenate(parts, axis=1).astype(o_ref.dtype)


def _sample_pv(page_table, lams, g_sub, scores, qrows, k_new, v_new, cache_v, pages_per_step, lam_init):
    db, n_pages = page_table.shape
    page, width = cache_v.shape[1], cache_v.shape[2]
    steps = n_pages // pages_per_step
    chunk = pages_per_step * page
    small = lambda shape: pl.BlockSpec(shape, lambda b, c, pt: (0, 0))
    per_b = lambda shape: pl.BlockSpec(shape, lambda b, c, pt: (b, 0, 0))

    def v_spec(i):
        return pl.BlockSpec((1, page, width), lambda b, c, pt: (pt[b, c * pages_per_step + i], 0, 0))

    grid_spec = pltpu.PrefetchScalarGridSpec(
        num_scalar_prefetch=1,
        grid=(db, steps),
        in_specs=[small((1, HEAD_DIM))] * 4 + [small((1, V_DIM)),
                  per_b((1, N_QK_HEADS, n_pages * page)),
                  per_b((1, N_QK_HEADS, width)),
                  per_b((1, 1, width)),
                  per_b((1, 1, width))] + [v_spec(i) for i in range(pages_per_step)],
        out_specs=per_b((1, 1, width)),
        scratch_shapes=[pltpu.VMEM((steps, N_HEADS, chunk), F32),
                        pltpu.VMEM((N_HEADS, 1), F32),
                        pltpu.VMEM((N_HEADS, width), F32)],
    )
    return pl.pallas_call(
        functools.partial(_sample_pv_kernel, pages_per_step=pages_per_step, lam_init=lam_init),
        out_shape=jax.ShapeDtypeStruct((db, 1, width), BF16),
        grid_spec=grid_spec,
        compiler_params=_cparams(("arbitrary", "arbitrary")),
        name="sample_pv",
    )(page_table, *lams, g_sub, scores, qrows, k_new, v_new, *([cache_v] * pages_per_step))


def _pool_matmul(pooled_groups, wp_ref, scale):
    outs = [jnp.dot(pg.astype(BF16), wp_ref[g], preferred_element_type=F32)
            for g, pg in enumerate(pooled_groups)]
    return jnp.concatenate(outs, axis=1) * scale


def _pool_prompt_kernel(u_ref, halo_ref, wp_ref, sc_ref, o_ref, ext_ref, *, ts):
    i = pl.program_id(1)
    group = wp_ref.shape[1]
    u = u_ref[...]
    ext_ref[pl.ds(POOL_HALO, ts), :] = u
    ext_ref[pl.ds(0, POOL_HALO), :] = jnp.where(i > 0, halo_ref[...], 0.0)
    pos1 = (i * ts + lax.broadcasted_iota(jnp.int32, (ts, 1), 0) + 1).astype(F32)
    pooled = []
    for g, win in enumerate(POOL_WINDOWS):
        cols = slice(g * group, (g + 1) * group)
        wsum = u[:, cols]
        for k in range(1, win):
            wsum = wsum + ext_ref[pl.ds(POOL_HALO - k, ts), cols]
        cnt = jnp.minimum(pos1, float(win))
        pooled.append(wsum / cnt - u[:, cols])
    o_ref[...] = _pool_matmul(pooled, wp_ref, sc_ref[...]).astype(o_ref.dtype)


def _pool_prompt(u, w_pool_bf, pool_scale, batch, seq, ts):
    m, width = u.shape
    nt = seq // ts
    r = ts // POOL_HALO
    return pl.pallas_call(
        functools.partial(_pool_prompt_kernel, ts=ts),
        out_shape=jax.ShapeDtypeStruct((m, width), BF16),
        grid=(batch, nt),
        in_specs=[pl.BlockSpec((ts, width), lambda b, i: (b * nt + i, 0)),
                  pl.BlockSpec((POOL_HALO, width), lambda b, i: (jnp.maximum((b * nt + i) * r - 1, 0), 0)),
                  pl.BlockSpec(w_pool_bf.shape, lambda b, i: (0, 0, 0)),
                  pl.BlockSpec((1, width), lambda b, i: (0, 0))],
        out_specs=pl.BlockSpec((ts, width), lambda b, i: (b * nt + i, 0)),
        scratch_shapes=[pltpu.VMEM((ts + POOL_HALO, width), F32)],
        compiler_params=_cparams(("arbitrary", "arbitrary")),
        name="pool_prompt",
    )(u, u, w_pool_bf, pool_scale)


def _pool_sample_kernel(u_ref, st_ref, wp_ref, sc_ref, o_ref):
    group = wp_ref.shape[1]
    u = u_ref[...]
    pooled = []
    for g, win in enumerate(POOL_WINDOWS):
        cols = slice(g * group, (g + 1) * group)
        wsum = u[:, cols]
        for k in range(1, win):
            wsum = wsum + st_ref[POOL_STATE - k, :, cols]
        pooled.append(wsum / float(win) - u[:, cols])
    o_ref[...] = _pool_matmul(pooled, wp_ref, sc_ref[...]).astype(o_ref.dtype)


def _pool_sample(u, state_t, w_pool_bf, pool_scale, tb):
    m, width = u.shape
    return pl.pallas_call(
        _pool_sample_kernel,
        out_shape=jax.ShapeDtypeStruct((m, width), BF16),
        grid=(m // tb,),
        in_specs=[pl.BlockSpec((tb, width), lambda i: (i, 0)),
                  pl.BlockSpec((POOL_STATE, tb, width), lambda i: (0, i, 0)),
                  pl.BlockSpec(w_pool_bf.shape, lambda i: (0, 0, 0)),
                  pl.BlockSpec((1, width), lambda i: (0, 0))],
        out_specs=pl.BlockSpec((tb, width), lambda i: (i, 0)),
        compiler_params=_cparams(("arbitrary",)),
        name="pool_sample",
    )(u, state_t, w_pool_bf, pool_scale)


def _out_proj_kernel(o_ref, p_ref, w_ref, h_ref, out_ref):
    half = o_ref.shape[1]
    acc = jnp.dot(o_ref[...], w_ref[pl.ds(0, half), :], preferred_element_type=F32)
    acc += jnp.dot(p_ref[...], w_ref[pl.ds(half, half), :], preferred_element_type=F32)
    out_ref[...] = h_ref[...] + acc


def _out_proj(o, pool, w_out_bf, h, tm):
    m, d = h.shape
    row = lambda i: (i, 0)
    return pl.pallas_call(
        _out_proj_kernel,
        out_shape=jax.ShapeDtypeStruct((m, d), F32),
        grid=(m // tm,),
        in_specs=[pl.BlockSpec((tm, o.shape[1]), row),
                  pl.BlockSpec((tm, pool.shape[1]), row),
                  pl.BlockSpec(w_out_bf.shape, lambda i: (0, 0)),
                  pl.BlockSpec((tm, d), row)],
        out_specs=pl.BlockSpec((tm, d), row),
        compiler_params=_cparams(("arbitrary",)),
        name="out_proj",
    )(o, pool, w_out_bf, h)


def _ffn_kernel(h_ref, g_ref, wg_ref, wu_ref, wd_ref, out_ref, xn_ref, acc_ref):
    f = pl.program_id(1)

    @pl.when(f == 0)
    def _():
        xn_ref[...] = _rms(h_ref[...], g_ref[...]).astype(BF16)
        acc_ref[...] = jnp.zeros(acc_ref.shape, F32)

    xn = xn_ref[...]
    gate = jnp.dot(xn, wg_ref[...], preferred_element_type=F32)
    up = jnp.dot(xn, wu_ref[...], preferred_element_type=F32)
    act = (gate * jax.nn.sigmoid(gate) * up).astype(BF16)
    acc_ref[...] += jnp.dot(act, wd_ref[...], preferred_element_type=F32)

    @pl.when(f == pl.num_programs(1) - 1)
    def _():
        out_ref[...] = h_ref[...] + acc_ref[...]


def _ffn(h, g_ffn, w_gate_up_bf, w_down_bf, tm, tf):
    m, d = h.shape
    d_ff = w_down_bf.shape[0]
    nf = d_ff // tf
    row = lambda i, f: (i, 0)
    return pl.pallas_call(
        _ffn_kernel,
        out_shape=jax.ShapeDtypeStruct((m, d), F32),
        grid=(m // tm, nf),
        in_specs=[pl.BlockSpec((tm, d), row),
                  pl.BlockSpec((1, d), lambda i, f: (0, 0)),
                  pl.BlockSpec((d, tf), lambda i, f: (0, f)),
                  pl.BlockSpec((d, tf), lambda i, f: (0, nf + f)),
                  pl.BlockSpec((tf, d), lambda i, f: (f, 0))],
        out_specs=pl.BlockSpec((tm, d), row),
        scratch_shapes=[pltpu.VMEM((tm, d), BF16), pltpu.VMEM((tm, d), F32)],
        compiler_params=_cparams(("arbitrary", "arbitrary")),
        name="ffn",
    )(h, g_ffn, w_gate_up_bf, w_gate_up_bf, w_down_bf)


def _ple_kernel(h_ref, p_ref, gp_ref, wg_ref, wp_ref, gf_ref, y_ref, *, final_norm):
    h = h_ref[...]
    xn = _rms(h, gp_ref[...]).astype(BF16)
    gate = jax.nn.sigmoid(jnp.dot(xn, wg_ref[...], preferred_element_type=F32))
    proj = jnp.dot(p_ref[...].astype(BF16), wp_ref[...], preferred_element_type=F32)
    y = h + gate * proj
    y_ref[...] = _rms(y, gf_ref[...]) if final_norm else y


def _ple(h, p, g_ple, w_gate_bf, w_proj_bf, g_final, tm, final_norm):
    m, d = h.shape
    row = lambda i: (i, 0)
    const = lambda i: (0, 0)
    return pl.pallas_call(
        functools.partial(_ple_kernel, final_norm=final_norm),
        out_shape=jax.ShapeDtypeStruct((m, d), F32),
        grid=(m // tm,),
        in_specs=[pl.BlockSpec((tm, d), row),
                  pl.BlockSpec((tm, p.shape[1]), row),
                  pl.BlockSpec((1, d), const),
                  pl.BlockSpec(w_gate_bf.shape, const),
                  pl.BlockSpec(w_proj_bf.shape, const),
                  pl.BlockSpec((1, d), const)],
        out_specs=pl.BlockSpec((tm, d), row),
        compiler_params=_cparams(("arbitrary",)),
        name="ple_final",
    )(h, p, g_ple, w_gate_bf, w_proj_bf, g_final)


def _rope_tables(positions):
    half = HEAD_DIM // 2
    inv = ROPE_THETA ** (-jnp.arange(half, dtype=F32) / half)
    ang = positions.astype(F32)[:, None] * inv[None, :]
    cos, sin = jnp.cos(ang), jnp.sin(ang)
    reps = LANES // HEAD_DIM
    return jnp.tile(jnp.concatenate([cos, cos], axis=1), (1, reps)), \
        jnp.tile(jnp.concatenate([-sin, sin], axis=1), (1, reps))


def _row_tile(m, want):
    return want if m % want == 0 else m


def kernel(x_prompt, x_sample, p_prompt, p_sample, cache_k, cache_v, state_pool, page_table, g_mix, w_in, w_out, lambda_q1, lambda_k1, lambda_q2, lambda_k2, g_sub, w_pool, pool_scale, g_ffn, w_gate_up, w_down, g_ple, w_ple_gate, w_ple_proj, g_final):
    batch, seq, d = x_prompt.shape
    db, ds, _ = x_sample.shape
    depth = g_mix.shape[0]
    n_pages = page_table.shape[1]
    page = cache_k.shape[2]
    past = n_pages * page
    assert ds == 1 and d == 2 * ATTN_WIDTH

    hp = x_prompt.reshape(batch * seq, d)
    hs = x_sample.reshape(db * ds, d)
    cos_p, sin_p = _rope_tables(jnp.arange(seq))
    cos_s, sin_s = _rope_tables(jnp.full((db,), past))
    row2 = lambda a: a.reshape(1, -1)

    tm_p = _row_tile(batch * seq, 512)
    tq = _row_tile(seq, 256)
    ts = _row_tile(seq, 512)
    pages_per_step = 4 if n_pages % 4 == 0 else 1
    tf = 512

    outs = {k: [] for k in ("kp", "vp", "pp", "ks", "vs", "ps")}
    for li in range(depth):
        lam_init = 0.8 - 0.6 * math.exp(-0.3 * li)
        w_in_bf = w_in[li].astype(BF16)
        w_out_bf = w_out[li].astype(BF16)
        w_pool_bf = w_pool[li].astype(BF16)
        w_gu_bf = w_gate_up[li].astype(BF16)
        w_down_bf = w_down[li].astype(BF16)
        w_pg_bf = w_ple_gate[li].astype(BF16)
        w_pp_bf = w_ple_proj[li].astype(BF16)
        lams = tuple(row2(a[li]) for a in (lambda_q1, lambda_k1, lambda_q2, lambda_k2))
        gsub = row2(g_sub[li])

        q, kf, kb, vf, vb, u = _in_proj(hp, row2(g_mix[li]), w_in_bf, cos_p, sin_p, tm_p, seq // tm_p if seq % tm_p == 0 else 1)
        o = _prompt_attention(q, kb, vb, lams, gsub, batch, seq, tq, lam_init)
        pool = _pool_prompt(u, w_pool_bf, row2(pool_scale[li]), batch, seq, ts)
        h1 = _out_proj(o, pool, w_out_bf, hp, tm_p)
        h2 = _ffn(h1, row2(g_ffn[li]), w_gu_bf, w_down_bf, tm_p, tf)
        outs["kp"].append(kf.reshape(batch, seq, N_QK_HEADS, HEAD_DIM))
        outs["vp"].append(vf.reshape(batch, seq, N_HEADS, V_DIM))
        outs["pp"].append(u.reshape(batch, seq, -1)[:, seq - POOL_STATE:])
        p_rows = p_prompt[li].reshape(batch * seq, -1)
        last = li == depth - 1
        gfin = row2(g_final)
        hp_next = _ple(h2, p_rows, row2(g_ple[li]), w_pg_bf, w_pp_bf, gfin, tm_p, last)

        qs, kfs, _, vfs, _, us = _in_proj(hs, row2(g_mix[li]), w_in_bf, cos_s, sin_s, db, 1)
        sub = jnp.arange(ATTN_WIDTH) // HEAD_DIM
        owner = (jnp.arange(N_QK_HEADS) % N_HEADS) * 2 + jnp.arange(N_QK_HEADS) // N_HEADS
        sel = (sub[None, :] == owner[:, None]).astype(F32)
        qrows = qs.astype(F32)[:, None, :] * sel[None]
        ck = cache_k[li].reshape(cache_k.shape[1], page, ATTN_WIDTH)
        cv = cache_v[li].reshape(cache_v.shape[1], page, ATTN_WIDTH)
        scores = _sample_scores(page_table, qrows, ck, pages_per_step)
        o_s = _sample_pv(page_table, lams, gsub, scores, qrows, kfs.reshape(db, 1, -1), vfs.reshape(db, 1, -1),
                         cv, pages_per_step, lam_init).reshape(db, -1)
        state = state_pool[li]
        pool_s = _pool_sample(us, jnp.swapaxes(state, 0, 1), w_pool_bf, row2(pool_scale[li]), db)
        h1s = _out_proj(o_s, pool_s, w_out_bf, hs, db)
        h2s = _ffn(h1s, row2(g_ffn[li]), w_gu_bf, w_down_bf, db, tf)
        hs_next = _ple(h2s, p_sample[li].reshape(db * ds, -1), row2(g_ple[li]), w_pg_bf, w_pp_bf, gfin, db, last)
        outs["ks"].append(kfs.reshape(db, ds, N_QK_HEADS, HEAD_DIM))
        outs["vs"].append(vfs.reshape(db, ds, N_HEADS, V_DIM))
        outs["ps"].append(jnp.concatenate([state[:, 1:], us[:, None, :]], axis=1))
        hp, hs = hp_next, hs_next

    y_prompt = hp.reshape(batch, seq, d)
    y_sample = hs.reshape(db, ds, d)
    return (y_prompt, y_sample, jnp.stack(outs["kp"]), jnp.stack(outs["vp"]), jnp.stack(outs["pp"]),
            jnp.stack(outs["ks"]), jnp.stack(outs["vs"]), jnp.stack(outs["ps"]))
```

```python
import functools
import math

import jax
import jax.numpy as jnp
from jax import lax
from jax.experimental import pallas as pl
from jax.experimental.pallas import tpu as pltpu

F32 = jnp.float32
BF16 = jnp.bfloat16

N_HEADS = 8
HEAD_DIM = 64
V_DIM = 2 * HEAD_DIM
N_QK_HEADS = 2 * N_HEADS
ATTN_WIDTH = N_HEADS * V_DIM
POOL_WINDOWS = (2, 4, 8, 16)
POOL_STATE = max(POOL_WINDOWS) - 1
POOL_HALO = 16
ROPE_THETA = 10000.0
EPS = 1e-6
LANES = 128
VMEM_LIMIT = 56 * 1024 * 1024


def _cparams(sem):
    return pltpu.CompilerParams(dimension_semantics=sem, vmem_limit_bytes=VMEM_LIMIT)


def _rms(x, g):
    ms = jnp.mean(x * x, axis=-1, keepdims=True)
    return x * lax.rsqrt(ms + EPS) * g


def _rope_chunk(x, cos, sin_signed, first_half):
    swapped = jnp.where(first_half, pltpu.roll(x, 96, 1), pltpu.roll(x, 32, 1))
    return x * cos + swapped * sin_signed


def _in_proj_kernel(x_ref, g_ref, w_ref, cos_ref, sin_ref,
                    q_ref, kf_ref, kb_ref, vf_ref, vb_ref, u_ref, xn_ref, *, q_scale):
    n = pl.program_id(1)

    @pl.when(n == 0)
    def _():
        xn_ref[...] = _rms(x_ref[...], g_ref[...]).astype(BF16)

    acc = jnp.dot(xn_ref[...], w_ref[...], preferred_element_type=F32)

    def rope(a):
        cos = cos_ref[...]
        sin = sin_ref[...]
        lane = lax.broadcasted_iota(jnp.int32, cos.shape, 1)
        first_half = (lane % HEAD_DIM) < (HEAD_DIM // 2)
        parts = [_rope_chunk(a[:, c * LANES:(c + 1) * LANES], cos, sin, first_half)
                 for c in range(a.shape[1] // LANES)]
        return jnp.concatenate(parts, axis=1)

    @pl.when(n == 0)
    def _():
        q_ref[...] = (rope(acc) * q_scale).astype(BF16)

    @pl.when(n == 1)
    def _():
        k = rope(acc)
        kf_ref[...] = k
        kb_ref[...] = k.astype(BF16)

    @pl.when(n == 2)
    def _():
        vf_ref[...] = acc
        vb_ref[...] = acc.astype(BF16)

    @pl.when(n == 3)
    def _():
        u_ref[...] = acc


def _in_proj(x, g_mix, w_in_bf, cos_t, sin_t, tm, table_blocks, q_scale):
    m, d = x.shape
    width = ATTN_WIDTH
    row = lambda i, n: (i, 0)
    tab = (lambda i, n: (i % table_blocks, 0))
    outs = [jax.ShapeDtypeStruct((m, width), BF16),
            jax.ShapeDtypeStruct((m, width), F32),
            jax.ShapeDtypeStruct((m, width), BF16),
            jax.ShapeDtypeStruct((m, width), F32),
            jax.ShapeDtypeStruct((m, width), BF16),
            jax.ShapeDtypeStruct((m, width), F32)]
    return pl.pallas_call(
        functools.partial(_in_proj_kernel, q_scale=q_scale),
        out_shape=outs,
        grid=(m // tm, 4),
        in_specs=[pl.BlockSpec((tm, d), row),
                  pl.BlockSpec((1, d), lambda i, n: (0, 0)),
                  pl.BlockSpec((d, width), lambda i, n: (0, n)),
                  pl.BlockSpec((tm, LANES), tab),
                  pl.BlockSpec((tm, LANES), tab)],
        out_specs=[pl.BlockSpec((tm, width), row)] * 6,
        scratch_shapes=[pltpu.VMEM((tm, d), BF16)],
        compiler_params=_cparams(("arbitrary", "arbitrary")),
        name="in_proj",
    )(x, g_mix, w_in_bf, cos_t, sin_t)


def _lambda_value(lq1, lk1, lq2, lk2, lam_init):
    a = jnp.sum(lq1 * lk1, axis=-1, keepdims=True)
    b = jnp.sum(lq2 * lk2, axis=-1, keepdims=True)
    return jnp.exp(a) - jnp.exp(b) + lam_init


def _sub_norm(o, g_sub, lam_init):
    return _rms(o, g_sub) * (1.0 - lam_init)


def _prompt_attn_kernel(lq1_ref, lk1_ref, lq2_ref, lk2_ref, gs_ref, q_ref, k_ref, v_ref, o_ref,
                        vt_ref, m0_ref, l0_ref, a0_ref, m1_ref, l1_ref, a1_ref, *, tq, lam_init):
    qi = pl.program_id(2)
    n_kv = vt_ref.shape[0]

    @pl.when(qi == 0)
    def _():
        for j in range(n_kv):
            vt_ref[j] = v_ref[pl.ds(j * tq, tq), :].astype(F32).T.astype(BF16)

    qt = q_ref[...].astype(F32).T
    drow = lax.broadcasted_iota(jnp.int32, qt.shape, 0)
    qts = (jnp.where(drow < HEAD_DIM, qt, 0.0).astype(BF16), jnp.where(drow >= HEAD_DIM, qt, 0.0).astype(BF16))
    stats = ((m0_ref, l0_ref, a0_ref), (m1_ref, l1_ref, a1_ref))

    for m_ref, l_ref, a_ref in stats:
        m_ref[...] = jnp.full(m_ref.shape, -jnp.inf, F32)
        l_ref[...] = jnp.zeros(l_ref.shape, F32)
        a_ref[...] = jnp.zeros(a_ref.shape, F32)

    def block(j, masked):
        start = pl.multiple_of(j * tq, tq)
        kj = k_ref[pl.ds(start, tq), :]
        vtj = vt_ref[j]
        if masked:
            key = lax.broadcasted_iota(jnp.int32, (tq, tq), 0)
            qry = lax.broadcasted_iota(jnp.int32, (tq, tq), 1)
            keep = key <= qry
        for qc, (m_ref, l_ref, a_ref) in zip(qts, stats):
            st = jnp.dot(kj, qc, preferred_element_type=F32)
            if masked:
                st = jnp.where(keep, st, -jnp.inf)
            m_old = m_ref[...]
            m_new = jnp.maximum(m_old, jnp.max(st, axis=0, keepdims=True))
            pt = jnp.exp2(st - m_new)
            alpha = jnp.exp2(m_old - m_new)
            l_ref[...] = alpha * l_ref[...] + jnp.sum(pt, axis=0, keepdims=True)
            a_ref[...] = alpha * a_ref[...] + jnp.dot(vtj, pt.astype(BF16), preferred_element_type=F32)
            m_ref[...] = m_new

    def body(j, carry):
        block(j, False)
        return carry

    lax.fori_loop(0, qi, body, 0)
    block(qi, True)

    lam = _lambda_value(lq1_ref[...], lk1_ref[...], lq2_ref[...], lk2_ref[...], lam_init)
    ot = a0_ref[...] * (1.0 / l0_ref[...]) - a1_ref[...] * (lam / l1_ref[...])
    ms = jnp.mean(ot * ot, axis=0, keepdims=True)
    ot = ot * lax.rsqrt(ms + EPS) * gs_ref[...] * (1.0 - lam_init)
    o_ref[...] = ot.T.astype(o_ref.dtype)


def _prompt_attention(q, k, v, lams, g_sub_col, batch, seq, tq, lam_init):
    nq = seq // tq
    small = lambda shape: pl.BlockSpec(shape, lambda b, h, i: (0, 0))
    kernel = functools.partial(_prompt_attn_kernel, tq=tq, lam_init=lam_init)
    return pl.pallas_call(
        kernel,
        out_shape=jax.ShapeDtypeStruct(q.shape, BF16),
        grid=(batch, N_HEADS, nq),
        in_specs=[small((1, HEAD_DIM))] * 4 + [small((V_DIM, 1))] + [
            pl.BlockSpec((tq, V_DIM), lambda b, h, i: (b * nq + i, h)),
            pl.BlockSpec((seq, V_DIM), lambda b, h, i: (b, h)),
            pl.BlockSpec((seq, V_DIM), lambda b, h, i: (b, h))],
        out_specs=pl.BlockSpec((tq, V_DIM), lambda b, h, i: (b * nq + i, h)),
        scratch_shapes=[pltpu.VMEM((nq, V_DIM, tq), BF16)]
                       + [pltpu.VMEM((1, tq), F32), pltpu.VMEM((1, tq), F32), pltpu.VMEM((V_DIM, tq), F32)] * 2,
        compiler_params=_cparams(("arbitrary", "arbitrary", "arbitrary")),
        name="prompt_attention",
    )(*lams, g_sub_col, q, k, v)


TOK_PER_ROW = LANES // N_QK_HEADS


def _own_scores(q16, keys2d):
    full = lax.dot_general(q16, keys2d, (((1,), (1,)), ((), ())), preferred_element_type=F32)
    srow = lax.broadcasted_iota(jnp.int32, full.shape, 0)
    col = lax.broadcasted_iota(jnp.int32, full.shape, 1)
    return jnp.sum(jnp.where(col % N_QK_HEADS == srow, full, 0.0), axis=0, keepdims=True)


def _sample_scores_kernel(pt_ref, q_ref, *rest, pages_per_step):
    k_refs = rest[:pages_per_step]
    s_ref = rest[pages_per_step]
    q16 = q_ref[0]
    page = k_refs[0].shape[0]
    rows = page // TOK_PER_ROW
    for i, k_ref in enumerate(k_refs):
        flat = _own_scores(q16, k_ref[...].reshape(page * N_QK_HEADS, HEAD_DIM))
        for r in range(rows):
            s_ref[0, pl.ds(i * rows + r, 1), :] = flat[:, r * LANES:(r + 1) * LANES]


def _sample_scores(page_table, q16, cache_k, li, pages_per_step):
    db, n_pages = page_table.shape
    page = cache_k.shape[2]
    steps = n_pages // pages_per_step
    rows = page // TOK_PER_ROW

    def k_spec(i):
        return pl.BlockSpec((None, None, page, N_QK_HEADS, HEAD_DIM),
                            lambda b, c, pt: (li, pt[b, c * pages_per_step + i], 0, 0, 0))

    grid_spec = pltpu.PrefetchScalarGridSpec(
        num_scalar_prefetch=1,
        grid=(db, steps),
        in_specs=[pl.BlockSpec((1, N_QK_HEADS, HEAD_DIM), lambda b, c, pt: (b, 0, 0))]
                 + [k_spec(i) for i in range(pages_per_step)],
        out_specs=pl.BlockSpec((1, pages_per_step * rows, LANES), lambda b, c, pt: (b, c, 0)),
    )
    return pl.pallas_call(
        functools.partial(_sample_scores_kernel, pages_per_step=pages_per_step),
        out_shape=jax.ShapeDtypeStruct((db, n_pages * rows, LANES), F32),
        grid_spec=grid_spec,
        compiler_params=_cparams(("arbitrary", "arbitrary")),
        name="sample_scores",
    )(page_table, q16, *([cache_k] * pages_per_step))


def _lane_groups(x, op):
    for shift in (N_QK_HEADS, 2 * N_QK_HEADS, 4 * N_QK_HEADS):
        x = op(x, pltpu.roll(x, shift, 1))
    return x


def _sample_pv_kernel(pt_ref, lq1_ref, lk1_ref, lq2_ref, lk2_ref, gs_ref, s_ref, q_ref, kn_ref, vn_ref,
                      *rest, pages_per_step, lam_init):
    v_refs = rest[:pages_per_step]
    o_ref = rest[pages_per_step]
    c_ref, w8_ref, acc_ref = rest[pages_per_step + 1:]
    step = pl.program_id(1)
    steps = pl.num_programs(1)
    page = v_refs[0].shape[0]
    n_rows = s_ref.shape[1]
    w_rows_per_page = page * N_HEADS // LANES
    sub = lax.broadcasted_iota(jnp.int32, (N_HEADS, LANES), 0)
    lane = lax.broadcasted_iota(jnp.int32, (N_HEADS, LANES), 1)
    own = lane % N_HEADS == sub

    @pl.when(step == 0)
    def _():
        s = s_ref[0]
        kn_rows = jnp.concatenate([kn_ref[0]] * TOK_PER_ROW, axis=0)
        s_new = jnp.broadcast_to(_own_scores(q_ref[0], kn_rows), (N_HEADS, LANES))
        m8 = jnp.maximum(_lane_groups(jnp.broadcast_to(jnp.max(s, axis=0, keepdims=True), (N_HEADS, LANES)),
                                      jnp.maximum), s_new)
        m = m8[0:1]
        p = jnp.exp(s - m)
        p_new = jnp.exp(s_new - m8)
        l8 = _lane_groups(jnp.broadcast_to(jnp.sum(p, axis=0, keepdims=True), (N_HEADS, LANES)), jnp.add) + p_new
        lam = _lambda_value(lq1_ref[...], lk1_ref[...], lq2_ref[...], lk2_ref[...], lam_init)
        coef8 = jnp.where(lane % 2 == 0, 1.0, -lam) / l8
        c_ref[pl.ds(0, n_rows), :] = p * coef8[0:1]
        tail_rows = c_ref.shape[0] - n_rows
        trow = lax.broadcasted_iota(jnp.int32, (tail_rows, LANES), 0)
        c_ref[pl.ds(n_rows, tail_rows), :] = jnp.where(
            trow == 0, jnp.broadcast_to((p_new * coef8)[0:1], (tail_rows, LANES)), 0.0)
        half = c_ref.shape[0] // 2
        even = c_ref[pl.ds(0, half, stride=2), :]
        odd = c_ref[pl.ds(1, half, stride=2), :]
        src = lax.broadcasted_iota(jnp.int32, (LANES, LANES), 0)
        dst = lax.broadcasted_iota(jnp.int32, (LANES, LANES), 1)
        pack_lo = (dst == src // 2).astype(F32)
        pack_hi = (dst == src // 2 + LANES // 2).astype(F32)
        w8_ref[...] = (jnp.dot(even, pack_lo, preferred_element_type=F32)
                       + jnp.dot(odd, pack_hi, preferred_element_type=F32))
        acc_ref[...] = jnp.zeros(acc_ref.shape, F32)

    def head_rows(w_row):
        return jnp.where(own, jnp.broadcast_to(w_row, (N_HEADS, LANES)), 0.0)

    acc = acc_ref[...]
    for i, v_ref in enumerate(v_refs):
        first = pl.multiple_of((step * pages_per_step + i) * w_rows_per_page, w_rows_per_page)
        wp = w8_ref[pl.ds(first, w_rows_per_page), :]
        wsel = jnp.concatenate([head_rows(wp[r:r + 1, :]) for r in range(w_rows_per_page)], axis=1)
        acc += jnp.dot(wsel, v_ref[...].reshape(page * N_HEADS, V_DIM), preferred_element_type=F32)
    acc_ref[...] = acc

    @pl.when(step == steps - 1)
    def _():
        w_new = head_rows(w8_ref[pl.ds(n_rows // 2, 1), :])
        w_new = jnp.sum(jnp.where(lane < N_HEADS, w_new, 0.0), axis=1, keepdims=True)
        full = acc_ref[...] + w_new * vn_ref[0]
        o_ref[0] = _sub_norm(full, gs_ref[...], lam_init).astype(o_ref.dtype)


def _sample_pv(page_table, lams, g_sub, scores, q16, k_new, v_new, cache_v, li, pages_per_step, lam_init):
    db, n_pages = page_table.shape
    page = cache_v.shape[2]
    steps = n_pages // pages_per_step
    n_rows = scores.shape[1]
    assert n_rows % 16 == 0 and (page * N_HEADS) % LANES == 0
    small = lambda shape: pl.BlockSpec(shape, lambda b, c, pt: (0, 0))
    per_b = lambda shape: pl.BlockSpec(shape, lambda b, c, pt: (b, 0, 0))

    def v_spec(i):
        return pl.BlockSpec((None, None, page, N_HEADS, V_DIM),
                            lambda b, c, pt: (li, pt[b, c * pages_per_step + i], 0, 0, 0))

    grid_spec = pltpu.PrefetchScalarGridSpec(
        num_scalar_prefetch=1,
        grid=(db, steps),
        in_specs=[small((1, HEAD_DIM))] * 4 + [small((1, V_DIM)),
                  per_b((1, n_rows, LANES)),
                  per_b((1, N_QK_HEADS, HEAD_DIM)),
                  per_b((1, N_QK_HEADS, HEAD_DIM)),
                  per_b((1, N_HEADS, V_DIM))] + [v_spec(i) for i in range(pages_per_step)],
        out_specs=per_b((1, N_HEADS, V_DIM)),
        scratch_shapes=[pltpu.VMEM((n_rows + 16, LANES), F32),
                        pltpu.VMEM((n_rows // 2 + 8, LANES), F32),
                        pltpu.VMEM((N_HEADS, V_DIM), F32)],
    )
    return pl.pallas_call(
        functools.partial(_sample_pv_kernel, pages_per_step=pages_per_step, lam_init=lam_init),
        out_shape=jax.ShapeDtypeStruct((db, N_HEADS, V_DIM), BF16),
        grid_spec=grid_spec,
        compiler_params=_cparams(("arbitrary", "arbitrary")),
        name="sample_pv",
    )(page_table, *lams, g_sub, scores, q16, k_new, v_new, *([cache_v] * pages_per_step))


def _pool_matmul(pooled_groups, wp_ref, scale):
    outs = [jnp.dot(pg.astype(BF16), wp_ref[g], preferred_element_type=F32)
            for g, pg in enumerate(pooled_groups)]
    return jnp.concatenate(outs, axis=1) * scale


def _pool_prompt_kernel(u_ref, halo_ref, wp_ref, sc_ref, o_ref, ext_ref, *, ts):
    i = pl.program_id(1)
    group = wp_ref.shape[1]
    u = u_ref[...]
    ext_ref[pl.ds(POOL_HALO, ts), :] = u
    ext_ref[pl.ds(0, POOL_HALO), :] = jnp.where(i > 0, halo_ref[...], 0.0)
    pos1 = (i * ts + lax.broadcasted_iota(jnp.int32, (ts, 1), 0) + 1).astype(F32)
    pooled = []
    for g, win in enumerate(POOL_WINDOWS):
        cols = slice(g * group, (g + 1) * group)
        wsum = u[:, cols]
        for k in range(1, win):
            wsum = wsum + ext_ref[pl.ds(POOL_HALO - k, ts), cols]
        cnt = jnp.minimum(pos1, float(win))
        pooled.append(wsum / cnt - u[:, cols])
    o_ref[...] = _pool_matmul(pooled, wp_ref, sc_ref[...]).astype(o_ref.dtype)


def _pool_prompt(u, w_pool_bf, pool_scale, batch, seq, ts):
    m, width = u.shape
    nt = seq // ts
    r = ts // POOL_HALO
    return pl.pallas_call(
        functools.partial(_pool_prompt_kernel, ts=ts),
        out_shape=jax.ShapeDtypeStruct((m, width), BF16),
        grid=(batch, nt),
        in_specs=[pl.BlockSpec((ts, width), lambda b, i: (b * nt + i, 0)),
                  pl.BlockSpec((POOL_HALO, width), lambda b, i: (jnp.maximum((b * nt + i) * r - 1, 0), 0)),
                  pl.BlockSpec(w_pool_bf.shape, lambda b, i: (0, 0, 0)),
                  pl.BlockSpec((1, width), lambda b, i: (0, 0))],
        out_specs=pl.BlockSpec((ts, width), lambda b, i: (b * nt + i, 0)),
        scratch_shapes=[pltpu.VMEM((ts + POOL_HALO, width), F32)],
        compiler_params=_cparams(("arbitrary", "arbitrary")),
        name="pool_prompt",
    )(u, u, w_pool_bf, pool_scale)


def _pool_sample_kernel(u_ref, st_ref, wp_ref, sc_ref, o_ref):
    group = wp_ref.shape[1]
    u = u_ref[...]
    pooled = []
    for g, win in enumerate(POOL_WINDOWS):
        cols = slice(g * group, (g + 1) * group)
        wsum = u[:, cols]
        for k in range(1, win):
            wsum = wsum + st_ref[POOL_STATE - k, :, cols]
        pooled.append(wsum / float(win) - u[:, cols])
    o_ref[...] = _pool_matmul(pooled, wp_ref, sc_ref[...]).astype(o_ref.dtype)


def _pool_sample(u, state_t, w_pool_bf, pool_scale, tb):
    m, width = u.shape
    return pl.pallas_call(
        _pool_sample_kernel,
        out_shape=jax.ShapeDtypeStruct((m, width), BF16),
        grid=(m // tb,),
        in_specs=[pl.BlockSpec((tb, width), lambda i: (i, 0)),
                  pl.BlockSpec((POOL_STATE, tb, width), lambda i: (0, i, 0)),
                  pl.BlockSpec(w_pool_bf.shape, lambda i: (0, 0, 0)),
                  pl.BlockSpec((1, width), lambda i: (0, 0))],
        out_specs=pl.BlockSpec((tb, width), lambda i: (i, 0)),
        compiler_params=_cparams(("arbitrary",)),
        name="pool_sample",
    )(u, state_t, w_pool_bf, pool_scale)


def _out_proj_kernel(o_ref, p_ref, w_ref, h_ref, out_ref):
    half = o_ref.shape[1]
    acc = jnp.dot(o_ref[...], w_ref[pl.ds(0, half), :], preferred_element_type=F32)
    acc += jnp.dot(p_ref[...], w_ref[pl.ds(half, half), :], preferred_element_type=F32)
    out_ref[...] = h_ref[...] + acc


def _out_proj(o, pool, w_out_bf, h, tm):
    m, d = h.shape
    row = lambda i: (i, 0)
    return pl.pallas_call(
        _out_proj_kernel,
        out_shape=jax.ShapeDtypeStruct((m, d), F32),
        grid=(m // tm,),
        in_specs=[pl.BlockSpec((tm, o.shape[1]), row),
                  pl.BlockSpec((tm, pool.shape[1]), row),
                  pl.BlockSpec(w_out_bf.shape, lambda i: (0, 0)),
                  pl.BlockSpec((tm, d), row)],
        out_specs=pl.BlockSpec((tm, d), row),
        compiler_params=_cparams(("arbitrary",)),
        name="out_proj",
    )(o, pool, w_out_bf, h)


def _ffn_kernel(h_ref, g_ref, wg_ref, wu_ref, wd_ref, out_ref, xn_ref, acc_ref):
    f = pl.program_id(1)

    @pl.when(f == 0)
    def _():
        xn_ref[...] = _rms(h_ref[...], g_ref[...]).astype(BF16)
        acc_ref[...] = jnp.zeros(acc_ref.shape, F32)

    xn = xn_ref[...]
    gate = jnp.dot(xn, wg_ref[...], preferred_element_type=F32)
    up = jnp.dot(xn, wu_ref[...], preferred_element_type=F32)
    act = (gate * jax.nn.sigmoid(gate) * up).astype(BF16)
    acc_ref[...] += jnp.dot(act, wd_ref[...], preferred_element_type=F32)

    @pl.when(f == pl.num_programs(1) - 1)
    def _():
        out_ref[...] = h_ref[...] + acc_ref[...]


def _ffn(h, g_ffn, w_gate_up_bf, w_down_bf, tm, tf):
    m, d = h.shape
    d_ff = w_down_bf.shape[0]
    nf = d_ff // tf
    row = lambda i, f: (i, 0)
    return pl.pallas_call(
        _ffn_kernel,
        out_shape=jax.ShapeDtypeStruct((m, d), F32),
        grid=(m // tm, nf),
        in_specs=[pl.BlockSpec((tm, d), row),
                  pl.BlockSpec((1, d), lambda i, f: (0, 0)),
                  pl.BlockSpec((d, tf), lambda i, f: (0, f)),
                  pl.BlockSpec((d, tf), lambda i, f: (0, nf + f)),
                  pl.BlockSpec((tf, d), lambda i, f: (f, 0))],
        out_specs=pl.BlockSpec((tm, d), row),
        scratch_shapes=[pltpu.VMEM((tm, d), BF16), pltpu.VMEM((tm, d), F32)],
        compiler_params=_cparams(("arbitrary", "arbitrary")),
        name="ffn",
    )(h, g_ffn, w_gate_up_bf, w_gate_up_bf, w_down_bf)


def _ple_kernel(h_ref, p_ref, gp_ref, wg_ref, wp_ref, gf_ref, y_ref, *, final_norm):
    h = h_ref[...]
    xn = _rms(h, gp_ref[...]).astype(BF16)
    gate = jax.nn.sigmoid(jnp.dot(xn, wg_ref[...], preferred_element_type=F32))
    proj = jnp.dot(p_ref[...].astype(BF16), wp_ref[...], preferred_element_type=F32)
    y = h + gate * proj
    y_ref[...] = _rms(y, gf_ref[...]) if final_norm else y


def _ple(h, p, g_ple, w_gate_bf, w_proj_bf, g_final, tm, final_norm):
    m, d = h.shape
    row = lambda i: (i, 0)
    const = lambda i: (0, 0)
    return pl.pallas_call(
        functools.partial(_ple_kernel, final_norm=final_norm),
        out_shape=jax.ShapeDtypeStruct((m, d), F32),
        grid=(m // tm,),
        in_specs=[pl.BlockSpec((tm, d), row),
                  pl.BlockSpec((tm, p.shape[1]), row),
                  pl.BlockSpec((1, d), const),
                  pl.BlockSpec(w_gate_bf.shape, const),
                  pl.BlockSpec(w_proj_bf.shape, const),
                  pl.BlockSpec((1, d), const)],
        out_specs=pl.BlockSpec((tm, d), row),
        compiler_params=_cparams(("arbitrary",)),
        name="ple_final",
    )(h, p, g_ple, w_gate_bf, w_proj_bf, g_final)


def _rope_tables(positions):
    half = HEAD_DIM // 2
    inv = ROPE_THETA ** (-jnp.arange(half, dtype=F32) / half)
    ang = positions.astype(F32)[:, None] * inv[None, :]
    cos, sin = jnp.cos(ang), jnp.sin(ang)
    reps = LANES // HEAD_DIM
    return jnp.tile(jnp.concatenate([cos, cos], axis=1), (1, reps)), \
        jnp.tile(jnp.concatenate([-sin, sin], axis=1), (1, reps))


def _row_tile(m, want):
    return want if m % want == 0 else m


def kernel(x_prompt, x_sample, p_prompt, p_sample, cache_k, cache_v, state_pool, page_table, g_mix, w_in, w_out, lambda_q1, lambda_k1, lambda_q2, lambda_k2, g_sub, w_pool, pool_scale, g_ffn, w_gate_up, w_down, g_ple, w_ple_gate, w_ple_proj, g_final):
    batch, seq, d = x_prompt.shape
    db, ds, _ = x_sample.shape
    depth = g_mix.shape[0]
    n_pages = page_table.shape[1]
    page = cache_k.shape[2]
    past = n_pages * page
    assert ds == 1 and d == 2 * ATTN_WIDTH

    hp = x_prompt.reshape(batch * seq, d)
    hs = x_sample.reshape(db * ds, d)
    cos_p, sin_p = _rope_tables(jnp.arange(seq))
    cos_s, sin_s = _rope_tables(jnp.full((db,), past))
    row2 = lambda a: a.reshape(1, -1)

    tm_p = _row_tile(batch * seq, 512)
    tq = _row_tile(seq, 512)
    ts = _row_tile(seq, 512)
    pages_k = 4 if n_pages % 4 == 0 else 1
    pages_v = 8 if n_pages % 8 == 0 else 1
    tf = 512
    qk_scale = HEAD_DIM ** -0.5

    outs = {k: [] for k in ("kp", "vp", "pp", "ks", "vs", "ps")}
    for li in range(depth):
        lam_init = 0.8 - 0.6 * math.exp(-0.3 * li)
        w_in_bf = w_in[li].astype(BF16)
        w_out_bf = w_out[li].astype(BF16)
        w_pool_bf = w_pool[li].astype(BF16)
        w_gu_bf = w_gate_up[li].astype(BF16)
        w_down_bf = w_down[li].astype(BF16)
        w_pg_bf = w_ple_gate[li].astype(BF16)
        w_pp_bf = w_ple_proj[li].astype(BF16)
        lams = tuple(row2(a[li]) for a in (lambda_q1, lambda_k1, lambda_q2, lambda_k2))
        gsub = row2(g_sub[li])

        q, kf, kb, vf, vb, u = _in_proj(hp, row2(g_mix[li]), w_in_bf, cos_p, sin_p, tm_p,
                                        seq // tm_p if seq % tm_p == 0 else 1, qk_scale * math.log2(math.e))
        o = _prompt_attention(q, kb, vb, lams, g_sub[li].reshape(-1, 1), batch, seq, tq, lam_init)
        pool = _pool_prompt(u, w_pool_bf, row2(pool_scale[li]), batch, seq, ts)
        h1 = _out_proj(o, pool, w_out_bf, hp, tm_p)
        h2 = _ffn(h1, row2(g_ffn[li]), w_gu_bf, w_down_bf, tm_p, tf)
        outs["kp"].append(kf.reshape(batch, seq, N_QK_HEADS, HEAD_DIM))
        outs["vp"].append(vf.reshape(batch, seq, N_HEADS, V_DIM))
        outs["pp"].append(u.reshape(batch, seq, -1)[:, seq - POOL_STATE:])
        p_rows = p_prompt[li].reshape(batch * seq, -1)
        last = li == depth - 1
        gfin = row2(g_final)
        hp_next = _ple(h2, p_rows, row2(g_ple[li]), w_pg_bf, w_pp_bf, gfin, tm_p, last)

        qs, kfs, _, vfs, _, us = _in_proj(hs, row2(g_mix[li]), w_in_bf, cos_s, sin_s, db, 1, qk_scale)
        q16 = qs.astype(F32).reshape(db, N_QK_HEADS, HEAD_DIM)
        scores = _sample_scores(page_table, q16, cache_k, li, pages_k)
        o_s = _sample_pv(page_table, lams, gsub, scores, q16, kfs.reshape(db, N_QK_HEADS, HEAD_DIM),
                         vfs.reshape(db, N_HEADS, V_DIM), cache_v, li, pages_v, lam_init).reshape(db, -1)
        state = state_pool[li]
        pool_s = _pool_sample(us, jnp.swapaxes(state, 0, 1), w_pool_bf, row2(pool_scale[li]), db)
        h1s = _out_proj(o_s, pool_s, w_out_bf, hs, db)
        h2s = _ffn(h1s, row2(g_ffn[li]), w_gu_bf, w_down_bf, db, tf)
        hs_next = _ple(h2s, p_sample[li].reshape(db * ds, -1), row2(g_ple[li]), w_pg_bf, w_pp_bf, gfin, db, last)
        outs["ks"].append(kfs.reshape(db, ds, N_QK_HEADS, HEAD_DIM))
        outs["vs"].append(vfs.reshape(db, ds, N_HEADS, V_DIM))
        outs["ps"].append(jnp.concatenate([state[:, 1:], us[:, None, :]], axis=1))
        hp, hs = hp_next, hs_next

    y_prompt = hp.reshape(batch, seq, d)
    y_sample = hs.reshape(db, ds, d)
    return (y_prompt, y_sample, jnp.stack(outs["kp"]), jnp.stack(outs["vp"]), jnp.stack(outs["pp"]),
            jnp.stack(outs["ks"]), jnp.stack(outs["vs"]), jnp.stack(outs["ps"]))
```

```python
import functools
import math

import jax
import jax.numpy as jnp
from jax import lax
from jax.experimental import pallas as pl
from jax.experimental.pallas import tpu as pltpu

F32 = jnp.float32
BF16 = jnp.bfloat16

N_HEADS = 8
HEAD_DIM = 64
V_DIM = 2 * HEAD_DIM
N_QK_HEADS = 2 * N_HEADS
ATTN_WIDTH = N_HEADS * V_DIM
POOL_WINDOWS = (2, 4, 8, 16)
POOL_STATE = max(POOL_WINDOWS) - 1
POOL_HALO = 16
ROPE_THETA = 10000.0
EPS = 1e-6
LANES = 128
VMEM_LIMIT = 56 * 1024 * 1024


def _cparams(sem):
    return pltpu.CompilerParams(dimension_semantics=sem, vmem_limit_bytes=VMEM_LIMIT)


def _rms(x, g):
    ms = jnp.mean(x * x, axis=-1, keepdims=True)
    return x * lax.rsqrt(ms + EPS) * g


def _rope_chunk(x, cos, sin_signed, first_half):
    swapped = jnp.where(first_half, pltpu.roll(x, 96, 1), pltpu.roll(x, 32, 1))
    return x * cos + swapped * sin_signed


def _in_proj_kernel(x_ref, g_ref, w_ref, cos_ref, sin_ref,
                    q_ref, kf_ref, kb_ref, vf_ref, vb_ref, u_ref, xn_ref, *, q_scale):
    n = pl.program_id(1)

    @pl.when(n == 0)
    def _():
        xn_ref[...] = _rms(x_ref[...], g_ref[...]).astype(BF16)

    acc = jnp.dot(xn_ref[...], w_ref[...], preferred_element_type=F32)

    def rope(a):
        cos = cos_ref[...]
        sin = sin_ref[...]
        lane = lax.broadcasted_iota(jnp.int32, cos.shape, 1)
        first_half = (lane % HEAD_DIM) < (HEAD_DIM // 2)
        parts = [_rope_chunk(a[:, c * LANES:(c + 1) * LANES], cos, sin, first_half)
                 for c in range(a.shape[1] // LANES)]
        return jnp.concatenate(parts, axis=1)

    @pl.when(n == 0)
    def _():
        q_ref[...] = (rope(acc) * q_scale).astype(BF16)

    @pl.when(n == 1)
    def _():
        k = rope(acc)
        kf_ref[...] = k
        kb_ref[...] = k.astype(BF16)

    @pl.when(n == 2)
    def _():
        vf_ref[...] = acc
        vb_ref[...] = acc.astype(BF16)

    @pl.when(n == 3)
    def _():
        u_ref[...] = acc


def _in_proj(x, g_mix, w_in_bf, cos_t, sin_t, tm, table_blocks, q_scale):
    m, d = x.shape
    width = ATTN_WIDTH
    row = lambda i, n: (i, 0)
    tab = (lambda i, n: (i % table_blocks, 0))
    outs = [jax.ShapeDtypeStruct((m, width), BF16),
            jax.ShapeDtypeStruct((m, width), F32),
            jax.ShapeDtypeStruct((m, width), BF16),
            jax.ShapeDtypeStruct((m, width), F32),
            jax.ShapeDtypeStruct((m, width), BF16),
            jax.ShapeDtypeStruct((m, width), F32)]
    return pl.pallas_call(
        functools.partial(_in_proj_kernel, q_scale=q_scale),
        out_shape=outs,
        grid=(m // tm, 4),
        in_specs=[pl.BlockSpec((tm, d), row),
                  pl.BlockSpec((1, d), lambda i, n: (0, 0)),
                  pl.BlockSpec((d, width), lambda i, n: (0, n)),
                  pl.BlockSpec((tm, LANES), tab),
                  pl.BlockSpec((tm, LANES), tab)],
        out_specs=[pl.BlockSpec((tm, width), row)] * 6,
        scratch_shapes=[pltpu.VMEM((tm, d), BF16)],
        compiler_params=_cparams(("arbitrary", "arbitrary")),
        name="in_proj",
    )(x, g_mix, w_in_bf, cos_t, sin_t)


def _lambda_value(lq1, lk1, lq2, lk2, lam_init):
    a = jnp.sum(lq1 * lk1, axis=-1, keepdims=True)
    b = jnp.sum(lq2 * lk2, axis=-1, keepdims=True)
    return jnp.exp(a) - jnp.exp(b) + lam_init


def _sub_norm(o, g_sub, lam_init):
    return _rms(o, g_sub) * (1.0 - lam_init)


def _prompt_attn_kernel(lq1_ref, lk1_ref, lq2_ref, lk2_ref, gs_ref, q_ref, k_ref, v_ref, o_ref,
                        vt_ref, m0_ref, l0_ref, a0_ref, m1_ref, l1_ref, a1_ref, *, tq, lam_init):
    qi = pl.program_id(2)
    n_kv = vt_ref.shape[0]

    @pl.when(qi == 0)
    def _():
        for j in range(n_kv):
            vt_ref[j] = v_ref[pl.ds(j * tq, tq), :].astype(F32).T.astype(BF16)

    qt = q_ref[...].astype(F32).T
    drow = lax.broadcasted_iota(jnp.int32, qt.shape, 0)
    qts = (jnp.where(drow < HEAD_DIM, qt, 0.0).astype(BF16), jnp.where(drow >= HEAD_DIM, qt, 0.0).astype(BF16))
    stats = ((m0_ref, l0_ref, a0_ref), (m1_ref, l1_ref, a1_ref))

    for m_ref, l_ref, a_ref in stats:
        m_ref[...] = jnp.full(m_ref.shape, -jnp.inf, F32)
        l_ref[...] = jnp.zeros(l_ref.shape, F32)
        a_ref[...] = jnp.zeros(a_ref.shape, F32)

    def block(j, masked):
        start = pl.multiple_of(j * tq, tq)
        kj = k_ref[pl.ds(start, tq), :]
        vtj = vt_ref[j]
        if masked:
            key = lax.broadcasted_iota(jnp.int32, (tq, tq), 0)
            qry = lax.broadcasted_iota(jnp.int32, (tq, tq), 1)
            keep = key <= qry
        for qc, (m_ref, l_ref, a_ref) in zip(qts, stats):
            st = jnp.dot(kj, qc, preferred_element_type=F32)
            if masked:
                st = jnp.where(keep, st, -jnp.inf)
            m_old = m_ref[...]
            m_new = jnp.maximum(m_old, jnp.max(st, axis=0, keepdims=True))
            pt = jnp.exp2(st - m_new)
            alpha = jnp.exp2(m_old - m_new)
            l_ref[...] = alpha * l_ref[...] + jnp.sum(pt, axis=0, keepdims=True)
            a_ref[...] = alpha * a_ref[...] + jnp.dot(vtj, pt.astype(BF16), preferred_element_type=F32)
            m_ref[...] = m_new

    def body(j, carry):
        block(j, False)
        return carry

    lax.fori_loop(0, qi, body, 0)
    block(qi, True)

    lam = _lambda_value(lq1_ref[...], lk1_ref[...], lq2_ref[...], lk2_ref[...], lam_init)
    ot = a0_ref[...] * (1.0 / l0_ref[...]) - a1_ref[...] * (lam / l1_ref[...])
    ms = jnp.mean(ot * ot, axis=0, keepdims=True)
    ot = ot * lax.rsqrt(ms + EPS) * gs_ref[...] * (1.0 - lam_init)
    o_ref[...] = ot.T.astype(o_ref.dtype)


def _prompt_attention(q, k, v, lams, g_sub_col, batch, seq, tq, lam_init):
    nq = seq // tq
    small = lambda shape: pl.BlockSpec(shape, lambda b, h, i: (0, 0))
    kernel = functools.partial(_prompt_attn_kernel, tq=tq, lam_init=lam_init)
    return pl.pallas_call(
        kernel,
        out_shape=jax.ShapeDtypeStruct(q.shape, BF16),
        grid=(batch, N_HEADS, nq),
        in_specs=[small((1, HEAD_DIM))] * 4 + [small((V_DIM, 1))] + [
            pl.BlockSpec((tq, V_DIM), lambda b, h, i: (b * nq + i, h)),
            pl.BlockSpec((seq, V_DIM), lambda b, h, i: (b, h)),
            pl.BlockSpec((seq, V_DIM), lambda b, h, i: (b, h))],
        out_specs=pl.BlockSpec((tq, V_DIM), lambda b, h, i: (b * nq + i, h)),
        scratch_shapes=[pltpu.VMEM((nq, V_DIM, tq), BF16)]
                       + [pltpu.VMEM((1, tq), F32), pltpu.VMEM((1, tq), F32), pltpu.VMEM((V_DIM, tq), F32)] * 2,
        compiler_params=_cparams(("arbitrary", "arbitrary", "arbitrary")),
        name="prompt_attention",
    )(*lams, g_sub_col, q, k, v)


def _sample_scores_kernel(pt_ref, qrows_ref, *rest, pages_per_step):
    k_refs = rest[:pages_per_step]
    s_ref = rest[pages_per_step]
    qrows = qrows_ref[0]
    _, _, page = k_refs[0].shape
    for i, k_ref in enumerate(k_refs):
        kt = k_ref[...].reshape(N_QK_HEADS * HEAD_DIM, page)
        s_ref[0, :, i * page:(i + 1) * page] = jnp.dot(qrows, kt, preferred_element_type=F32)


def _sample_scores(page_table, qrows, cache_kt, li, pages_per_step):
    db, n_pages = page_table.shape
    page = cache_kt.shape[4]
    steps = n_pages // pages_per_step

    def k_spec(i):
        return pl.BlockSpec((None, None, N_QK_HEADS, HEAD_DIM, page),
                            lambda b, c, pt: (li, pt[b, c * pages_per_step + i], 0, 0, 0))

    grid_spec = pltpu.PrefetchScalarGridSpec(
        num_scalar_prefetch=1,
        grid=(db, steps),
        in_specs=[pl.BlockSpec((1, N_QK_HEADS, ATTN_WIDTH), lambda b, c, pt: (b, 0, 0))]
                 + [k_spec(i) for i in range(pages_per_step)],
        out_specs=pl.BlockSpec((1, N_QK_HEADS, pages_per_step * page), lambda b, c, pt: (b, 0, c)),
    )
    return pl.pallas_call(
        functools.partial(_sample_scores_kernel, pages_per_step=pages_per_step),
        out_shape=jax.ShapeDtypeStruct((db, N_QK_HEADS, n_pages * page), F32),
        grid_spec=grid_spec,
        compiler_params=_cparams(("arbitrary", "arbitrary")),
        name="sample_scores",
    )(page_table, qrows, *([cache_kt] * pages_per_step))


def _sample_pv_kernel(pt_ref, lq1_ref, lk1_ref, lq2_ref, lk2_ref, gs_ref, s_ref, qrows_ref, kn_ref, vn_ref,
                      *rest, pages_per_step, lam_init):
    v_refs = rest[:pages_per_step]
    o_ref = rest[pages_per_step]
    wsel_ref, wn_ref, acc_ref = rest[pages_per_step + 1:]
    step = pl.program_id(1)
    steps = pl.num_programs(1)
    page = v_refs[0].shape[0]
    n_pages = s_ref.shape[2] // page

    @pl.when(step == 0)
    def _():
        s = s_ref[0]
        s_new = jnp.sum(qrows_ref[0] * kn_ref[0], axis=-1, keepdims=True)
        m = jnp.maximum(jnp.max(s, axis=-1, keepdims=True), s_new)
        p = jnp.exp(s - m)
        p_new = jnp.exp(s_new - m)
        l = jnp.sum(p, axis=-1, keepdims=True) + p_new
        lam = _lambda_value(lq1_ref[...], lk1_ref[...], lq2_ref[...], lk2_ref[...], lam_init)
        a = p / l
        a_new = p_new / l
        w = a[:N_HEADS] - lam * a[N_HEADS:]
        wn_ref[...] = a_new[:N_HEADS] - lam * a_new[N_HEADS:]
        w_rows = jnp.concatenate([w[:, g * page:(g + 1) * page] for g in range(n_pages)], axis=0)
        tok = lax.broadcasted_iota(jnp.int32, (page, page * N_HEADS), 0)
        dst = lax.broadcasted_iota(jnp.int32, (page, page * N_HEADS), 1)
        spread = (dst // N_HEADS == tok).astype(F32)
        wide = jnp.dot(w_rows, spread, preferred_element_type=F32)
        row = lax.broadcasted_iota(jnp.int32, wide.shape, 0)
        col = lax.broadcasted_iota(jnp.int32, wide.shape, 1)
        wsel_ref[...] = jnp.where(row % N_HEADS == col % N_HEADS, wide, 0.0)
        acc_ref[...] = jnp.zeros(acc_ref.shape, F32)

    acc = acc_ref[...]
    for i, v_ref in enumerate(v_refs):
        first = pl.multiple_of((step * pages_per_step + i) * N_HEADS, N_HEADS)
        wsel = wsel_ref[pl.ds(first, N_HEADS), :]
        acc += jnp.dot(wsel, v_ref[...].reshape(page * N_HEADS, V_DIM), preferred_element_type=F32)
    acc_ref[...] = acc

    @pl.when(step == steps - 1)
    def _():
        full = acc_ref[...] + wn_ref[...] * vn_ref[0]
        o_ref[0] = _sub_norm(full, gs_ref[...], lam_init).astype(o_ref.dtype)


def _sample_pv(page_table, lams, g_sub, scores, qrows, k_new, v_new, cache_v, li, pages_per_step, lam_init):
    db, n_pages = page_table.shape
    page = cache_v.shape[2]
    steps = n_pages // pages_per_step
    small = lambda shape: pl.BlockSpec(shape, lambda b, c, pt: (0, 0))
    per_b = lambda shape: pl.BlockSpec(shape, lambda b, c, pt: (b, 0, 0))

    def v_spec(i):
        return pl.BlockSpec((None, None, page, N_HEADS, V_DIM),
                            lambda b, c, pt: (li, pt[b, c * pages_per_step + i], 0, 0, 0))

    grid_spec = pltpu.PrefetchScalarGridSpec(
        num_scalar_prefetch=1,
        grid=(db, steps),
        in_specs=[small((1, HEAD_DIM))] * 4 + [small((1, V_DIM)),
                  per_b((1, N_QK_HEADS, n_pages * page)),
                  per_b((1, N_QK_HEADS, ATTN_WIDTH)),
                  per_b((1, 1, ATTN_WIDTH)),
                  per_b((1, N_HEADS, V_DIM))] + [v_spec(i) for i in range(pages_per_step)],
        out_specs=per_b((1, N_HEADS, V_DIM)),
        scratch_shapes=[pltpu.VMEM((n_pages * N_HEADS, page * N_HEADS), F32),
                        pltpu.VMEM((N_HEADS, 1), F32),
                        pltpu.VMEM((N_HEADS, V_DIM), F32)],
    )
    return pl.pallas_call(
        functools.partial(_sample_pv_kernel, pages_per_step=pages_per_step, lam_init=lam_init),
        out_shape=jax.ShapeDtypeStruct((db, N_HEADS, V_DIM), BF16),
        grid_spec=grid_spec,
        compiler_params=_cparams(("arbitrary", "arbitrary")),
        name="sample_pv",
    )(page_table, *lams, g_sub, scores, qrows, k_new, v_new, *([cache_v] * pages_per_step))


def _pool_matmul(pooled_groups, wp_ref, scale):
    outs = [jnp.dot(pg.astype(BF16), wp_ref[g], preferred_element_type=F32)
            for g, pg in enumerate(pooled_groups)]
    return jnp.concatenate(outs, axis=1) * scale


def _pool_prompt_kernel(u_ref, halo_ref, wp_ref, sc_ref, o_ref, ext_ref, *, ts):
    i = pl.program_id(1)
    group = wp_ref.shape[1]
    u = u_ref[...]
    ext_ref[pl.ds(POOL_HALO, ts), :] = u
    ext_ref[pl.ds(0, POOL_HALO), :] = jnp.where(i > 0, halo_ref[...], 0.0)
    pos1 = (i * ts + lax.broadcasted_iota(jnp.int32, (ts, 1), 0) + 1).astype(F32)
    pooled = []
    for g, win in enumerate(POOL_WINDOWS):
        cols = slice(g * group, (g + 1) * group)
        wsum = u[:, cols]
        for k in range(1, win):
            wsum = wsum + ext_ref[pl.ds(POOL_HALO - k, ts), cols]
        cnt = jnp.minimum(pos1, float(win))
        pooled.append(wsum / cnt - u[:, cols])
    o_ref[...] = _pool_matmul(pooled, wp_ref, sc_ref[...]).astype(o_ref.dtype)


def _pool_prompt(u, w_pool_bf, pool_scale, batch, seq, ts):
    m, width = u.shape
    nt = seq // ts
    r = ts // POOL_HALO
    return pl.pallas_call(
        functools.partial(_pool_prompt_kernel, ts=ts),
        out_shape=jax.ShapeDtypeStruct((m, width), BF16),
        grid=(batch, nt),
        in_specs=[pl.BlockSpec((ts, width), lambda b, i: (b * nt + i, 0)),
                  pl.BlockSpec((POOL_HALO, width), lambda b, i: (jnp.maximum((b * nt + i) * r - 1, 0), 0)),
                  pl.BlockSpec(w_pool_bf.shape, lambda b, i: (0, 0, 0)),
                  pl.BlockSpec((1, width), lambda b, i: (0, 0))],
        out_specs=pl.BlockSpec((ts, width), lambda b, i: (b * nt + i, 0)),
        scratch_shapes=[pltpu.VMEM((ts + POOL_HALO, width), F32)],
        compiler_params=_cparams(("arbitrary", "arbitrary")),
        name="pool_prompt",
    )(u, u, w_pool_bf, pool_scale)


def _pool_sample_kernel(u_ref, st_ref, wp_ref, sc_ref, o_ref):
    group = wp_ref.shape[1]
    u = u_ref[...]
    pooled = []
    for g, win in enumerate(POOL_WINDOWS):
        cols = slice(g * group, (g + 1) * group)
        wsum = u[:, cols]
        for k in range(1, win):
            wsum = wsum + st_ref[POOL_STATE - k, :, cols]
        pooled.append(wsum / float(win) - u[:, cols])
    o_ref[...] = _pool_matmul(pooled, wp_ref, sc_ref[...]).astype(o_ref.dtype)


def _pool_sample(u, state_t, w_pool_bf, pool_scale, tb):
    m, width = u.shape
    return pl.pallas_call(
        _pool_sample_kernel,
        out_shape=jax.ShapeDtypeStruct((m, width), BF16),
        grid=(m // tb,),
        in_specs=[pl.BlockSpec((tb, width), lambda i: (i, 0)),
                  pl.BlockSpec((POOL_STATE, tb, width), lambda i: (0, i, 0)),
                  pl.BlockSpec(w_pool_bf.shape, lambda i: (0, 0, 0)),
                  pl.BlockSpec((1, width), lambda i: (0, 0))],
        out_specs=pl.BlockSpec((tb, width), lambda i: (i, 0)),
        compiler_params=_cparams(("arbitrary",)),
        name="pool_sample",
    )(u, state_t, w_pool_bf, pool_scale)


def _out_proj_kernel(o_ref, p_ref, w_ref, h_ref, out_ref):
    half = o_ref.shape[1]
    acc = jnp.dot(o_ref[...], w_ref[pl.ds(0, half), :], preferred_element_type=F32)
    acc += jnp.dot(p_ref[...], w_ref[pl.ds(half, half), :], preferred_element_type=F32)
    out_ref[...] = h_ref[...] + acc


def _out_proj(o, pool, w_out_bf, h, tm):
    m, d = h.shape
    row = lambda i: (i, 0)
    return pl.pallas_call(
        _out_proj_kernel,
        out_shape=jax.ShapeDtypeStruct((m, d), F32),
        grid=(m // tm,),
        in_specs=[pl.BlockSpec((tm, o.shape[1]), row),
                  pl.BlockSpec((tm, pool.shape[1]), row),
                  pl.BlockSpec(w_out_bf.shape, lambda i: (0, 0)),
                  pl.BlockSpec((tm, d), row)],
        out_specs=pl.BlockSpec((tm, d), row),
        compiler_params=_cparams(("arbitrary",)),
        name="out_proj",
    )(o, pool, w_out_bf, h)


def _ffn_kernel(h_ref, g_ref, wg_ref, wu_ref, wd_ref, out_ref, xn_ref, acc_ref):
    f = pl.program_id(1)

    @pl.when(f == 0)
    def _():
        xn_ref[...] = _rms(h_ref[...], g_ref[...]).astype(BF16)
        acc_ref[...] = jnp.zeros(acc_ref.shape, F32)

    xn = xn_ref[...]
    gate = jnp.dot(xn, wg_ref[...], preferred_element_type=F32)
    up = jnp.dot(xn, wu_ref[...], preferred_element_type=F32)
    act = (gate * jax.nn.sigmoid(gate) * up).astype(BF16)
    acc_ref[...] += jnp.dot(act, wd_ref[...], preferred_element_type=F32)

    @pl.when(f == pl.num_programs(1) - 1)
    def _():
        out_ref[...] = h_ref[...] + acc_ref[...]


def _ffn(h, g_ffn, w_gate_up_bf, w_down_bf, tm, tf):
    m, d = h.shape
    d_ff = w_down_bf.shape[0]
    nf = d_ff // tf
    row = lambda i, f: (i, 0)
    return pl.pallas_call(
        _ffn_kernel,
        out_shape=jax.ShapeDtypeStruct((m, d), F32),
        grid=(m // tm, nf),
        in_specs=[pl.BlockSpec((tm, d), row),
                  pl.BlockSpec((1, d), lambda i, f: (0, 0)),
                  pl.BlockSpec((d, tf), lambda i, f: (0, f)),
                  pl.BlockSpec((d, tf), lambda i, f: (0, nf + f)),
                  pl.BlockSpec((tf, d), lambda i, f: (f, 0))],
        out_specs=pl.BlockSpec((tm, d), row),
        scratch_shapes=[pltpu.VMEM((tm, d), BF16), pltpu.VMEM((tm, d), F32)],
        compiler_params=_cparams(("arbitrary", "arbitrary")),
        name="ffn",
    )(h, g_ffn, w_gate_up_bf, w_gate_up_bf, w_down_bf)


def _ple_kernel(h_ref, p_ref, gp_ref, wg_ref, wp_ref, gf_ref, y_ref, *, final_norm):
    h = h_ref[...]
    xn = _rms(h, gp_ref[...]).astype(BF16)
    gate = jax.nn.sigmoid(jnp.dot(xn, wg_ref[...], preferred_element_type=F32))
    proj = jnp.dot(p_ref[...].astype(BF16), wp_ref[...], preferred_element_type=F32)
    y = h + gate * proj
    y_ref[...] = _rms(y, gf_ref[...]) if final_norm else y


def _ple(h, p, g_ple, w_gate_bf, w_proj_bf, g_final, tm, final_norm):
    m, d = h.shape
    row = lambda i: (i, 0)
    const = lambda i: (0, 0)
    return pl.pallas_call(
        functools.partial(_ple_kernel, final_norm=final_norm),
        out_shape=jax.ShapeDtypeStruct((m, d), F32),
        grid=(m // tm,),
        in_specs=[pl.BlockSpec((tm, d), row),
                  pl.BlockSpec((tm, p.shape[1]), row),
                  pl.BlockSpec((1, d), const),
                  pl.BlockSpec(w_gate_bf.shape, const),
                  pl.BlockSpec(w_proj_bf.shape, const),
                  pl.BlockSpec((1, d), const)],
        out_specs=pl.BlockSpec((tm, d), row),
        compiler_params=_cparams(("arbitrary",)),
        name="ple_final",
    )(h, p, g_ple, w_gate_bf, w_proj_bf, g_final)


def _rope_tables(positions):
    half = HEAD_DIM // 2
    inv = ROPE_THETA ** (-jnp.arange(half, dtype=F32) / half)
    ang = positions.astype(F32)[:, None] * inv[None, :]
    cos, sin = jnp.cos(ang), jnp.sin(ang)
    reps = LANES // HEAD_DIM
    return jnp.tile(jnp.concatenate([cos, cos], axis=1), (1, reps)), \
        jnp.tile(jnp.concatenate([-sin, sin], axis=1), (1, reps))


def _row_tile(m, want):
    return want if m % want == 0 else m


def kernel(x_prompt, x_sample, p_prompt, p_sample, cache_k, cache_v, state_pool, page_table, g_mix, w_in, w_out, lambda_q1, lambda_k1, lambda_q2, lambda_k2, g_sub, w_pool, pool_scale, g_ffn, w_gate_up, w_down, g_ple, w_ple_gate, w_ple_proj, g_final):
    batch, seq, d = x_prompt.shape
    db, ds, _ = x_sample.shape
    depth = g_mix.shape[0]
    n_pages = page_table.shape[1]
    page = cache_k.shape[2]
    past = n_pages * page
    assert ds == 1 and d == 2 * ATTN_WIDTH

    hp = x_prompt.reshape(batch * seq, d)
    hs = x_sample.reshape(db * ds, d)
    cos_p, sin_p = _rope_tables(jnp.arange(seq))
    cos_s, sin_s = _rope_tables(jnp.full((db,), past))
    row2 = lambda a: a.reshape(1, -1)

    tm_p = _row_tile(batch * seq, 512)
    tq = _row_tile(seq, 512)
    ts = _row_tile(seq, 512)
    pages_k = 8 if n_pages % 8 == 0 else 1
    pages_v = 8 if n_pages % 8 == 0 else 1
    cache_kt = jnp.transpose(cache_k, (0, 1, 3, 4, 2))
    tf = 512
    qk_scale = HEAD_DIM ** -0.5

    outs = {k: [] for k in ("kp", "vp", "pp", "ks", "vs", "ps")}
    for li in range(depth):
        lam_init = 0.8 - 0.6 * math.exp(-0.3 * li)
        w_in_bf = w_in[li].astype(BF16)
        w_out_bf = w_out[li].astype(BF16)
        w_pool_bf = w_pool[li].astype(BF16)
        w_gu_bf = w_gate_up[li].astype(BF16)
        w_down_bf = w_down[li].astype(BF16)
        w_pg_bf = w_ple_gate[li].astype(BF16)
        w_pp_bf = w_ple_proj[li].astype(BF16)
        lams = tuple(row2(a[li]) for a in (lambda_q1, lambda_k1, lambda_q2, lambda_k2))
        gsub = row2(g_sub[li])

        q, kf, kb, vf, vb, u = _in_proj(hp, row2(g_mix[li]), w_in_bf, cos_p, sin_p, tm_p,
                                        seq // tm_p if seq % tm_p == 0 else 1, qk_scale * math.log2(math.e))
        o = _prompt_attention(q, kb, vb, lams, g_sub[li].reshape(-1, 1), batch, seq, tq, lam_init)
        pool = _pool_prompt(u, w_pool_bf, row2(pool_scale[li]), batch, seq, ts)
        h1 = _out_proj(o, pool, w_out_bf, hp, tm_p)
        h2 = _ffn(h1, row2(g_ffn[li]), w_gu_bf, w_down_bf, tm_p, tf)
        outs["kp"].append(kf.reshape(batch, seq, N_QK_HEADS, HEAD_DIM))
        outs["vp"].append(vf.reshape(batch, seq, N_HEADS, V_DIM))
        outs["pp"].append(u.reshape(batch, seq, -1)[:, seq - POOL_STATE:])
        p_rows = p_prompt[li].reshape(batch * seq, -1)
        last = li == depth - 1
        gfin = row2(g_final)
        hp_next = _ple(h2, p_rows, row2(g_ple[li]), w_pg_bf, w_pp_bf, gfin, tm_p, last)

        qs, kfs, _, vfs, _, us = _in_proj(hs, row2(g_mix[li]), w_in_bf, cos_s, sin_s, db, 1, qk_scale)
        col_sub = jnp.arange(ATTN_WIDTH) // HEAD_DIM
        row_sub = (jnp.arange(N_QK_HEADS) % N_HEADS) * 2 + jnp.arange(N_QK_HEADS) // N_HEADS
        qrows = qs.astype(F32)[:, None, :] * (col_sub[None, :] == row_sub[:, None]).astype(F32)[None]
        scores = _sample_scores(page_table, qrows, cache_kt, li, pages_k)
        o_s = _sample_pv(page_table, lams, gsub, scores, qrows, kfs.reshape(db, 1, ATTN_WIDTH),
                         vfs.reshape(db, N_HEADS, V_DIM), cache_v, li, pages_v, lam_init).reshape(db, -1)
        state = state_pool[li]
        pool_s = _pool_sample(us, jnp.swapaxes(state, 0, 1), w_pool_bf, row2(pool_scale[li]), db)
        h1s = _out_proj(o_s, pool_s, w_out_bf, hs, db)
        h2s = _ffn(h1s, row2(g_ffn[li]), w_gu_bf, w_down_bf, db, tf)
        hs_next = _ple(h2s, p_sample[li].reshape(db * ds, -1), row2(g_ple[li]), w_pg_bf, w_pp_bf, gfin, db, last)
        outs["ks"].append(kfs.reshape(db, ds, N_QK_HEADS, HEAD_DIM))
        outs["vs"].append(vfs.reshape(db, ds, N_HEADS, V_DIM))
        outs["ps"].append(jnp.concatenate([state[:, 1:], us[:, None, :]], axis=1))
        hp, hs = hp_next, hs_next

    y_prompt = hp.reshape(batch, seq, d)
    y_sample = hs.reshape(db, ds, d)
    return (y_prompt, y_sample, jnp.stack(outs["kp"]), jnp.stack(outs["vp"]), jnp.stack(outs["pp"]),
            jnp.stack(outs["ks"]), jnp.stack(outs["vs"]), jnp.stack(outs["ps"]))
```

```python
import functools
import math

import jax
import jax.numpy as jnp
from jax import lax
from jax.experimental import pallas as pl
from jax.experimental.pallas import tpu as pltpu

F32 = jnp.float32
BF16 = jnp.bfloat16

N_HEADS = 8
HEAD_DIM = 64
V_DIM = 2 * HEAD_DIM
N_QK_HEADS = 2 * N_HEADS
ATTN_WIDTH = N_HEADS * V_DIM
POOL_WINDOWS = (2, 4, 8, 16)
POOL_STATE = max(POOL_WINDOWS) - 1
POOL_HALO = 16
ROPE_THETA = 10000.0
EPS = 1e-6
LANES = 128
MXU_COLS = 256
VMEM_LIMIT = 56 * 1024 * 1024


def _cparams(sem):
    return pltpu.CompilerParams(dimension_semantics=sem, vmem_limit_bytes=VMEM_LIMIT)


def _rms(x, g):
    ms = jnp.mean(x * x, axis=-1, keepdims=True)
    return x * lax.rsqrt(ms + EPS) * g


def _rope_chunk(x, cos, sin_signed, first_half):
    swapped = jnp.where(first_half, pltpu.roll(x, 96, 1), pltpu.roll(x, 32, 1))
    return x * cos + swapped * sin_signed


def _col_chunks(xn_ref, w_ref):
    xn = xn_ref[...]
    for c in range(w_ref.shape[1] // MXU_COLS):
        cols = slice(c * MXU_COLS, (c + 1) * MXU_COLS)
        yield cols, jnp.dot(xn, w_ref[:, cols], preferred_element_type=F32)


def _rope(a, cos_ref, sin_ref):
    cos = cos_ref[...]
    sin = sin_ref[...]
    lane = lax.broadcasted_iota(jnp.int32, cos.shape, 1)
    first_half = (lane % HEAD_DIM) < (HEAD_DIM // 2)
    parts = [_rope_chunk(a[:, c * LANES:(c + 1) * LANES], cos, sin, first_half)
             for c in range(a.shape[1] // LANES)]
    return jnp.concatenate(parts, axis=1)


def _in_proj_prompt_kernel(x_ref, g_ref, w_ref, cos_ref, sin_ref,
                           qt_ref, kb_ref, kt_ref, vt_ref, vf_ref, u_ref, xn_ref, *, q_scale):
    n = pl.program_id(1)

    @pl.when(n == 0)
    def _():
        xn_ref[...] = _rms(x_ref[...], g_ref[...]).astype(BF16)
        for cols, acc in _col_chunks(xn_ref, w_ref):
            qt_ref[cols, :] = (_rope(acc, cos_ref, sin_ref) * q_scale).T.astype(BF16)

    @pl.when(n == 1)
    def _():
        for cols, acc in _col_chunks(xn_ref, w_ref):
            k = _rope(acc, cos_ref, sin_ref)
            kb_ref[:, cols] = k.astype(BF16)
            kt_ref[cols, :] = k.T

    @pl.when(n == 2)
    def _():
        for cols, acc in _col_chunks(xn_ref, w_ref):
            vf_ref[:, cols] = acc
            vt_ref[cols, :] = acc.T.astype(BF16)

    @pl.when(n == 3)
    def _():
        for cols, acc in _col_chunks(xn_ref, w_ref):
            u_ref[:, cols] = acc


def _in_proj_prompt(x, g_mix, w_in_bf, cos_t, sin_t, batch, seq, tm, q_scale):
    m, d = x.shape
    width = ATTN_WIDTH
    nt = seq // tm
    row = lambda i, n: (i, 0)
    tile = lambda i, n: (i, 0, 0)
    tab = lambda i, n: (i % nt, 0)
    outs = [jax.ShapeDtypeStruct((m // tm, width, tm), BF16),
            jax.ShapeDtypeStruct((m, width), BF16),
            jax.ShapeDtypeStruct((batch, width, seq), F32),
            jax.ShapeDtypeStruct((m // tm, width, tm), BF16),
            jax.ShapeDtypeStruct((m, width), F32),
            jax.ShapeDtypeStruct((m, width), F32)]
    return pl.pallas_call(
        functools.partial(_in_proj_prompt_kernel, q_scale=q_scale),
        out_shape=outs,
        grid=(m // tm, 4),
        in_specs=[pl.BlockSpec((tm, d), row),
                  pl.BlockSpec((1, d), lambda i, n: (0, 0)),
                  pl.BlockSpec((d, width), lambda i, n: (0, n)),
                  pl.BlockSpec((tm, LANES), tab),
                  pl.BlockSpec((tm, LANES), tab)],
        out_specs=[pl.BlockSpec((None, width, tm), tile),
                   pl.BlockSpec((tm, width), row),
                   pl.BlockSpec((None, width, tm), lambda i, n: (i // nt, 0, i % nt)),
                   pl.BlockSpec((None, width, tm), tile),
                   pl.BlockSpec((tm, width), row),
                   pl.BlockSpec((tm, width), row)],
        scratch_shapes=[pltpu.VMEM((tm, d), BF16)],
        compiler_params=_cparams(("arbitrary", "arbitrary")),
        name="in_proj_prompt",
    )(x, g_mix, w_in_bf, cos_t, sin_t)


def _in_proj_sample_kernel(x_ref, g_ref, w_ref, cos_ref, sin_ref, q_ref, k_ref, v_ref, u_ref, xn_ref, *, q_scale):
    n = pl.program_id(1)

    @pl.when(n == 0)
    def _():
        xn_ref[...] = _rms(x_ref[...], g_ref[...]).astype(BF16)

    acc = jnp.dot(xn_ref[...], w_ref[...], preferred_element_type=F32)

    @pl.when(n == 0)
    def _():
        q_ref[...] = (_rope(acc, cos_ref, sin_ref) * q_scale).astype(BF16)

    @pl.when(n == 1)
    def _():
        k_ref[...] = _rope(acc, cos_ref, sin_ref)

    @pl.when(n == 2)
    def _():
        v_ref[...] = acc

    @pl.when(n == 3)
    def _():
        u_ref[...] = acc


def _in_proj_sample(x, g_mix, w_in_bf, cos_t, sin_t, q_scale):
    m, d = x.shape
    width = ATTN_WIDTH
    whole = lambda i, n: (0, 0)
    outs = [jax.ShapeDtypeStruct((m, width), BF16)] + [jax.ShapeDtypeStruct((m, width), F32)] * 3
    return pl.pallas_call(
        functools.partial(_in_proj_sample_kernel, q_scale=q_scale),
        out_shape=outs,
        grid=(1, 4),
        in_specs=[pl.BlockSpec((m, d), whole),
                  pl.BlockSpec((1, d), whole),
                  pl.BlockSpec((d, width), lambda i, n: (0, n)),
                  pl.BlockSpec((m, LANES), whole),
                  pl.BlockSpec((m, LANES), whole)],
        out_specs=[pl.BlockSpec((m, width), whole)] * 4,
        scratch_shapes=[pltpu.VMEM((m, d), BF16)],
        compiler_params=_cparams(("arbitrary", "arbitrary")),
        name="in_proj_sample",
    )(x, g_mix, w_in_bf, cos_t, sin_t)


def _lambda_value(lq1, lk1, lq2, lk2, lam_init):
    a = jnp.sum(lq1 * lk1, axis=-1, keepdims=True)
    b = jnp.sum(lq2 * lk2, axis=-1, keepdims=True)
    return jnp.exp(a) - jnp.exp(b) + lam_init


def _sub_norm(o, g_sub, lam_init):
    return _rms(o, g_sub) * (1.0 - lam_init)


def _prompt_attn_kernel(lq1_ref, lk1_ref, lq2_ref, lk2_ref, gs_ref, qt_ref, k_ref, vt_ref, o_ref,
                        m0_ref, l0_ref, a0_ref, m1_ref, l1_ref, a1_ref, *, tq, lam_init):
    qi = pl.program_id(2)
    hq = tq // 2
    qt = qt_ref[...]
    drow = lax.broadcasted_iota(jnp.int32, qt.shape, 0)
    zero = jnp.zeros_like(qt)
    qts = (jnp.where(drow < HEAD_DIM, qt, zero), jnp.where(drow >= HEAD_DIM, qt, zero))
    stats = ((m0_ref, l0_ref, a0_ref), (m1_ref, l1_ref, a1_ref))

    for m_ref, l_ref, a_ref in stats:
        m_ref[...] = jnp.full(m_ref.shape, -jnp.inf, F32)
        l_ref[...] = jnp.zeros(l_ref.shape, F32)
        a_ref[...] = jnp.zeros(a_ref.shape, F32)

    def run_chains(chains):
        scores = [jnp.dot(kj, qc[:, lanes], preferred_element_type=F32)
                  for kj, _, qc, _, lanes, _ in chains]
        probs = []
        for st, (_, _, _, (m_ref, l_ref, _), lanes, keep) in zip(scores, chains):
            if keep is not None:
                st = jnp.where(keep, st, -jnp.inf)
            m_old = m_ref[:, lanes]
            m_new = jnp.maximum(m_old, jnp.max(st, axis=0, keepdims=True))
            pt = jnp.exp2(st - m_new)
            alpha = jnp.exp2(m_old - m_new)
            l_ref[:, lanes] = alpha * l_ref[:, lanes] + jnp.sum(pt, axis=0, keepdims=True)
            m_ref[:, lanes] = m_new
            probs.append((pt.astype(BF16), alpha))
        for (pt, alpha), (_, vtj, _, (_, _, a_ref), lanes, _) in zip(probs, chains):
            a_ref[:, lanes] = alpha * a_ref[:, lanes] + jnp.dot(vtj, pt, preferred_element_type=F32)

    def body(j, carry):
        kj = k_ref[pl.ds(pl.multiple_of(j * tq, tq), tq), :]
        vtj = vt_ref[j]
        run_chains([(kj, vtj, qc, refs, slice(half * hq, (half + 1) * hq), None)
                    for qc, refs in zip(qts, stats) for half in range(2)])
        return carry

    lax.fori_loop(0, qi, body, 0)

    start = pl.multiple_of(qi * tq, tq)
    k_lo = k_ref[pl.ds(start, hq), :]
    k_all = k_ref[pl.ds(start, tq), :]
    vt_all = vt_ref[qi]
    tri = (lax.broadcasted_iota(jnp.int32, (hq, hq), 0) <= lax.broadcasted_iota(jnp.int32, (hq, hq), 1))
    late = (lax.broadcasted_iota(jnp.int32, (tq, hq), 0) <= lax.broadcasted_iota(jnp.int32, (tq, hq), 1) + hq)
    diagonal = []
    for qc, refs in zip(qts, stats):
        diagonal.append((k_lo, vt_all[:, :hq], qc, refs, slice(0, hq), tri))
        diagonal.append((k_all, vt_all, qc, refs, slice(hq, tq), late))
    run_chains(diagonal)

    lam = _lambda_value(lq1_ref[...], lk1_ref[...], lq2_ref[...], lk2_ref[...], lam_init)
    ot = a0_ref[...] * (1.0 / l0_ref[...]) - a1_ref[...] * (lam / l1_ref[...])
    ms = jnp.mean(ot * ot, axis=0, keepdims=True)
    ot = ot * lax.rsqrt(ms + EPS) * gs_ref[...] * (1.0 - lam_init)
    o_ref[...] = ot.T.astype(o_ref.dtype)


def _prompt_attention(qt, k, vt, lams, g_sub_col, batch, seq, tq, lam_init):
    nq = seq // tq
    small = lambda shape: pl.BlockSpec(shape, lambda b, h, i: (0, 0))
    kernel = functools.partial(_prompt_attn_kernel, tq=tq, lam_init=lam_init)
    return pl.pallas_call(
        kernel,
        out_shape=jax.ShapeDtypeStruct(k.shape, BF16),
        grid=(batch, N_HEADS, nq),
        in_specs=[small((1, HEAD_DIM))] * 4 + [small((V_DIM, 1))] + [
            pl.BlockSpec((None, V_DIM, tq), lambda b, h, i: (b * nq + i, h, 0)),
            pl.BlockSpec((seq, V_DIM), lambda b, h, i: (b, h)),
            pl.BlockSpec((nq, V_DIM, tq), lambda b, h, i: (b, h, 0))],
        out_specs=pl.BlockSpec((tq, V_DIM), lambda b, h, i: (b * nq + i, h)),
        scratch_shapes=[pltpu.VMEM((1, tq), F32), pltpu.VMEM((1, tq), F32), pltpu.VMEM((V_DIM, tq), F32)] * 2,
        compiler_params=_cparams(("arbitrary", "arbitrary", "arbitrary")),
        name="prompt_attention",
    )(*lams, g_sub_col, qt, k, vt)


def _sample_scores_kernel(pt_ref, qrows_ref, *rest, pages_per_step):
    k_refs = rest[:pages_per_step]
    s_ref = rest[pages_per_step]
    qrows = qrows_ref[0]
    _, _, page = k_refs[0].shape
    for i, k_ref in enumerate(k_refs):
        kt = k_ref[...].reshape(N_QK_HEADS * HEAD_DIM, page)
        s_ref[0, :, i * page:(i + 1) * page] = jnp.dot(qrows, kt, preferred_element_type=F32)


def _sample_scores(page_table, qrows, cache_kt, li, pages_per_step):
    db, n_pages = page_table.shape
    page = cache_kt.shape[4]
    steps = n_pages // pages_per_step

    def k_spec(i):
        return pl.BlockSpec((None, None, N_QK_HEADS, HEAD_DIM, page),
                            lambda b, c, pt: (li, pt[b, c * pages_per_step + i], 0, 0, 0))

    grid_spec = pltpu.PrefetchScalarGridSpec(
        num_scalar_prefetch=1,
        grid=(db, steps),
        in_specs=[pl.BlockSpec((1, N_QK_HEADS, ATTN_WIDTH), lambda b, c, pt: (b, 0, 0))]
                 + [k_spec(i) for i in range(pages_per_step)],
        out_specs=pl.BlockSpec((1, N_QK_HEADS, pages_per_step * page), lambda b, c, pt: (b, 0, c)),
    )
    return pl.pallas_call(
        functools.partial(_sample_scores_kernel, pages_per_step=pages_per_step),
        out_shape=jax.ShapeDtypeStruct((db, N_QK_HEADS, n_pages * page), F32),
        grid_spec=grid_spec,
        compiler_params=_cparams(("arbitrary", "arbitrary")),
        name="sample_scores",
    )(page_table, qrows, *([cache_kt] * pages_per_step))


def _sample_pv_kernel(pt_ref, lq1_ref, lk1_ref, lq2_ref, lk2_ref, gs_ref, s_ref, qrows_ref, kn_ref, vn_ref,
                      *rest, pages_per_step, lam_init):
    v_refs = rest[:pages_per_step]
    o_ref = rest[pages_per_step]
    wsel_ref, wn_ref, acc_ref = rest[pages_per_step + 1:]
    step = pl.program_id(1)
    steps = pl.num_programs(1)
    page = v_refs[0].shape[0]
    n_pages = s_ref.shape[2] // page

    @pl.when(step == 0)
    def _():
        s = s_ref[0]
        s_new = jnp.sum(qrows_ref[0] * kn_ref[0], axis=-1, keepdims=True)
        m = jnp.maximum(jnp.max(s, axis=-1, keepdims=True), s_new)
        p = jnp.exp(s - m)
        p_new = jnp.exp(s_new - m)
        l = jnp.sum(p, axis=-1, keepdims=True) + p_new
        lam = _lambda_value(lq1_ref[...], lk1_ref[...], lq2_ref[...], lk2_ref[...], lam_init)
        a = p / l
        a_new = p_new / l
        w = a[:N_HEADS] - lam * a[N_HEADS:]
        wn_ref[...] = a_new[:N_HEADS] - lam * a_new[N_HEADS:]
        w_rows = jnp.concatenate([w[:, g * page:(g + 1) * page] for g in range(n_pages)], axis=0)
        tok = lax.broadcasted_iota(jnp.int32, (page, page * N_HEADS), 0)
        dst = lax.broadcasted_iota(jnp.int32, (page, page * N_HEADS), 1)
        spread = (dst // N_HEADS == tok).astype(F32)
        wide = jnp.dot(w_rows, spread, preferred_element_type=F32)
        row = lax.broadcasted_iota(jnp.int32, wide.shape, 0)
        col = lax.broadcasted_iota(jnp.int32, wide.shape, 1)
        wsel_ref[...] = jnp.where(row % N_HEADS == col % N_HEADS, wide, 0.0)
        acc_ref[...] = jnp.zeros(acc_ref.shape, F32)

    acc = acc_ref[...]
    for i, v_ref in enumerate(v_refs):
        first = pl.multiple_of((step * pages_per_step + i) * N_HEADS, N_HEADS)
        wsel = wsel_ref[pl.ds(first, N_HEADS), :]
        acc += jnp.dot(wsel, v_ref[...].reshape(page * N_HEADS, V_DIM), preferred_element_type=F32)
    acc_ref[...] = acc

    @pl.when(step == steps - 1)
    def _():
        full = acc_ref[...] + wn_ref[...] * vn_ref[0]
        o_ref[0] = _sub_norm(full, gs_ref[...], lam_init).astype(o_ref.dtype)


def _sample_pv(page_table, lams, g_sub, scores, qrows, k_new, v_new, cache_v, li, pages_per_step, lam_init):
    db, n_pages = page_table.shape
    page = cache_v.shape[2]
    steps = n_pages // pages_per_step
    small = lambda shape: pl.BlockSpec(shape, lambda b, c, pt: (0, 0))
    per_b = lambda shape: pl.BlockSpec(shape, lambda b, c, pt: (b, 0, 0))

    def v_spec(i):
        return pl.BlockSpec((None, None, page, N_HEADS, V_DIM),
                            lambda b, c, pt: (li, pt[b, c * pages_per_step + i], 0, 0, 0))

    grid_spec = pltpu.PrefetchScalarGridSpec(
        num_scalar_prefetch=1,
        grid=(db, steps),
        in_specs=[small((1, HEAD_DIM))] * 4 + [small((1, V_DIM)),
                  per_b((1, N_QK_HEADS, n_pages * page)),
                  per_b((1, N_QK_HEADS, ATTN_WIDTH)),
                  per_b((1, 1, ATTN_WIDTH)),
                  per_b((1, N_HEADS, V_DIM))] + [v_spec(i) for i in range(pages_per_step)],
        out_specs=per_b((1, N_HEADS, V_DIM)),
        scratch_shapes=[pltpu.VMEM((n_pages * N_HEADS, page * N_HEADS), F32),
                        pltpu.VMEM((N_HEADS, 1), F32),
                        pltpu.VMEM((N_HEADS, V_DIM), F32)],
    )
    return pl.pallas_call(
        functools.partial(_sample_pv_kernel, pages_per_step=pages_per_step, lam_init=lam_init),
        out_shape=jax.ShapeDtypeStruct((db, N_HEADS, V_DIM), BF16),
        grid_spec=grid_spec,
        compiler_params=_cparams(("arbitrary", "arbitrary")),
        name="sample_pv",
    )(page_table, *lams, g_sub, scores, qrows, k_new, v_new, *([cache_v] * pages_per_step))


def _pool_matmul(pooled_groups, wp_ref, scale):
    outs = [jnp.dot(pg.astype(BF16), wp_ref[g], preferred_element_type=F32)
            for g, pg in enumerate(pooled_groups)]
    return jnp.concatenate(outs, axis=1) * scale


def _pool_prompt_kernel(u_ref, halo_ref, wp_ref, sc_ref, o_ref, ext_ref, *, ts):
    i = pl.program_id(1)
    group = wp_ref.shape[1]
    u = u_ref[...]
    ext_ref[pl.ds(POOL_HALO, ts), :] = u
    ext_ref[pl.ds(0, POOL_HALO), :] = jnp.where(i > 0, halo_ref[...], 0.0)
    pos1 = (i * ts + lax.broadcasted_iota(jnp.int32, (ts, 1), 0) + 1).astype(F32)
    pooled = []
    for g, win in enumerate(POOL_WINDOWS):
        cols = slice(g * group, (g + 1) * group)
        wsum = u[:, cols]
        for k in range(1, win):
            wsum = wsum + ext_ref[pl.ds(POOL_HALO - k, ts), cols]
        cnt = jnp.minimum(pos1, float(win))
        pooled.append(wsum / cnt - u[:, cols])
    o_ref[...] = _pool_matmul(pooled, wp_ref, sc_ref[...]).astype(o_ref.dtype)


def _pool_prompt(u, w_pool_bf, pool_scale, batch, seq, ts):
    m, width = u.shape
    nt = seq // ts
    r = ts // POOL_HALO
    return pl.pallas_call(
        functools.partial(_pool_prompt_kernel, ts=ts),
        out_shape=jax.ShapeDtypeStruct((m, width), BF16),
        grid=(batch, nt),
        in_specs=[pl.BlockSpec((ts, width), lambda b, i: (b * nt + i, 0)),
                  pl.BlockSpec((POOL_HALO, width), lambda b, i: (jnp.maximum((b * nt + i) * r - 1, 0), 0)),
                  pl.BlockSpec(w_pool_bf.shape, lambda b, i: (0, 0, 0)),
                  pl.BlockSpec((1, width), lambda b, i: (0, 0))],
        out_specs=pl.BlockSpec((ts, width), lambda b, i: (b * nt + i, 0)),
        scratch_shapes=[pltpu.VMEM((ts + POOL_HALO, width), F32)],
        compiler_params=_cparams(("arbitrary", "arbitrary")),
        name="pool_prompt",
    )(u, u, w_pool_bf, pool_scale)


def _pool_sample_kernel(u_ref, st_ref, wp_ref, sc_ref, o_ref):
    group = wp_ref.shape[1]
    u = u_ref[...]
    pooled = []
    for g, win in enumerate(POOL_WINDOWS):
        cols = slice(g * group, (g + 1) * group)
        wsum = u[:, cols]
        for k in range(1, win):
            wsum = wsum + st_ref[POOL_STATE - k, :, cols]
        pooled.append(wsum / float(win) - u[:, cols])
    o_ref[...] = _pool_matmul(pooled, wp_ref, sc_ref[...]).astype(o_ref.dtype)


def _pool_sample(u, state_t, w_pool_bf, pool_scale, tb):
    m, width = u.shape
    return pl.pallas_call(
        _pool_sample_kernel,
        out_shape=jax.ShapeDtypeStruct((m, width), BF16),
        grid=(m // tb,),
        in_specs=[pl.BlockSpec((tb, width), lambda i: (i, 0)),
                  pl.BlockSpec((POOL_STATE, tb, width), lambda i: (0, i, 0)),
                  pl.BlockSpec(w_pool_bf.shape, lambda i: (0, 0, 0)),
                  pl.BlockSpec((1, width), lambda i: (0, 0))],
        out_specs=pl.BlockSpec((tb, width), lambda i: (i, 0)),
        compiler_params=_cparams(("arbitrary",)),
        name="pool_sample",
    )(u, state_t, w_pool_bf, pool_scale)


def _out_proj_kernel(o_ref, p_ref, w_ref, h_ref, out_ref):
    half = o_ref.shape[1]
    acc = jnp.dot(o_ref[...], w_ref[pl.ds(0, half), :], preferred_element_type=F32)
    acc += jnp.dot(p_ref[...], w_ref[pl.ds(half, half), :], preferred_element_type=F32)
    out_ref[...] = h_ref[...] + acc


def _out_proj(o, pool, w_out_bf, h, tm):
    m, d = h.shape
    row = lambda i: (i, 0)
    return pl.pallas_call(
        _out_proj_kernel,
        out_shape=jax.ShapeDtypeStruct((m, d), F32),
        grid=(m // tm,),
        in_specs=[pl.BlockSpec((tm, o.shape[1]), row),
                  pl.BlockSpec((tm, pool.shape[1]), row),
                  pl.BlockSpec(w_out_bf.shape, lambda i: (0, 0)),
                  pl.BlockSpec((tm, d), row)],
        out_specs=pl.BlockSpec((tm, d), row),
        compiler_params=_cparams(("arbitrary",)),
        name="out_proj",
    )(o, pool, w_out_bf, h)


def _ffn_kernel(h_ref, g_ref, wg_ref, wu_ref, wd_ref, out_ref, xn_ref, acc_ref):
    f = pl.program_id(1)

    @pl.when(f == 0)
    def _():
        xn_ref[...] = _rms(h_ref[...], g_ref[...]).astype(BF16)
        acc_ref[...] = jnp.zeros(acc_ref.shape, F32)

    xn = xn_ref[...]
    gate = jnp.dot(xn, wg_ref[...], preferred_element_type=F32)
    up = jnp.dot(xn, wu_ref[...], preferred_element_type=F32)
    act = (gate * jax.nn.sigmoid(gate) * up).astype(BF16)
    acc_ref[...] += jnp.dot(act, wd_ref[...], preferred_element_type=F32)

    @pl.when(f == pl.num_programs(1) - 1)
    def _():
        out_ref[...] = h_ref[...] + acc_ref[...]


def _ffn(h, g_ffn, w_gate_up_bf, w_down_bf, tm, tf):
    m, d = h.shape
    d_ff = w_down_bf.shape[0]
    nf = d_ff // tf
    row = lambda i, f: (i, 0)
    return pl.pallas_call(
        _ffn_kernel,
        out_shape=jax.ShapeDtypeStruct((m, d), F32),
        grid=(m // tm, nf),
        in_specs=[pl.BlockSpec((tm, d), row),
                  pl.BlockSpec((1, d), lambda i, f: (0, 0)),
                  pl.BlockSpec((d, tf), lambda i, f: (0, f)),
                  pl.BlockSpec((d, tf), lambda i, f: (0, nf + f)),
                  pl.BlockSpec((tf, d), lambda i, f: (f, 0))],
        out_specs=pl.BlockSpec((tm, d), row),
        scratch_shapes=[pltpu.VMEM((tm, d), BF16), pltpu.VMEM((tm, d), F32)],
        compiler_params=_cparams(("arbitrary", "arbitrary")),
        name="ffn",
    )(h, g_ffn, w_gate_up_bf, w_gate_up_bf, w_down_bf)


def _ple_kernel(h_ref, p_ref, gp_ref, wg_ref, wp_ref, gf_ref, y_ref, *, final_norm):
    h = h_ref[...]
    xn = _rms(h, gp_ref[...]).astype(BF16)
    gate = jax.nn.sigmoid(jnp.dot(xn, wg_ref[...], preferred_element_type=F32))
    proj = jnp.dot(p_ref[...].astype(BF16), wp_ref[...], preferred_element_type=F32)
    y = h + gate * proj
    y_ref[...] = _rms(y, gf_ref[...]) if final_norm else y


def _ple(h, p, g_ple, w_gate_bf, w_proj_bf, g_final, tm, final_norm):
    m, d = h.shape
    row = lambda i: (i, 0)
    const = lambda i: (0, 0)
    return pl.pallas_call(
        functools.partial(_ple_kernel, final_norm=final_norm),
        out_shape=jax.ShapeDtypeStruct((m, d), F32),
        grid=(m // tm,),
        in_specs=[pl.BlockSpec((tm, d), row),
                  pl.BlockSpec((tm, p.shape[1]), row),
                  pl.BlockSpec((1, d), const),
                  pl.BlockSpec(w_gate_bf.shape, const),
                  pl.BlockSpec(w_proj_bf.shape, const),
                  pl.BlockSpec((1, d), const)],
        out_specs=pl.BlockSpec((tm, d), row),
        compiler_params=_cparams(("arbitrary",)),
        name="ple_final",
    )(h, p, g_ple, w_gate_bf, w_proj_bf, g_final)


def _rope_tables(positions):
    half = HEAD_DIM // 2
    inv = ROPE_THETA ** (-jnp.arange(half, dtype=F32) / half)
    ang = positions.astype(F32)[:, None] * inv[None, :]
    cos, sin = jnp.cos(ang), jnp.sin(ang)
    reps = LANES // HEAD_DIM
    return jnp.tile(jnp.concatenate([cos, cos], axis=1), (1, reps)), \
        jnp.tile(jnp.concatenate([-sin, sin], axis=1), (1, reps))


def _row_tile(m, want):
    return want if m % want == 0 else m


def kernel(x_prompt, x_sample, p_prompt, p_sample, cache_k, cache_v, state_pool, page_table, g_mix, w_in, w_out, lambda_q1, lambda_k1, lambda_q2, lambda_k2, g_sub, w_pool, pool_scale, g_ffn, w_gate_up, w_down, g_ple, w_ple_gate, w_ple_proj, g_final):
    batch, seq, d = x_prompt.shape
    db, ds, _ = x_sample.shape
    depth = g_mix.shape[0]
    n_pages = page_table.shape[1]
    page = cache_k.shape[2]
    past = n_pages * page
    assert ds == 1 and d == 2 * ATTN_WIDTH

    hp = x_prompt.reshape(batch * seq, d)
    hs = x_sample.reshape(db * ds, d)
    cos_p, sin_p = _rope_tables(jnp.arange(seq))
    cos_s, sin_s = _rope_tables(jnp.full((db,), past))
    row2 = lambda a: a.reshape(1, -1)

    tm_p = _row_tile(batch * seq, 512)
    tq = _row_tile(seq, 512)
    ts = _row_tile(seq, 512)
    pages_k = 8 if n_pages % 8 == 0 else 1
    pages_v = 8 if n_pages % 8 == 0 else 1
    cache_kt = jnp.transpose(cache_k, (0, 1, 3, 4, 2))
    tf = 512
    qk_scale = HEAD_DIM ** -0.5

    outs = {k: [] for k in ("kp", "vp", "pp", "ks", "vs", "ps")}
    for li in range(depth):
        lam_init = 0.8 - 0.6 * math.exp(-0.3 * li)
        w_in_bf = w_in[li].astype(BF16)
        w_out_bf = w_out[li].astype(BF16)
        w_pool_bf = w_pool[li].astype(BF16)
        w_gu_bf = w_gate_up[li].astype(BF16)
        w_down_bf = w_down[li].astype(BF16)
        w_pg_bf = w_ple_gate[li].astype(BF16)
        w_pp_bf = w_ple_proj[li].astype(BF16)
        lams = tuple(row2(a[li]) for a in (lambda_q1, lambda_k1, lambda_q2, lambda_k2))
        gsub = row2(g_sub[li])

        qt, kb, kt, vt, vf, u = _in_proj_prompt(hp, row2(g_mix[li]), w_in_bf, cos_p, sin_p, batch, seq, tq,
                                                qk_scale * math.log2(math.e))
        o = _prompt_attention(qt, kb, vt, lams, g_sub[li].reshape(-1, 1), batch, seq, tq, lam_init)
        pool = _pool_prompt(u, w_pool_bf, row2(pool_scale[li]), batch, seq, ts)
        h1 = _out_proj(o, pool, w_out_bf, hp, tm_p)
        h2 = _ffn(h1, row2(g_ffn[li]), w_gu_bf, w_down_bf, tm_p, tf)
        outs["kp"].append(jnp.transpose(kt.reshape(batch, N_QK_HEADS, HEAD_DIM, seq), (0, 3, 1, 2)))
        outs["vp"].append(vf.reshape(batch, seq, N_HEADS, V_DIM))
        outs["pp"].append(u.reshape(batch, seq, -1)[:, seq - POOL_STATE:])
        p_rows = p_prompt[li].reshape(batch * seq, -1)
        last = li == depth - 1
        gfin = row2(g_final)
        hp_next = _ple(h2, p_rows, row2(g_ple[li]), w_pg_bf, w_pp_bf, gfin, tm_p, last)

        qs, kfs, vfs, us = _in_proj_sample(hs, row2(g_mix[li]), w_in_bf, cos_s, sin_s, qk_scale)
        col_sub = jnp.arange(ATTN_WIDTH) // HEAD_DIM
        row_sub = (jnp.arange(N_QK_HEADS) % N_HEADS) * 2 + jnp.arange(N_QK_HEADS) // N_HEADS
        qrows = qs.astype(F32)[:, None, :] * (col_sub[None, :] == row_sub[:, None]).astype(F32)[None]
        scores = _sample_scores(page_table, qrows, cache_kt, li, pages_k)
        o_s = _sample_pv(page_table, lams, gsub, scores, qrows, kfs.reshape(db, 1, ATTN_WIDTH),
                         vfs.reshape(db, N_HEADS, V_DIM), cache_v, li, pages_v, lam_init).reshape(db, -1)
        state = state_pool[li]
        pool_s = _pool_sample(us, jnp.swapaxes(state, 0, 1), w_pool_bf, row2(pool_scale[li]), db)
        h1s = _out_proj(o_s, pool_s, w_out_bf, hs, db)
        h2s = _ffn(h1s, row2(g_ffn[li]), w_gu_bf, w_down_bf, db, tf)
        hs_next = _ple(h2s, p_sample[li].reshape(db * ds, -1), row2(g_ple[li]), w_pg_bf, w_pp_bf, gfin, db, last)
        outs["ks"].append(kfs.reshape(db, ds, N_QK_HEADS, HEAD_DIM))
        outs["vs"].append(vfs.reshape(db, ds, N_HEADS, V_DIM))
        outs["ps"].append(jnp.concatenate([state[:, 1:], us[:, None, :]], axis=1))
        hp, hs = hp_next, hs_next

    y_prompt = hp.reshape(batch, seq, d)
    y_sample = hs.reshape(db, ds, d)
    return (y_prompt, y_sample, jnp.stack(outs["kp"]), jnp.stack(outs["vp"]), jnp.stack(outs["pp"]),
            jnp.stack(outs["ks"]), jnp.stack(outs["vs"]), jnp.stack(outs["ps"]))
```

```python
import functools
import math

import jax
import jax.numpy as jnp
from jax import lax
from jax.experimental import pallas as pl
from jax.experimental.pallas import tpu as pltpu

F32 = jnp.float32
BF16 = jnp.bfloat16

N_HEADS = 8
HEAD_DIM = 64
V_DIM = 2 * HEAD_DIM
N_QK_HEADS = 2 * N_HEADS
ATTN_WIDTH = N_HEADS * V_DIM
POOL_WINDOWS = (2, 4, 8, 16)
POOL_STATE = max(POOL_WINDOWS) - 1
POOL_HALO = 16
ROPE_THETA = 10000.0
EPS = 1e-6
LANES = 128
MXU_COLS = 256
VMEM_LIMIT = 56 * 1024 * 1024


def _cparams(sem):
    return pltpu.CompilerParams(dimension_semantics=sem, vmem_limit_bytes=VMEM_LIMIT)


def _rms(x, g):
    ms = jnp.mean(x * x, axis=-1, keepdims=True)
    return x * lax.rsqrt(ms + EPS) * g


def _rope_chunk(x, cos, sin_signed, first_half):
    swapped = jnp.where(first_half, pltpu.roll(x, 96, 1), pltpu.roll(x, 32, 1))
    return x * cos + swapped * sin_signed


def _col_chunks(xn_ref, w_ref):
    xn = xn_ref[...]
    for c in range(w_ref.shape[1] // MXU_COLS):
        cols = slice(c * MXU_COLS, (c + 1) * MXU_COLS)
        yield cols, jnp.dot(xn, w_ref[:, cols], preferred_element_type=F32)


def _rope(a, cos_ref, sin_ref):
    cos = cos_ref[...]
    sin = sin_ref[...]
    lane = lax.broadcasted_iota(jnp.int32, cos.shape, 1)
    first_half = (lane % HEAD_DIM) < (HEAD_DIM // 2)
    parts = [_rope_chunk(a[:, c * LANES:(c + 1) * LANES], cos, sin, first_half)
             for c in range(a.shape[1] // LANES)]
    return jnp.concatenate(parts, axis=1)


def _in_proj_prompt_kernel(x_ref, g_ref, w_ref, cos_ref, sin_ref,
                           qt_ref, kb_ref, kt_ref, vt_ref, vf_ref, u_ref, xn_ref, *, q_scale):
    n = pl.program_id(1)

    @pl.when(n == 0)
    def _():
        xn_ref[...] = _rms(x_ref[...], g_ref[...]).astype(BF16)
        for cols, acc in _col_chunks(xn_ref, w_ref):
            qt_ref[cols, :] = (_rope(acc, cos_ref, sin_ref) * q_scale).T.astype(BF16)

    @pl.when(n == 1)
    def _():
        for cols, acc in _col_chunks(xn_ref, w_ref):
            k = _rope(acc, cos_ref, sin_ref)
            kb_ref[:, cols] = k.astype(BF16)
            kt_ref[cols, :] = k.T

    @pl.when(n == 2)
    def _():
        for cols, acc in _col_chunks(xn_ref, w_ref):
            vf_ref[:, cols] = acc
            vt_ref[cols, :] = acc.T.astype(BF16)

    @pl.when(n == 3)
    def _():
        for cols, acc in _col_chunks(xn_ref, w_ref):
            u_ref[:, cols] = acc


def _in_proj_prompt(x, g_mix, w_in_bf, cos_t, sin_t, batch, seq, tm, q_scale):
    m, d = x.shape
    width = ATTN_WIDTH
    nt = seq // tm
    row = lambda i, n: (i, 0)
    tile = lambda i, n: (i, 0, 0)
    tab = lambda i, n: (i % nt, 0)
    outs = [jax.ShapeDtypeStruct((m // tm, width, tm), BF16),
            jax.ShapeDtypeStruct((m, width), BF16),
            jax.ShapeDtypeStruct((batch, width, seq), F32),
            jax.ShapeDtypeStruct((m // tm, width, tm), BF16),
            jax.ShapeDtypeStruct((m, width), F32),
            jax.ShapeDtypeStruct((m, width), F32)]
    return pl.pallas_call(
        functools.partial(_in_proj_prompt_kernel, q_scale=q_scale),
        out_shape=outs,
        grid=(m // tm, 4),
        in_specs=[pl.BlockSpec((tm, d), row),
                  pl.BlockSpec((1, d), lambda i, n: (0, 0)),
                  pl.BlockSpec((d, width), lambda i, n: (0, n)),
                  pl.BlockSpec((tm, LANES), tab),
                  pl.BlockSpec((tm, LANES), tab)],
        out_specs=[pl.BlockSpec((None, width, tm), tile),
                   pl.BlockSpec((tm, width), row),
                   pl.BlockSpec((None, width, tm), lambda i, n: (i // nt, 0, i % nt)),
                   pl.BlockSpec((None, width, tm), tile),
                   pl.BlockSpec((tm, width), row),
                   pl.BlockSpec((tm, width), row)],
        scratch_shapes=[pltpu.VMEM((tm, d), BF16)],
        compiler_params=_cparams(("arbitrary", "arbitrary")),
        name="in_proj_prompt",
    )(x, g_mix, w_in_bf, cos_t, sin_t)


def _in_proj_sample_kernel(x_ref, g_ref, w_ref, cos_ref, sin_ref, q_ref, k_ref, v_ref, u_ref, xn_ref, *, q_scale):
    n = pl.program_id(1)

    @pl.when(n == 0)
    def _():
        xn_ref[...] = _rms(x_ref[...], g_ref[...]).astype(BF16)

    acc = jnp.dot(xn_ref[...], w_ref[...], preferred_element_type=F32)

    @pl.when(n == 0)
    def _():
        q_ref[...] = (_rope(acc, cos_ref, sin_ref) * q_scale).astype(BF16)

    @pl.when(n == 1)
    def _():
        k_ref[...] = _rope(acc, cos_ref, sin_ref)

    @pl.when(n == 2)
    def _():
        v_ref[...] = acc

    @pl.when(n == 3)
    def _():
        u_ref[...] = acc


def _in_proj_sample(x, g_mix, w_in_bf, cos_t, sin_t, q_scale):
    m, d = x.shape
    width = ATTN_WIDTH
    whole = lambda i, n: (0, 0)
    outs = [jax.ShapeDtypeStruct((m, width), BF16)] + [jax.ShapeDtypeStruct((m, width), F32)] * 3
    return pl.pallas_call(
        functools.partial(_in_proj_sample_kernel, q_scale=q_scale),
        out_shape=outs,
        grid=(1, 4),
        in_specs=[pl.BlockSpec((m, d), whole),
                  pl.BlockSpec((1, d), whole),
                  pl.BlockSpec((d, width), lambda i, n: (0, n)),
                  pl.BlockSpec((m, LANES), whole),
                  pl.BlockSpec((m, LANES), whole)],
        out_specs=[pl.BlockSpec((m, width), whole)] * 4,
        scratch_shapes=[pltpu.VMEM((m, d), BF16)],
        compiler_params=_cparams(("arbitrary", "arbitrary")),
        name="in_proj_sample",
    )(x, g_mix, w_in_bf, cos_t, sin_t)


def _lambda_value(lq1, lk1, lq2, lk2, lam_init):
    a = jnp.sum(lq1 * lk1, axis=-1, keepdims=True)
    b = jnp.sum(lq2 * lk2, axis=-1, keepdims=True)
    return jnp.exp(a) - jnp.exp(b) + lam_init


def _sub_norm(o, g_sub, lam_init):
    return _rms(o, g_sub) * (1.0 - lam_init)


def _prompt_attn_kernel(lq1_ref, lk1_ref, lq2_ref, lk2_ref, gs_ref, qt_ref, k_ref, vt_ref, o_ref,
                        m0_ref, l0_ref, a0_ref, m1_ref, l1_ref, a1_ref, *, tq, lam_init):
    qi = pl.program_id(2)
    hq = tq // 2
    qt = qt_ref[...]
    drow = lax.broadcasted_iota(jnp.int32, qt.shape, 0)
    zero = jnp.zeros_like(qt)
    qts = (jnp.where(drow < HEAD_DIM, qt, zero), jnp.where(drow >= HEAD_DIM, qt, zero))
    stats = ((m0_ref, l0_ref, a0_ref), (m1_ref, l1_ref, a1_ref))

    for m_ref, l_ref, a_ref in stats:
        m_ref[...] = jnp.full(m_ref.shape, -jnp.inf, F32)
        l_ref[...] = jnp.zeros(l_ref.shape, F32)
        a_ref[...] = jnp.zeros(a_ref.shape, F32)

    def run_chains(chains):
        scores = [jnp.dot(kj, qc[:, lanes], preferred_element_type=F32)
                  for kj, _, qc, _, lanes, _ in chains]
        probs = []
        for st, (_, _, _, (m_ref, l_ref, _), lanes, keep) in zip(scores, chains):
            if keep is not None:
                st = jnp.where(keep, st, -jnp.inf)
            m_old = m_ref[:, lanes]
            m_new = jnp.maximum(m_old, jnp.max(st, axis=0, keepdims=True))
            pt = jnp.exp2(st - m_new)
            alpha = jnp.exp2(m_old - m_new)
            l_ref[:, lanes] = alpha * l_ref[:, lanes] + jnp.sum(pt, axis=0, keepdims=True)
            m_ref[:, lanes] = m_new
            probs.append((pt.astype(BF16), alpha))
        for (pt, alpha), (_, vtj, _, (_, _, a_ref), lanes, _) in zip(probs, chains):
            a_ref[:, lanes] = alpha * a_ref[:, lanes] + jnp.dot(vtj, pt, preferred_element_type=F32)

    def body(j, carry):
        kj = k_ref[pl.ds(pl.multiple_of(j * tq, tq), tq), :]
        vtj = vt_ref[j]
        run_chains([(kj, vtj, qc, refs, slice(half * hq, (half + 1) * hq), None)
                    for qc, refs in zip(qts, stats) for half in range(2)])
        return carry

    lax.fori_loop(0, qi, body, 0)

    start = pl.multiple_of(qi * tq, tq)
    k_lo = k_ref[pl.ds(start, hq), :]
    k_all = k_ref[pl.ds(start, tq), :]
    vt_all = vt_ref[qi]
    tri = (lax.broadcasted_iota(jnp.int32, (hq, hq), 0) <= lax.broadcasted_iota(jnp.int32, (hq, hq), 1))
    late = (lax.broadcasted_iota(jnp.int32, (tq, hq), 0) <= lax.broadcasted_iota(jnp.int32, (tq, hq), 1) + hq)
    diagonal = []
    for qc, refs in zip(qts, stats):
        diagonal.append((k_lo, vt_all[:, :hq], qc, refs, slice(0, hq), tri))
        diagonal.append((k_all, vt_all, qc, refs, slice(hq, tq), late))
    run_chains(diagonal)

    lam = _lambda_value(lq1_ref[...], lk1_ref[...], lq2_ref[...], lk2_ref[...], lam_init)
    ot = a0_ref[...] * (1.0 / l0_ref[...]) - a1_ref[...] * (lam / l1_ref[...])
    ms = jnp.mean(ot * ot, axis=0, keepdims=True)
    ot = ot * lax.rsqrt(ms + EPS) * gs_ref[...] * (1.0 - lam_init)
    o_ref[...] = ot.T.astype(o_ref.dtype)


def _prompt_attention(qt, k, vt, lams, g_sub_col, batch, seq, tq, lam_init):
    nq = seq // tq
    small = lambda shape: pl.BlockSpec(shape, lambda b, h, i: (0, 0))
    kernel = functools.partial(_prompt_attn_kernel, tq=tq, lam_init=lam_init)
    return pl.pallas_call(
        kernel,
        out_shape=jax.ShapeDtypeStruct(k.shape, BF16),
        grid=(batch, N_HEADS, nq),
        in_specs=[small((1, HEAD_DIM))] * 4 + [small((V_DIM, 1))] + [
            pl.BlockSpec((None, V_DIM, tq), lambda b, h, i: (b * nq + i, h, 0)),
            pl.BlockSpec((seq, V_DIM), lambda b, h, i: (b, h)),
            pl.BlockSpec((nq, V_DIM, tq), lambda b, h, i: (b, h, 0))],
        out_specs=pl.BlockSpec((tq, V_DIM), lambda b, h, i: (b * nq + i, h)),
        scratch_shapes=[pltpu.VMEM((1, tq), F32), pltpu.VMEM((1, tq), F32), pltpu.VMEM((V_DIM, tq), F32)] * 2,
        compiler_params=_cparams(("arbitrary", "arbitrary", "arbitrary")),
        name="prompt_attention",
    )(*lams, g_sub_col, qt, k, vt)


def _sample_weights_kernel(pt_ref, lq1_ref, lk1_ref, lq2_ref, lk2_ref, qrows_ref, kn_ref, *rest, n_seq, lam_init):
    n_pages = len(rest) - 2
    k_refs = rest[:n_pages]
    wsel_ref, wnew_ref = rest[n_pages:]
    g = pl.program_id(0)
    qrows = qrows_ref[0]
    _, _, page = k_refs[0].shape
    s = jnp.concatenate([jnp.dot(qrows, k_ref[...].reshape(N_QK_HEADS * HEAD_DIM, page),
                                 preferred_element_type=F32) for k_ref in k_refs], axis=1)
    s_new = jnp.sum(qrows * kn_ref[0], axis=-1, keepdims=True)
    m = jnp.maximum(jnp.max(s, axis=-1, keepdims=True), s_new)
    p = jnp.exp(s - m)
    p_new = jnp.exp(s_new - m)
    l = jnp.sum(p, axis=-1, keepdims=True) + p_new
    lam = _lambda_value(lq1_ref[...], lk1_ref[...], lq2_ref[...], lk2_ref[...], lam_init)
    a = p / l
    a_new = p_new / l
    w = a[:N_HEADS] - lam * a[N_HEADS:]
    w_new = a_new[:N_HEADS] - lam * a_new[N_HEADS:]
    w_rows = jnp.concatenate([w[:, i * page:(i + 1) * page] for i in range(n_pages)], axis=0)
    tok = lax.broadcasted_iota(jnp.int32, (page, page * N_HEADS), 0)
    dst = lax.broadcasted_iota(jnp.int32, (page, page * N_HEADS), 1)
    spread = (dst // N_HEADS == tok).astype(F32)
    wide = jnp.dot(w_rows, spread, preferred_element_type=F32)
    row = lax.broadcasted_iota(jnp.int32, wide.shape, 0)
    col = lax.broadcasted_iota(jnp.int32, wide.shape, 1)
    own = row % N_HEADS == col % N_HEADS
    wsel_ref[...] = jnp.where(jnp.logical_and(own, g < n_seq), wide, 0.0)
    wnew_ref[0] = jnp.broadcast_to(w_new, wnew_ref.shape[1:])


def _sample_weights(page_table, lams, qrows, k_new, cache_kt, li, n_steps, lam_init):
    db, n_pages = page_table.shape
    page = cache_kt.shape[4]
    seq_of = lambda g: jnp.minimum(g, db - 1)
    small = lambda shape: pl.BlockSpec(shape, lambda g, pt: (0, 0))
    per_b = lambda shape: pl.BlockSpec(shape, lambda g, pt: (seq_of(g), 0, 0))

    def k_spec(i):
        return pl.BlockSpec((None, None, N_QK_HEADS, HEAD_DIM, page),
                            lambda g, pt: (li, pt[seq_of(g), i], 0, 0, 0))

    grid_spec = pltpu.PrefetchScalarGridSpec(
        num_scalar_prefetch=1,
        grid=(n_steps,),
        in_specs=[small((1, HEAD_DIM))] * 4 + [per_b((1, N_QK_HEADS, ATTN_WIDTH)), per_b((1, 1, ATTN_WIDTH))]
                 + [k_spec(i) for i in range(n_pages)],
        out_specs=[pl.BlockSpec((n_pages * N_HEADS, page * N_HEADS), lambda g, pt: (g, 0)),
                   pl.BlockSpec((1, N_HEADS, V_DIM), lambda g, pt: (g, 0, 0))],
    )
    return pl.pallas_call(
        functools.partial(_sample_weights_kernel, n_seq=db, lam_init=lam_init),
        out_shape=[jax.ShapeDtypeStruct((n_steps * n_pages * N_HEADS, page * N_HEADS), F32),
                   jax.ShapeDtypeStruct((n_steps, N_HEADS, V_DIM), F32)],
        grid_spec=grid_spec,
        compiler_params=_cparams(("arbitrary",)),
        name="sample_weights",
    )(page_table, *lams, qrows, k_new, *([cache_kt] * n_pages))


def _paged_values_step(step, page_ids_ref, wsel_ref, v_refs, oacc_ref, *, pages_per_seq, n_valid_pages):
    per_step = len(v_refs)
    page = v_refs[0].shape[0]

    @pl.when(step == 0)
    def _():
        oacc_ref[...] = jnp.zeros(oacc_ref.shape, F32)

    for i, v_ref in enumerate(v_refs):
        flat = jnp.minimum(step * per_step + i, n_valid_pages - 1)
        seq = flat // pages_per_seq
        wsel = wsel_ref[pl.ds(i * N_HEADS, N_HEADS), :]
        oacc_ref[seq] += jnp.dot(wsel, v_ref[...].reshape(page * N_HEADS, V_DIM), preferred_element_type=F32)


def _sample_finish_kernel(gs_ref, oacc_ref, wnew_ref, vn_ref, o_ref, *, lam_init):
    full = oacc_ref[...] + wnew_ref[...] * vn_ref[...]
    o_ref[...] = _sub_norm(full, gs_ref[...], lam_init).astype(o_ref.dtype)


def _sample_finish(g_sub, oacc, w_new, v_new, lam_init):
    db = oacc.shape[0]
    whole = pl.BlockSpec((db, N_HEADS, V_DIM), lambda i: (0, 0, 0))
    return pl.pallas_call(
        functools.partial(_sample_finish_kernel, lam_init=lam_init),
        out_shape=jax.ShapeDtypeStruct((db, N_HEADS, V_DIM), BF16),
        grid=(1,),
        in_specs=[pl.BlockSpec((1, V_DIM), lambda i: (0, 0)), whole, whole, whole],
        out_specs=whole,
        compiler_params=_cparams(("arbitrary",)),
        name="sample_finish",
    )(g_sub, oacc, w_new, v_new)


def _pool_matmul(pooled_groups, wp_ref, scale):
    outs = [jnp.dot(pg.astype(BF16), wp_ref[g], preferred_element_type=F32)
            for g, pg in enumerate(pooled_groups)]
    return jnp.concatenate(outs, axis=1) * scale


def _pool_prompt_kernel(u_ref, halo_ref, wp_ref, sc_ref, o_ref, ext_ref, *, ts):
    i = pl.program_id(1)
    group = wp_ref.shape[1]
    u = u_ref[...]
    ext_ref[pl.ds(POOL_HALO, ts), :] = u
    ext_ref[pl.ds(0, POOL_HALO), :] = jnp.where(i > 0, halo_ref[...], 0.0)
    pos1 = (i * ts + lax.broadcasted_iota(jnp.int32, (ts, 1), 0) + 1).astype(F32)
    pooled = []
    for g, win in enumerate(POOL_WINDOWS):
        cols = slice(g * group, (g + 1) * group)
        wsum = u[:, cols]
        for k in range(1, win):
            wsum = wsum + ext_ref[pl.ds(POOL_HALO - k, ts), cols]
        cnt = jnp.minimum(pos1, float(win))
        pooled.append(wsum / cnt - u[:, cols])
    o_ref[...] = _pool_matmul(pooled, wp_ref, sc_ref[...]).astype(o_ref.dtype)


def _pool_prompt(u, w_pool_bf, pool_scale, batch, seq, ts):
    m, width = u.shape
    nt = seq // ts
    r = ts // POOL_HALO
    return pl.pallas_call(
        functools.partial(_pool_prompt_kernel, ts=ts),
        out_shape=jax.ShapeDtypeStruct((m, width), BF16),
        grid=(batch, nt),
        in_specs=[pl.BlockSpec((ts, width), lambda b, i: (b * nt + i, 0)),
                  pl.BlockSpec((POOL_HALO, width), lambda b, i: (jnp.maximum((b * nt + i) * r - 1, 0), 0)),
                  pl.BlockSpec(w_pool_bf.shape, lambda b, i: (0, 0, 0)),
                  pl.BlockSpec((1, width), lambda b, i: (0, 0))],
        out_specs=pl.BlockSpec((ts, width), lambda b, i: (b * nt + i, 0)),
        scratch_shapes=[pltpu.VMEM((ts + POOL_HALO, width), F32)],
        compiler_params=_cparams(("arbitrary", "arbitrary")),
        name="pool_prompt",
    )(u, u, w_pool_bf, pool_scale)


def _pool_sample_kernel(u_ref, st_ref, wp_ref, sc_ref, o_ref):
    group = wp_ref.shape[1]
    u = u_ref[...]
    pooled = []
    for g, win in enumerate(POOL_WINDOWS):
        cols = slice(g * group, (g + 1) * group)
        wsum = u[:, cols]
        for k in range(1, win):
            wsum = wsum + st_ref[POOL_STATE - k, :, cols]
        pooled.append(wsum / float(win) - u[:, cols])
    o_ref[...] = _pool_matmul(pooled, wp_ref, sc_ref[...]).astype(o_ref.dtype)


def _pool_sample(u, state_t, w_pool_bf, pool_scale, tb):
    m, width = u.shape
    return pl.pallas_call(
        _pool_sample_kernel,
        out_shape=jax.ShapeDtypeStruct((m, width), BF16),
        grid=(m // tb,),
        in_specs=[pl.BlockSpec((tb, width), lambda i: (i, 0)),
                  pl.BlockSpec((POOL_STATE, tb, width), lambda i: (0, i, 0)),
                  pl.BlockSpec(w_pool_bf.shape, lambda i: (0, 0, 0)),
                  pl.BlockSpec((1, width), lambda i: (0, 0))],
        out_specs=pl.BlockSpec((tb, width), lambda i: (i, 0)),
        compiler_params=_cparams(("arbitrary",)),
        name="pool_sample",
    )(u, state_t, w_pool_bf, pool_scale)


def _out_proj_kernel(o_ref, p_ref, w_ref, h_ref, out_ref):
    half = o_ref.shape[1]
    acc = jnp.dot(o_ref[...], w_ref[pl.ds(0, half), :], preferred_element_type=F32)
    acc += jnp.dot(p_ref[...], w_ref[pl.ds(half, half), :], preferred_element_type=F32)
    out_ref[...] = h_ref[...] + acc


def _out_proj(o, pool, w_out_bf, h, tm):
    m, d = h.shape
    row = lambda i: (i, 0)
    return pl.pallas_call(
        _out_proj_kernel,
        out_shape=jax.ShapeDtypeStruct((m, d), F32),
        grid=(m // tm,),
        in_specs=[pl.BlockSpec((tm, o.shape[1]), row),
                  pl.BlockSpec((tm, pool.shape[1]), row),
                  pl.BlockSpec(w_out_bf.shape, lambda i: (0, 0)),
                  pl.BlockSpec((tm, d), row)],
        out_specs=pl.BlockSpec((tm, d), row),
        compiler_params=_cparams(("arbitrary",)),
        name="out_proj",
    )(o, pool, w_out_bf, h)


def _ffn_body(h_ref, g_ref, wg_ref, wu_ref, wd_ref, out_ref, xn_ref, acc_ref):
    f = pl.program_id(1)

    @pl.when(f == 0)
    def _():
        xn_ref[...] = _rms(h_ref[...], g_ref[...]).astype(BF16)
        acc_ref[...] = jnp.zeros(acc_ref.shape, F32)

    xn = xn_ref[...]
    gate = jnp.dot(xn, wg_ref[...], preferred_element_type=F32)
    up = jnp.dot(xn, wu_ref[...], preferred_element_type=F32)
    act = (gate * jax.nn.sigmoid(gate) * up).astype(BF16)
    acc_ref[...] += jnp.dot(act, wd_ref[...], preferred_element_type=F32)

    @pl.when(f == pl.num_programs(1) - 1)
    def _():
        out_ref[...] = h_ref[...] + acc_ref[...]


def _ffn_kernel(h_ref, g_ref, wg_ref, wu_ref, wd_ref, out_ref, xn_ref, acc_ref):
    _ffn_body(h_ref, g_ref, wg_ref, wu_ref, wd_ref, out_ref, xn_ref, acc_ref)


def _ffn_paged_kernel(page_ids_ref, h_ref, g_ref, wg_ref, wu_ref, wd_ref, wsel_ref, *rest,
                      per_step, pages_per_seq, n_valid_pages):
    v_refs = rest[:per_step]
    out_ref, oacc_ref, xn_ref, acc_ref = rest[per_step:]
    step = pl.program_id(0) * pl.num_programs(1) + pl.program_id(1)
    _paged_values_step(step, page_ids_ref, wsel_ref, v_refs, oacc_ref,
                       pages_per_seq=pages_per_seq, n_valid_pages=n_valid_pages)
    _ffn_body(h_ref, g_ref, wg_ref, wu_ref, wd_ref, out_ref, xn_ref, acc_ref)


def _ffn_specs(d, tm, tf, nf, index):
    return [pl.BlockSpec((tm, d), index(lambda i, f: (i, 0))),
            pl.BlockSpec((1, d), index(lambda i, f: (0, 0))),
            pl.BlockSpec((d, tf), index(lambda i, f: (0, f))),
            pl.BlockSpec((d, tf), index(lambda i, f: (0, nf + f))),
            pl.BlockSpec((tf, d), index(lambda i, f: (f, 0)))]


def _ffn(h, g_ffn, w_gate_up_bf, w_down_bf, tm, tf):
    m, d = h.shape
    nf = w_down_bf.shape[0] // tf
    plain = lambda fn: fn
    return pl.pallas_call(
        _ffn_kernel,
        out_shape=jax.ShapeDtypeStruct((m, d), F32),
        grid=(m // tm, nf),
        in_specs=_ffn_specs(d, tm, tf, nf, plain),
        out_specs=pl.BlockSpec((tm, d), lambda i, f: (i, 0)),
        scratch_shapes=[pltpu.VMEM((tm, d), BF16), pltpu.VMEM((tm, d), F32)],
        compiler_params=_cparams(("arbitrary", "arbitrary")),
        name="ffn",
    )(h, g_ffn, w_gate_up_bf, w_gate_up_bf, w_down_bf)


def _paged_plan(n_seq, pages_per_seq, n_steps):
    per_step = -(-n_seq * pages_per_seq // n_steps)
    weight_steps = -(-n_steps * per_step // pages_per_seq)
    return per_step, weight_steps


def _ffn_paged(h, g_ffn, w_gate_up_bf, w_down_bf, tm, tf, page_table, wsel, cache_v, li):
    m, d = h.shape
    nf = w_down_bf.shape[0] // tf
    n_steps = (m // tm) * nf
    db, pages_per_seq = page_table.shape
    page = cache_v.shape[2]
    n_valid = db * pages_per_seq
    per_step, _ = _paged_plan(db, pages_per_seq, n_steps)
    flat = page_table.reshape(-1)
    page_ids = jnp.concatenate([flat, jnp.broadcast_to(flat[-1:], (n_steps * per_step - n_valid,))])
    with_ids = lambda fn: (lambda i, f, ids: fn(i, f))

    def v_spec(s):
        return pl.BlockSpec((None, None, page, N_HEADS, V_DIM),
                            lambda i, f, ids: (li, ids[(i * nf + f) * per_step + s], 0, 0, 0))

    grid_spec = pltpu.PrefetchScalarGridSpec(
        num_scalar_prefetch=1,
        grid=(m // tm, nf),
        in_specs=_ffn_specs(d, tm, tf, nf, with_ids)
                 + [pl.BlockSpec((per_step * N_HEADS, page * N_HEADS), lambda i, f, ids: (i * nf + f, 0))]
                 + [v_spec(s) for s in range(per_step)],
        out_specs=[pl.BlockSpec((tm, d), lambda i, f, ids: (i, 0)),
                   pl.BlockSpec((db, N_HEADS, V_DIM), lambda i, f, ids: (0, 0, 0))],
        scratch_shapes=[pltpu.VMEM((tm, d), BF16), pltpu.VMEM((tm, d), F32)],
    )
    return pl.pallas_call(
        functools.partial(_ffn_paged_kernel, per_step=per_step, pages_per_seq=pages_per_seq,
                          n_valid_pages=n_valid),
        out_shape=[jax.ShapeDtypeStruct((m, d), F32), jax.ShapeDtypeStruct((db, N_HEADS, V_DIM), F32)],
        grid_spec=grid_spec,
        compiler_params=_cparams(("arbitrary", "arbitrary")),
        name="ffn_paged",
    )(page_ids, h, g_ffn, w_gate_up_bf, w_gate_up_bf, w_down_bf, wsel, *([cache_v] * per_step))


def _ple_kernel(h_ref, p_ref, gp_ref, wg_ref, wp_ref, gf_ref, y_ref, *, final_norm):
    h = h_ref[...]
    xn = _rms(h, gp_ref[...]).astype(BF16)
    gate = jax.nn.sigmoid(jnp.dot(xn, wg_ref[...], preferred_element_type=F32))
    proj = jnp.dot(p_ref[...].astype(BF16), wp_ref[...], preferred_element_type=F32)
    y = h + gate * proj
    y_ref[...] = _rms(y, gf_ref[...]) if final_norm else y


def _ple(h, p, g_ple, w_gate_bf, w_proj_bf, g_final, tm, final_norm):
    m, d = h.shape
    row = lambda i: (i, 0)
    const = lambda i: (0, 0)
    return pl.pallas_call(
        functools.partial(_ple_kernel, final_norm=final_norm),
        out_shape=jax.ShapeDtypeStruct((m, d), F32),
        grid=(m // tm,),
        in_specs=[pl.BlockSpec((tm, d), row),
                  pl.BlockSpec((tm, p.shape[1]), row),
                  pl.BlockSpec((1, d), const),
                  pl.BlockSpec(w_gate_bf.shape, const),
                  pl.BlockSpec(w_proj_bf.shape, const),
                  pl.BlockSpec((1, d), const)],
        out_specs=pl.BlockSpec((tm, d), row),
        compiler_params=_cparams(("arbitrary",)),
        name="ple_final",
    )(h, p, g_ple, w_gate_bf, w_proj_bf, g_final)


def _rope_tables(positions):
    half = HEAD_DIM // 2
    inv = ROPE_THETA ** (-jnp.arange(half, dtype=F32) / half)
    ang = positions.astype(F32)[:, None] * inv[None, :]
    cos, sin = jnp.cos(ang), jnp.sin(ang)
    reps = LANES // HEAD_DIM
    return jnp.tile(jnp.concatenate([cos, cos], axis=1), (1, reps)), \
        jnp.tile(jnp.concatenate([-sin, sin], axis=1), (1, reps))


def _row_tile(m, want):
    return want if m % want == 0 else m


def kernel(x_prompt, x_sample, p_prompt, p_sample, cache_k, cache_v, state_pool, page_table, g_mix, w_in, w_out, lambda_q1, lambda_k1, lambda_q2, lambda_k2, g_sub, w_pool, pool_scale, g_ffn, w_gate_up, w_down, g_ple, w_ple_gate, w_ple_proj, g_final):
    batch, seq, d = x_prompt.shape
    db, ds, _ = x_sample.shape
    depth = g_mix.shape[0]
    n_pages = page_table.shape[1]
    page = cache_k.shape[2]
    past = n_pages * page
    assert ds == 1 and d == 2 * ATTN_WIDTH

    hp = x_prompt.reshape(batch * seq, d)
    hs = x_sample.reshape(db * ds, d)
    cos_p, sin_p = _rope_tables(jnp.arange(seq))
    cos_s, sin_s = _rope_tables(jnp.full((db,), past))
    row2 = lambda a: a.reshape(1, -1)

    tm_p = _row_tile(batch * seq, 512)
    tq = _row_tile(seq, 512)
    ts = _row_tile(seq, 512)
    cache_kt = jnp.transpose(cache_k, (0, 1, 3, 4, 2))
    tf = 512
    qk_scale = HEAD_DIM ** -0.5

    outs = {k: [] for k in ("kp", "vp", "pp", "ks", "vs", "ps")}
    for li in range(depth):
        lam_init = 0.8 - 0.6 * math.exp(-0.3 * li)
        w_in_bf = w_in[li].astype(BF16)
        w_out_bf = w_out[li].astype(BF16)
        w_pool_bf = w_pool[li].astype(BF16)
        w_gu_bf = w_gate_up[li].astype(BF16)
        w_down_bf = w_down[li].astype(BF16)
        w_pg_bf = w_ple_gate[li].astype(BF16)
        w_pp_bf = w_ple_proj[li].astype(BF16)
        lams = tuple(row2(a[li]) for a in (lambda_q1, lambda_k1, lambda_q2, lambda_k2))
        gsub = row2(g_sub[li])

        qs, kfs, vfs, us = _in_proj_sample(hs, row2(g_mix[li]), w_in_bf, cos_s, sin_s, qk_scale)
        col_sub = jnp.arange(ATTN_WIDTH) // HEAD_DIM
        row_sub = (jnp.arange(N_QK_HEADS) % N_HEADS) * 2 + jnp.arange(N_QK_HEADS) // N_HEADS
        qrows = qs.astype(F32)[:, None, :] * (col_sub[None, :] == row_sub[:, None]).astype(F32)[None]
        ffn_steps = (batch * seq // tm_p) * (w_down_bf.shape[0] // tf)
        _, weight_steps = _paged_plan(db, n_pages, ffn_steps)
        wsel, w_new = _sample_weights(page_table, lams, qrows, kfs.reshape(db, 1, ATTN_WIDTH), cache_kt, li,
                                      weight_steps, lam_init)

        qt, kb, kt, vt, vf, u = _in_proj_prompt(hp, row2(g_mix[li]), w_in_bf, cos_p, sin_p, batch, seq, tq,
                                                qk_scale * math.log2(math.e))
        o = _prompt_attention(qt, kb, vt, lams, g_sub[li].reshape(-1, 1), batch, seq, tq, lam_init)
        pool = _pool_prompt(u, w_pool_bf, row2(pool_scale[li]), batch, seq, ts)
        h1 = _out_proj(o, pool, w_out_bf, hp, tm_p)
        h2, oacc = _ffn_paged(h1, row2(g_ffn[li]), w_gu_bf, w_down_bf, tm_p, tf, page_table, wsel, cache_v, li)
        outs["kp"].append(jnp.transpose(kt.reshape(batch, N_QK_HEADS, HEAD_DIM, seq), (0, 3, 1, 2)))
        outs["vp"].append(vf.reshape(batch, seq, N_HEADS, V_DIM))
        outs["pp"].append(u.reshape(batch, seq, -1)[:, seq - POOL_STATE:])
        p_rows = p_prompt[li].reshape(batch * seq, -1)
        last = li == depth - 1
        gfin = row2(g_final)
        hp_next = _ple(h2, p_rows, row2(g_ple[li]), w_pg_bf, w_pp_bf, gfin, tm_p, last)

        o_s = _sample_finish(gsub, oacc, w_new, vfs.reshape(db, N_HEADS, V_DIM), lam_init).reshape(db, -1)
        state = state_pool[li]
        pool_s = _pool_sample(us, jnp.swapaxes(state, 0, 1), w_pool_bf, row2(pool_scale[li]), db)
        h1s = _out_proj(o_s, pool_s, w_out_bf, hs, db)
        h2s = _ffn(h1s, row2(g_ffn[li]), w_gu_bf, w_down_bf, db, tf)
        hs_next = _ple(h2s, p_sample[li].reshape(db * ds, -1), row2(g_ple[li]), w_pg_bf, w_pp_bf, gfin, db, last)
        outs["ks"].append(kfs.reshape(db, ds, N_QK_HEADS, HEAD_DIM))
        outs["vs"].append(vfs.reshape(db, ds, N_HEADS, V_DIM))
        outs["ps"].append(jnp.concatenate([state[:, 1:], us[:, None, :]], axis=1))
        hp, hs = hp_next, hs_next

    y_prompt = hp.reshape(batch, seq, d)
    y_sample = hs.reshape(db, ds, d)
    return (y_prompt, y_sample, jnp.stack(outs["kp"]), jnp.stack(outs["vp"]), jnp.stack(outs["pp"]),
            jnp.stack(outs["ks"]), jnp.stack(outs["vs"]), jnp.stack(outs["ps"]))
```

```python
import functools
import math

import jax
import jax.numpy as jnp
from jax import lax
from jax.experimental import pallas as pl
from jax.experimental.pallas import tpu as pltpu

F32 = jnp.float32
BF16 = jnp.bfloat16

N_HEADS = 8
HEAD_DIM = 64
V_DIM = 2 * HEAD_DIM
N_QK_HEADS = 2 * N_HEADS
ATTN_WIDTH = N_HEADS * V_DIM
POOL_WINDOWS = (2, 4, 8, 16)
POOL_STATE = max(POOL_WINDOWS) - 1
POOL_HALO = 16
ROPE_THETA = 10000.0
EPS = 1e-6
LANES = 128
MXU_COLS = 256
VMEM_LIMIT = 56 * 1024 * 1024


def _cparams(sem):
    return pltpu.CompilerParams(dimension_semantics=sem, vmem_limit_bytes=VMEM_LIMIT)


def _rms(x, g):
    ms = jnp.mean(x * x, axis=-1, keepdims=True)
    return x * lax.rsqrt(ms + EPS) * g


def _rope_chunk(x, cos, sin_signed, first_half):
    swapped = jnp.where(first_half, pltpu.roll(x, 96, 1), pltpu.roll(x, 32, 1))
    return x * cos + swapped * sin_signed


def _col_chunks(xn_ref, w_ref):
    xn = xn_ref[...]
    for c in range(w_ref.shape[1] // MXU_COLS):
        cols = slice(c * MXU_COLS, (c + 1) * MXU_COLS)
        yield cols, jnp.dot(xn, w_ref[:, cols], preferred_element_type=F32)


def _rope(a, cos_ref, sin_ref):
    cos = cos_ref[...]
    sin = sin_ref[...]
    lane = lax.broadcasted_iota(jnp.int32, cos.shape, 1)
    first_half = (lane % HEAD_DIM) < (HEAD_DIM // 2)
    parts = [_rope_chunk(a[:, c * LANES:(c + 1) * LANES], cos, sin, first_half)
             for c in range(a.shape[1] // LANES)]
    return jnp.concatenate(parts, axis=1)


def _in_proj_prompt_kernel(x_ref, g_ref, w_ref, cos_ref, sin_ref,
                           qt_ref, kb_ref, kt_ref, vt_ref, vf_ref, u_ref, xn_ref, *, q_scale):
    n = pl.program_id(1)

    @pl.when(n == 0)
    def _():
        xn_ref[...] = _rms(x_ref[...], g_ref[...]).astype(BF16)
        for cols, acc in _col_chunks(xn_ref, w_ref):
            qt_ref[cols, :] = (_rope(acc, cos_ref, sin_ref) * q_scale).T.astype(BF16)

    @pl.when(n == 1)
    def _():
        for cols, acc in _col_chunks(xn_ref, w_ref):
            k = _rope(acc, cos_ref, sin_ref)
            kb_ref[:, cols] = k.astype(BF16)
            kt_ref[cols, :] = k.T

    @pl.when(n == 2)
    def _():
        for cols, acc in _col_chunks(xn_ref, w_ref):
            vf_ref[:, cols] = acc
            vt_ref[cols, :] = acc.T.astype(BF16)

    @pl.when(n == 3)
    def _():
        for cols, acc in _col_chunks(xn_ref, w_ref):
            u_ref[:, cols] = acc


def _in_proj_prompt(x, g_mix, w_in_bf, cos_t, sin_t, batch, seq, tm, q_scale):
    m, d = x.shape
    width = ATTN_WIDTH
    nt = seq // tm
    row = lambda i, n: (i, 0)
    tile = lambda i, n: (i, 0, 0)
    tab = lambda i, n: (i % nt, 0)
    outs = [jax.ShapeDtypeStruct((m // tm, width, tm), BF16),
            jax.ShapeDtypeStruct((m, width), BF16),
            jax.ShapeDtypeStruct((batch, width, seq), F32),
            jax.ShapeDtypeStruct((m // tm, width, tm), BF16),
            jax.ShapeDtypeStruct((m, width), F32),
            jax.ShapeDtypeStruct((m, width), F32)]
    return pl.pallas_call(
        functools.partial(_in_proj_prompt_kernel, q_scale=q_scale),
        out_shape=outs,
        grid=(m // tm, 4),
        in_specs=[pl.BlockSpec((tm, d), row),
                  pl.BlockSpec((1, d), lambda i, n: (0, 0)),
                  pl.BlockSpec((d, width), lambda i, n: (0, n)),
                  pl.BlockSpec((tm, LANES), tab),
                  pl.BlockSpec((tm, LANES), tab)],
        out_specs=[pl.BlockSpec((None, width, tm), tile),
                   pl.BlockSpec((tm, width), row),
                   pl.BlockSpec((None, width, tm), lambda i, n: (i // nt, 0, i % nt)),
                   pl.BlockSpec((None, width, tm), tile),
                   pl.BlockSpec((tm, width), row),
                   pl.BlockSpec((tm, width), row)],
        scratch_shapes=[pltpu.VMEM((tm, d), BF16)],
        compiler_params=_cparams(("arbitrary", "arbitrary")),
        name="in_proj_prompt",
    )(x, g_mix, w_in_bf, cos_t, sin_t)


def _in_proj_sample_kernel(x_ref, g_ref, w_ref, cos_ref, sin_ref, q_ref, k_ref, v_ref, u_ref, xn_ref, *, q_scale):
    n = pl.program_id(1)

    @pl.when(n == 0)
    def _():
        xn_ref[...] = _rms(x_ref[...], g_ref[...]).astype(BF16)

    acc = jnp.dot(xn_ref[...], w_ref[...], preferred_element_type=F32)

    @pl.when(n == 0)
    def _():
        q_ref[...] = (_rope(acc, cos_ref, sin_ref) * q_scale).astype(BF16)

    @pl.when(n == 1)
    def _():
        k_ref[...] = _rope(acc, cos_ref, sin_ref)

    @pl.when(n == 2)
    def _():
        v_ref[...] = acc

    @pl.when(n == 3)
    def _():
        u_ref[...] = acc


def _in_proj_sample(x, g_mix, w_in_bf, cos_t, sin_t, q_scale):
    m, d = x.shape
    width = ATTN_WIDTH
    whole = lambda i, n: (0, 0)
    outs = [jax.ShapeDtypeStruct((m, width), BF16)] + [jax.ShapeDtypeStruct((m, width), F32)] * 3
    return pl.pallas_call(
        functools.partial(_in_proj_sample_kernel, q_scale=q_scale),
        out_shape=outs,
        grid=(1, 4),
        in_specs=[pl.BlockSpec((m, d), whole),
                  pl.BlockSpec((1, d), whole),
                  pl.BlockSpec((d, width), lambda i, n: (0, n)),
                  pl.BlockSpec((m, LANES), whole),
                  pl.BlockSpec((m, LANES), whole)],
        out_specs=[pl.BlockSpec((m, width), whole)] * 4,
        scratch_shapes=[pltpu.VMEM((m, d), BF16)],
        compiler_params=_cparams(("arbitrary", "arbitrary")),
        name="in_proj_sample",
    )(x, g_mix, w_in_bf, cos_t, sin_t)


def _lambda_value(lq1, lk1, lq2, lk2, lam_init):
    a = jnp.sum(lq1 * lk1, axis=-1, keepdims=True)
    b = jnp.sum(lq2 * lk2, axis=-1, keepdims=True)
    return jnp.exp(a) - jnp.exp(b) + lam_init


def _sub_norm(o, g_sub, lam_init):
    return _rms(o, g_sub) * (1.0 - lam_init)


def _prompt_attn_kernel(pt_ref, lq1_ref, lk1_ref, lq2_ref, lk2_ref, gs_ref, qt_ref, k_ref, vt_ref,
                        qrows_ref, kn_ref, *rest, n_key_pages, tq, lam_init):
    key_pages = rest[:n_key_pages]
    o_ref, wsel_ref, wnew_ref, m0_ref, l0_ref, a0_ref, m1_ref, l1_ref, a1_ref = rest[n_key_pages:]
    lam = _lambda_value(lq1_ref[...], lk1_ref[...], lq2_ref[...], lk2_ref[...], lam_init)
    wsel, w_new = _sequence_weights(qrows_ref[0], kn_ref[0], key_pages, lam)
    wsel_ref[...] = wsel
    wnew_ref[0] = jnp.broadcast_to(w_new, wnew_ref.shape[1:])

    qi = pl.program_id(2)
    hq = tq // 2
    qt = qt_ref[...]
    drow = lax.broadcasted_iota(jnp.int32, qt.shape, 0)
    zero = jnp.zeros_like(qt)
    qts = (jnp.where(drow < HEAD_DIM, qt, zero), jnp.where(drow >= HEAD_DIM, qt, zero))
    stats = ((m0_ref, l0_ref, a0_ref), (m1_ref, l1_ref, a1_ref))

    for m_ref, l_ref, a_ref in stats:
        m_ref[...] = jnp.full(m_ref.shape, -jnp.inf, F32)
        l_ref[...] = jnp.zeros(l_ref.shape, F32)
        a_ref[...] = jnp.zeros(a_ref.shape, F32)

    def run_chains(chains):
        scores = [jnp.dot(kj, qc[:, lanes], preferred_element_type=F32)
                  for kj, _, qc, _, lanes, _ in chains]
        probs = []
        for st, (_, _, _, (m_ref, l_ref, _), lanes, keep) in zip(scores, chains):
            if keep is not None:
                st = jnp.where(keep, st, -jnp.inf)
            m_old = m_ref[:, lanes]
            m_new = jnp.maximum(m_old, jnp.max(st, axis=0, keepdims=True))
            pt = jnp.exp2(st - m_new)
            alpha = jnp.exp2(m_old - m_new)
            l_ref[:, lanes] = alpha * l_ref[:, lanes] + jnp.sum(pt, axis=0, keepdims=True)
            m_ref[:, lanes] = m_new
            probs.append((pt.astype(BF16), alpha))
        for (pt, alpha), (_, vtj, _, (_, _, a_ref), lanes, _) in zip(probs, chains):
            a_ref[:, lanes] = alpha * a_ref[:, lanes] + jnp.dot(vtj, pt, preferred_element_type=F32)

    def body(j, carry):
        kj = k_ref[pl.ds(pl.multiple_of(j * tq, tq), tq), :]
        vtj = vt_ref[j]
        run_chains([(kj, vtj, qc, refs, slice(half * hq, (half + 1) * hq), None)
                    for qc, refs in zip(qts, stats) for half in range(2)])
        return carry

    lax.fori_loop(0, qi, body, 0)

    start = pl.multiple_of(qi * tq, tq)
    k_lo = k_ref[pl.ds(start, hq), :]
    k_all = k_ref[pl.ds(start, tq), :]
    vt_all = vt_ref[qi]
    tri = (lax.broadcasted_iota(jnp.int32, (hq, hq), 0) <= lax.broadcasted_iota(jnp.int32, (hq, hq), 1))
    late = (lax.broadcasted_iota(jnp.int32, (tq, hq), 0) <= lax.broadcasted_iota(jnp.int32, (tq, hq), 1) + hq)
    diagonal = []
    for qc, refs in zip(qts, stats):
        diagonal.append((k_lo, vt_all[:, :hq], qc, refs, slice(0, hq), tri))
        diagonal.append((k_all, vt_all, qc, refs, slice(hq, tq), late))
    run_chains(diagonal)

    ot = a0_ref[...] * (1.0 / l0_ref[...]) - a1_ref[...] * (lam / l1_ref[...])
    ms = jnp.mean(ot * ot, axis=0, keepdims=True)
    ot = ot * lax.rsqrt(ms + EPS) * gs_ref[...] * (1.0 - lam_init)
    o_ref[...] = ot.T.astype(o_ref.dtype)


def _prompt_attention(qt, k, vt, lams, g_sub_col, batch, seq, tq, lam_init,
                      page_table, qrows, k_new, cache_kt, li, wsel_rows):
    nq = seq // tq
    db, n_pages = page_table.shape
    page = cache_kt.shape[4]
    assert batch * N_HEADS * nq >= db, "one sample sequence rides on each grid step"
    seq_of = lambda b, h, i: jnp.minimum((b * N_HEADS + h) * nq + i, db - 1)
    small = lambda shape: pl.BlockSpec(shape, lambda b, h, i, pt: (0, 0))
    per_seq = lambda shape: pl.BlockSpec(shape, lambda b, h, i, pt: (seq_of(b, h, i), 0, 0))

    def key_page(s):
        return pl.BlockSpec((None, None, N_QK_HEADS, HEAD_DIM, page),
                            lambda b, h, i, pt: (li, pt[seq_of(b, h, i), s], 0, 0, 0))

    grid_spec = pltpu.PrefetchScalarGridSpec(
        num_scalar_prefetch=1,
        grid=(batch, N_HEADS, nq),
        in_specs=[small((1, HEAD_DIM))] * 4 + [small((V_DIM, 1))] + [
            pl.BlockSpec((None, V_DIM, tq), lambda b, h, i, pt: (b * nq + i, h, 0)),
            pl.BlockSpec((seq, V_DIM), lambda b, h, i, pt: (b, h)),
            pl.BlockSpec((nq, V_DIM, tq), lambda b, h, i, pt: (b, h, 0)),
            per_seq((1, N_QK_HEADS, ATTN_WIDTH)),
            per_seq((1, 1, ATTN_WIDTH))] + [key_page(s) for s in range(n_pages)],
        out_specs=[pl.BlockSpec((tq, V_DIM), lambda b, h, i, pt: (b * nq + i, h)),
                   pl.BlockSpec((n_pages * N_HEADS, page * N_HEADS), lambda b, h, i, pt: (seq_of(b, h, i), 0)),
                   per_seq((1, N_HEADS, V_DIM))],
        scratch_shapes=[pltpu.VMEM((1, tq), F32), pltpu.VMEM((1, tq), F32), pltpu.VMEM((V_DIM, tq), F32)] * 2,
    )
    kernel = functools.partial(_prompt_attn_kernel, n_key_pages=n_pages, tq=tq, lam_init=lam_init)
    return pl.pallas_call(
        kernel,
        out_shape=[jax.ShapeDtypeStruct(k.shape, BF16),
                   jax.ShapeDtypeStruct((wsel_rows, page * N_HEADS), F32),
                   jax.ShapeDtypeStruct((db, N_HEADS, V_DIM), F32)],
        grid_spec=grid_spec,
        compiler_params=_cparams(("arbitrary", "arbitrary", "arbitrary")),
        name="prompt_attention",
    )(page_table, *lams, g_sub_col, qt, k, vt, qrows, k_new, *([cache_kt] * n_pages))


def _sequence_weights(qrows, k_new, k_refs, lam):
    n_pages = len(k_refs)
    _, _, page = k_refs[0].shape
    s = jnp.concatenate([jnp.dot(qrows, k_ref[...].reshape(N_QK_HEADS * HEAD_DIM, page),
                                 preferred_element_type=F32) for k_ref in k_refs], axis=1)
    s_new = jnp.sum(qrows * k_new, axis=-1, keepdims=True)
    m = jnp.maximum(jnp.max(s, axis=-1, keepdims=True), s_new)
    p = jnp.exp(s - m)
    p_new = jnp.exp(s_new - m)
    l = jnp.sum(p, axis=-1, keepdims=True) + p_new
    a = p / l
    a_new = p_new / l
    w = a[:N_HEADS] - lam * a[N_HEADS:]
    w_new = a_new[:N_HEADS] - lam * a_new[N_HEADS:]
    w_rows = jnp.concatenate([w[:, i * page:(i + 1) * page] for i in range(n_pages)], axis=0)
    tok = lax.broadcasted_iota(jnp.int32, (page, page * N_HEADS), 0)
    dst = lax.broadcasted_iota(jnp.int32, (page, page * N_HEADS), 1)
    spread = (dst // N_HEADS == tok).astype(F32)
    wide = jnp.dot(w_rows, spread, preferred_element_type=F32)
    row = lax.broadcasted_iota(jnp.int32, wide.shape, 0)
    col = lax.broadcasted_iota(jnp.int32, wide.shape, 1)
    return jnp.where(row % N_HEADS == col % N_HEADS, wide, 0.0), w_new


def _zero_tail_kernel(x_ref, o_ref):
    o_ref[...] = jnp.zeros(o_ref.shape, o_ref.dtype)


def _zero_tail(x, block_rows, first_block):
    n_blocks = x.shape[0] // block_rows - first_block
    return pl.pallas_call(
        _zero_tail_kernel,
        out_shape=jax.ShapeDtypeStruct(x.shape, x.dtype),
        grid=(n_blocks,),
        in_specs=[pl.BlockSpec(memory_space=pl.ANY)],
        out_specs=pl.BlockSpec((block_rows, x.shape[1]), lambda i: (first_block + i, 0)),
        input_output_aliases={0: 0},
        compiler_params=_cparams(("arbitrary",)),
        name="zero_tail",
    )(x)


def _paged_values_step(step, page_ids_ref, wsel_ref, v_refs, oacc_ref, *, pages_per_seq, n_valid_pages):
    per_step = len(v_refs)
    page = v_refs[0].shape[0]

    @pl.when(step == 0)
    def _():
        oacc_ref[...] = jnp.zeros(oacc_ref.shape, F32)

    for i, v_ref in enumerate(v_refs):
        flat = jnp.minimum(step * per_step + i, n_valid_pages - 1)
        seq = flat // pages_per_seq
        wsel = wsel_ref[pl.ds(i * N_HEADS, N_HEADS), :]
        oacc_ref[seq] += jnp.dot(wsel, v_ref[...].reshape(page * N_HEADS, V_DIM), preferred_element_type=F32)


def _sample_finish_kernel(gs_ref, oacc_ref, wnew_ref, vn_ref, o_ref, *, lam_init):
    full = oacc_ref[...] + wnew_ref[...] * vn_ref[...]
    o_ref[...] = _sub_norm(full, gs_ref[...], lam_init).astype(o_ref.dtype)


def _sample_finish(g_sub, oacc, w_new, v_new, lam_init):
    db = oacc.shape[0]
    whole = pl.BlockSpec((db, N_HEADS, V_DIM), lambda i: (0, 0, 0))
    return pl.pallas_call(
        functools.partial(_sample_finish_kernel, lam_init=lam_init),
        out_shape=jax.ShapeDtypeStruct((db, N_HEADS, V_DIM), BF16),
        grid=(1,),
        in_specs=[pl.BlockSpec((1, V_DIM), lambda i: (0, 0)), whole, whole, whole],
        out_specs=whole,
        compiler_params=_cparams(("arbitrary",)),
        name="sample_finish",
    )(g_sub, oacc, w_new, v_new)


def _pool_matmul(pooled_groups, wp_ref, scale):
    outs = [jnp.dot(pg.astype(BF16), wp_ref[g], preferred_element_type=F32)
            for g, pg in enumerate(pooled_groups)]
    return jnp.concatenate(outs, axis=1) * scale


def _pool_prompt_kernel(u_ref, halo_ref, wp_ref, sc_ref, o_ref, ext_ref, *, ts):
    i = pl.program_id(1)
    group = wp_ref.shape[1]
    u = u_ref[...]
    ext_ref[pl.ds(POOL_HALO, ts), :] = u
    ext_ref[pl.ds(0, POOL_HALO), :] = jnp.where(i > 0, halo_ref[...], 0.0)
    pos1 = (i * ts + lax.broadcasted_iota(jnp.int32, (ts, 1), 0) + 1).astype(F32)
    pooled = []
    for g, win in enumerate(POOL_WINDOWS):
        cols = slice(g * group, (g + 1) * group)
        wsum = u[:, cols]
        for k in range(1, win):
            wsum = wsum + ext_ref[pl.ds(POOL_HALO - k, ts), cols]
        cnt = jnp.minimum(pos1, float(win))
        pooled.append(wsum / cnt - u[:, cols])
    o_ref[...] = _pool_matmul(pooled, wp_ref, sc_ref[...]).astype(o_ref.dtype)


def _pool_prompt(u, w_pool_bf, pool_scale, batch, seq, ts):
    m, width = u.shape
    nt = seq // ts
    r = ts // POOL_HALO
    return pl.pallas_call(
        functools.partial(_pool_prompt_kernel, ts=ts),
        out_shape=jax.ShapeDtypeStruct((m, width), BF16),
        grid=(batch, nt),
        in_specs=[pl.BlockSpec((ts, width), lambda b, i: (b * nt + i, 0)),
                  pl.BlockSpec((POOL_HALO, width), lambda b, i: (jnp.maximum((b * nt + i) * r - 1, 0), 0)),
                  pl.BlockSpec(w_pool_bf.shape, lambda b, i: (0, 0, 0)),
                  pl.BlockSpec((1, width), lambda b, i: (0, 0))],
        out_specs=pl.BlockSpec((ts, width), lambda b, i: (b * nt + i, 0)),
        scratch_shapes=[pltpu.VMEM((ts + POOL_HALO, width), F32)],
        compiler_params=_cparams(("arbitrary", "arbitrary")),
        name="pool_prompt",
    )(u, u, w_pool_bf, pool_scale)


def _pool_sample_kernel(u_ref, st_ref, wp_ref, sc_ref, o_ref):
    group = wp_ref.shape[1]
    u = u_ref[...]
    pooled = []
    for g, win in enumerate(POOL_WINDOWS):
        cols = slice(g * group, (g + 1) * group)
        wsum = u[:, cols]
        for k in range(1, win):
            wsum = wsum + st_ref[POOL_STATE - k, :, cols]
        pooled.append(wsum / float(win) - u[:, cols])
    o_ref[...] = _pool_matmul(pooled, wp_ref, sc_ref[...]).astype(o_ref.dtype)


def _pool_sample(u, state_t, w_pool_bf, pool_scale, tb):
    m, width = u.shape
    return pl.pallas_call(
        _pool_sample_kernel,
        out_shape=jax.ShapeDtypeStruct((m, width), BF16),
        grid=(m // tb,),
        in_specs=[pl.BlockSpec((tb, width), lambda i: (i, 0)),
                  pl.BlockSpec((POOL_STATE, tb, width), lambda i: (0, i, 0)),
                  pl.BlockSpec(w_pool_bf.shape, lambda i: (0, 0, 0)),
                  pl.BlockSpec((1, width), lambda i: (0, 0))],
        out_specs=pl.BlockSpec((tb, width), lambda i: (i, 0)),
        compiler_params=_cparams(("arbitrary",)),
        name="pool_sample",
    )(u, state_t, w_pool_bf, pool_scale)


def _out_proj_kernel(o_ref, p_ref, w_ref, h_ref, out_ref):
    half = o_ref.shape[1]
    acc = jnp.dot(o_ref[...], w_ref[pl.ds(0, half), :], preferred_element_type=F32)
    acc += jnp.dot(p_ref[...], w_ref[pl.ds(half, half), :], preferred_element_type=F32)
    out_ref[...] = h_ref[...] + acc


def _out_proj(o, pool, w_out_bf, h, tm):
    m, d = h.shape
    row = lambda i: (i, 0)
    return pl.pallas_call(
        _out_proj_kernel,
        out_shape=jax.ShapeDtypeStruct((m, d), F32),
        grid=(m // tm,),
        in_specs=[pl.BlockSpec((tm, o.shape[1]), row),
                  pl.BlockSpec((tm, pool.shape[1]), row),
                  pl.BlockSpec(w_out_bf.shape, lambda i: (0, 0)),
                  pl.BlockSpec((tm, d), row)],
        out_specs=pl.BlockSpec((tm, d), row),
        compiler_params=_cparams(("arbitrary",)),
        name="out_proj",
    )(o, pool, w_out_bf, h)


def _ffn_body(h_ref, g_ref, wg_ref, wu_ref, wd_ref, out_ref, xn_ref, acc_ref):
    f = pl.program_id(1)

    @pl.when(f == 0)
    def _():
        xn_ref[...] = _rms(h_ref[...], g_ref[...]).astype(BF16)
        acc_ref[...] = jnp.zeros(acc_ref.shape, F32)

    xn = xn_ref[...]
    gate = jnp.dot(xn, wg_ref[...], preferred_element_type=F32)
    up = jnp.dot(xn, wu_ref[...], preferred_element_type=F32)
    act = (gate * jax.nn.sigmoid(gate) * up).astype(BF16)
    acc_ref[...] += jnp.dot(act, wd_ref[...], preferred_element_type=F32)

    @pl.when(f == pl.num_programs(1) - 1)
    def _():
        out_ref[...] = h_ref[...] + acc_ref[...]


def _ffn_kernel(h_ref, g_ref, wg_ref, wu_ref, wd_ref, out_ref, xn_ref, acc_ref):
    _ffn_body(h_ref, g_ref, wg_ref, wu_ref, wd_ref, out_ref, xn_ref, acc_ref)


def _ffn_paged_kernel(page_ids_ref, h_ref, g_ref, wg_ref, wu_ref, wd_ref, wsel_ref, *rest,
                      per_step, pages_per_seq, n_valid_pages):
    v_refs = rest[:per_step]
    out_ref, oacc_ref, xn_ref, acc_ref = rest[per_step:]
    step = pl.program_id(0) * pl.num_programs(1) + pl.program_id(1)
    _paged_values_step(step, page_ids_ref, wsel_ref, v_refs, oacc_ref,
                       pages_per_seq=pages_per_seq, n_valid_pages=n_valid_pages)
    _ffn_body(h_ref, g_ref, wg_ref, wu_ref, wd_ref, out_ref, xn_ref, acc_ref)


def _ffn_specs(d, tm, tf, nf, index):
    return [pl.BlockSpec((tm, d), index(lambda i, f: (i, 0))),
            pl.BlockSpec((1, d), index(lambda i, f: (0, 0))),
            pl.BlockSpec((d, tf), index(lambda i, f: (0, f))),
            pl.BlockSpec((d, tf), index(lambda i, f: (0, nf + f))),
            pl.BlockSpec((tf, d), index(lambda i, f: (f, 0)))]


def _ffn(h, g_ffn, w_gate_up_bf, w_down_bf, tm, tf):
    m, d = h.shape
    nf = w_down_bf.shape[0] // tf
    plain = lambda fn: fn
    return pl.pallas_call(
        _ffn_kernel,
        out_shape=jax.ShapeDtypeStruct((m, d), F32),
        grid=(m // tm, nf),
        in_specs=_ffn_specs(d, tm, tf, nf, plain),
        out_specs=pl.BlockSpec((tm, d), lambda i, f: (i, 0)),
        scratch_shapes=[pltpu.VMEM((tm, d), BF16), pltpu.VMEM((tm, d), F32)],
        compiler_params=_cparams(("arbitrary", "arbitrary")),
        name="ffn",
    )(h, g_ffn, w_gate_up_bf, w_gate_up_bf, w_down_bf)


def _paged_plan(n_seq, pages_per_seq, n_steps):
    per_step = -(-n_seq * pages_per_seq // n_steps)
    weight_steps = -(-n_steps * per_step // pages_per_seq)
    return per_step, weight_steps


def _ffn_paged(h, g_ffn, w_gate_up_bf, w_down_bf, tm, tf, page_table, wsel, cache_v, li):
    m, d = h.shape
    nf = w_down_bf.shape[0] // tf
    n_steps = (m // tm) * nf
    db, pages_per_seq = page_table.shape
    page = cache_v.shape[2]
    n_valid = db * pages_per_seq
    per_step, _ = _paged_plan(db, pages_per_seq, n_steps)
    flat = page_table.reshape(-1)
    page_ids = jnp.concatenate([flat, jnp.broadcast_to(flat[-1:], (n_steps * per_step - n_valid,))])
    with_ids = lambda fn: (lambda i, f, ids: fn(i, f))

    def v_spec(s):
        return pl.BlockSpec((None, None, page, N_HEADS, V_DIM),
                            lambda i, f, ids: (li, ids[(i * nf + f) * per_step + s], 0, 0, 0))

    grid_spec = pltpu.PrefetchScalarGridSpec(
        num_scalar_prefetch=1,
        grid=(m // tm, nf),
        in_specs=_ffn_specs(d, tm, tf, nf, with_ids)
                 + [pl.BlockSpec((per_step * N_HEADS, page * N_HEADS), lambda i, f, ids: (i * nf + f, 0))]
                 + [v_spec(s) for s in range(per_step)],
        out_specs=[pl.BlockSpec((tm, d), lambda i, f, ids: (i, 0)),
                   pl.BlockSpec((db, N_HEADS, V_DIM), lambda i, f, ids: (0, 0, 0))],
        scratch_shapes=[pltpu.VMEM((tm, d), BF16), pltpu.VMEM((tm, d), F32)],
    )
    return pl.pallas_call(
        functools.partial(_ffn_paged_kernel, per_step=per_step, pages_per_seq=pages_per_seq,
                          n_valid_pages=n_valid),
        out_shape=[jax.ShapeDtypeStruct((m, d), F32), jax.ShapeDtypeStruct((db, N_HEADS, V_DIM), F32)],
        grid_spec=grid_spec,
        compiler_params=_cparams(("arbitrary", "arbitrary")),
        name="ffn_paged",
    )(page_ids, h, g_ffn, w_gate_up_bf, w_gate_up_bf, w_down_bf, wsel, *([cache_v] * per_step))


def _ple_kernel(h_ref, p_ref, gp_ref, wg_ref, wp_ref, gf_ref, y_ref, *, final_norm):
    h = h_ref[...]
    xn = _rms(h, gp_ref[...]).astype(BF16)
    gate = jax.nn.sigmoid(jnp.dot(xn, wg_ref[...], preferred_element_type=F32))
    proj = jnp.dot(p_ref[...].astype(BF16), wp_ref[...], preferred_element_type=F32)
    y = h + gate * proj
    y_ref[...] = _rms(y, gf_ref[...]) if final_norm else y


def _ple(h, p, g_ple, w_gate_bf, w_proj_bf, g_final, tm, final_norm):
    m, d = h.shape
    row = lambda i: (i, 0)
    const = lambda i: (0, 0)
    return pl.pallas_call(
        functools.partial(_ple_kernel, final_norm=final_norm),
        out_shape=jax.ShapeDtypeStruct((m, d), F32),
        grid=(m // tm,),
        in_specs=[pl.BlockSpec((tm, d), row),
                  pl.BlockSpec((tm, p.shape[1]), row),
                  pl.BlockSpec((1, d), const),
                  pl.BlockSpec(w_gate_bf.shape, const),
                  pl.BlockSpec(w_proj_bf.shape, const),
                  pl.BlockSpec((1, d), const)],
        out_specs=pl.BlockSpec((tm, d), row),
        compiler_params=_cparams(("arbitrary",)),
        name="ple_final",
    )(h, p, g_ple, w_gate_bf, w_proj_bf, g_final)


def _rope_tables(positions):
    half = HEAD_DIM // 2
    inv = ROPE_THETA ** (-jnp.arange(half, dtype=F32) / half)
    ang = positions.astype(F32)[:, None] * inv[None, :]
    cos, sin = jnp.cos(ang), jnp.sin(ang)
    reps = LANES // HEAD_DIM
    return jnp.tile(jnp.concatenate([cos, cos], axis=1), (1, reps)), \
        jnp.tile(jnp.concatenate([-sin, sin], axis=1), (1, reps))


def _row_tile(m, want):
    return want if m % want == 0 else m


def kernel(x_prompt, x_sample, p_prompt, p_sample, cache_k, cache_v, state_pool, page_table, g_mix, w_in, w_out, lambda_q1, lambda_k1, lambda_q2, lambda_k2, g_sub, w_pool, pool_scale, g_ffn, w_gate_up, w_down, g_ple, w_ple_gate, w_ple_proj, g_final):
    batch, seq, d = x_prompt.shape
    db, ds, _ = x_sample.shape
    depth = g_mix.shape[0]
    n_pages = page_table.shape[1]
    page = cache_k.shape[2]
    past = n_pages * page
    assert ds == 1 and d == 2 * ATTN_WIDTH

    hp = x_prompt.reshape(batch * seq, d)
    hs = x_sample.reshape(db * ds, d)
    cos_p, sin_p = _rope_tables(jnp.arange(seq))
    cos_s, sin_s = _rope_tables(jnp.full((db,), past))
    row2 = lambda a: a.reshape(1, -1)

    tm_p = _row_tile(batch * seq, 512)
    tq = _row_tile(seq, 512)
    ts = _row_tile(seq, 512)
    cache_kt = jnp.transpose(cache_k, (0, 1, 3, 4, 2))
    tf = 512
    qk_scale = HEAD_DIM ** -0.5

    outs = {k: [] for k in ("kp", "vp", "pp", "ks", "vs", "ps")}
    for li in range(depth):
        lam_init = 0.8 - 0.6 * math.exp(-0.3 * li)
        w_in_bf = w_in[li].astype(BF16)
        w_out_bf = w_out[li].astype(BF16)
        w_pool_bf = w_pool[li].astype(BF16)
        w_gu_bf = w_gate_up[li].astype(BF16)
        w_down_bf = w_down[li].astype(BF16)
        w_pg_bf = w_ple_gate[li].astype(BF16)
        w_pp_bf = w_ple_proj[li].astype(BF16)
        lams = tuple(row2(a[li]) for a in (lambda_q1, lambda_k1, lambda_q2, lambda_k2))
        gsub = row2(g_sub[li])

        qs, kfs, vfs, us = _in_proj_sample(hs, row2(g_mix[li]), w_in_bf, cos_s, sin_s, qk_scale)
        col_sub = jnp.arange(ATTN_WIDTH) // HEAD_DIM
        row_sub = (jnp.arange(N_QK_HEADS) % N_HEADS) * 2 + jnp.arange(N_QK_HEADS) // N_HEADS
        qrows = qs.astype(F32)[:, None, :] * (col_sub[None, :] == row_sub[:, None]).astype(F32)[None]
        ffn_steps = (batch * seq // tm_p) * (w_down_bf.shape[0] // tf)
        _, weight_blocks = _paged_plan(db, n_pages, ffn_steps)
        wsel_block = n_pages * N_HEADS

        qt, kb, kt, vt, vf, u = _in_proj_prompt(hp, row2(g_mix[li]), w_in_bf, cos_p, sin_p, batch, seq, tq,
                                                qk_scale * math.log2(math.e))
        o, wsel, w_new = _prompt_attention(qt, kb, vt, lams, g_sub[li].reshape(-1, 1), batch, seq, tq, lam_init,
                                           page_table, qrows, kfs.reshape(db, 1, ATTN_WIDTH), cache_kt, li,
                                           weight_blocks * wsel_block)
        if weight_blocks > db:
            wsel = _zero_tail(wsel, wsel_block, db)
        pool = _pool_prompt(u, w_pool_bf, row2(pool_scale[li]), batch, seq, ts)
        h1 = _out_proj(o, pool, w_out_bf, hp, tm_p)
        h2, oacc = _ffn_paged(h1, row2(g_ffn[li]), w_gu_bf, w_down_bf, tm_p, tf, page_table, wsel, cache_v, li)
        outs["kp"].append(jnp.transpose(kt.reshape(batch, N_QK_HEADS, HEAD_DIM, seq), (0, 3, 1, 2)))
        outs["vp"].append(vf.reshape(batch, seq, N_HEADS, V_DIM))
        outs["pp"].append(u.reshape(batch, seq, -1)[:, seq - POOL_STATE:])
        p_rows = p_prompt[li].reshape(batch * seq, -1)
        last = li == depth - 1
        gfin = row2(g_final)
        hp_next = _ple(h2, p_rows, row2(g_ple[li]), w_pg_bf, w_pp_bf, gfin, tm_p, last)

        o_s = _sample_finish(gsub, oacc, w_new, vfs.reshape(db, N_HEADS, V_DIM), lam_init).reshape(db, -1)
        state = state_pool[li]
        pool_s = _pool_sample(us, jnp.swapaxes(state, 0, 1), w_pool_bf, row2(pool_scale[li]), db)
        h1s = _out_proj(o_s, pool_s, w_out_bf, hs, db)
        h2s = _ffn(h1s, row2(g_ffn[li]), w_gu_bf, w_down_bf, db, tf)
        hs_next = _ple(h2s, p_sample[li].reshape(db * ds, -1), row2(g_ple[li]), w_pg_bf, w_pp_bf, gfin, db, last)
        outs["ks"].append(kfs.reshape(db, ds, N_QK_HEADS, HEAD_DIM))
        outs["vs"].append(vfs.reshape(db, ds, N_HEADS, V_DIM))
        outs["ps"].append(jnp.concatenate([state[:, 1:], us[:, None, :]], axis=1))
        hp, hs = hp_next, hs_next

    y_prompt = hp.reshape(batch, seq, d)
    y_sample = hs.reshape(db, ds, d)
    return (y_prompt, y_sample, jnp.stack(outs["kp"]), jnp.stack(outs["vp"]), jnp.stack(outs["pp"]),
            jnp.stack(outs["ks"]), jnp.stack(outs["vs"]), jnp.stack(outs["ps"]))
```

```python
import functools
import math

import jax
import jax.numpy as jnp
from jax import lax
from jax.experimental import pallas as pl
from jax.experimental.pallas import tpu as pltpu

F32 = jnp.float32
BF16 = jnp.bfloat16

N_HEADS = 8
HEAD_DIM = 64
V_DIM = 2 * HEAD_DIM
N_QK_HEADS = 2 * N_HEADS
ATTN_WIDTH = N_HEADS * V_DIM
POOL_WINDOWS = (2, 4, 8, 16)
POOL_STATE = max(POOL_WINDOWS) - 1
POOL_HALO = 16
ROPE_THETA = 10000.0
EPS = 1e-6
LANES = 128
MXU_COLS = 256
VMEM_LIMIT = 56 * 1024 * 1024


def _cparams(sem):
    return pltpu.CompilerParams(dimension_semantics=sem, vmem_limit_bytes=VMEM_LIMIT)


def _rms(x, g):
    ms = jnp.mean(x * x, axis=-1, keepdims=True)
    return x * lax.rsqrt(ms + EPS) * g


def _rope_chunk(x, cos, sin_signed, first_half):
    swapped = jnp.where(first_half, pltpu.roll(x, 96, 1), pltpu.roll(x, 32, 1))
    return x * cos + swapped * sin_signed


def _col_chunks(xn_ref, w_ref):
    xn = xn_ref[...]
    for c in range(w_ref.shape[1] // MXU_COLS):
        cols = slice(c * MXU_COLS, (c + 1) * MXU_COLS)
        yield cols, jnp.dot(xn, w_ref[:, cols], preferred_element_type=F32)


def _rope(a, cos_ref, sin_ref):
    cos = cos_ref[...]
    sin = sin_ref[...]
    lane = lax.broadcasted_iota(jnp.int32, cos.shape, 1)
    first_half = (lane % HEAD_DIM) < (HEAD_DIM // 2)
    parts = [_rope_chunk(a[:, c * LANES:(c + 1) * LANES], cos, sin, first_half)
             for c in range(a.shape[1] // LANES)]
    return jnp.concatenate(parts, axis=1)


def _in_proj_prompt_kernel(x_ref, g_ref, w_ref, cos_ref, sin_ref,
                           qt_ref, kb_ref, kt_ref, vt_ref, vf_ref, u_ref, xn_ref, *, q_scale):
    width = qt_ref.shape[0]
    xn_ref[...] = _rms(x_ref[...], g_ref[...]).astype(BF16)
    for cols, acc in _col_chunks(xn_ref, w_ref):
        part, local = cols.start // width, slice(cols.start % width, cols.start % width + MXU_COLS)
        if part == 0:
            qt_ref[local, :] = (_rope(acc, cos_ref, sin_ref) * q_scale).T.astype(BF16)
        elif part == 1:
            k = _rope(acc, cos_ref, sin_ref)
            kb_ref[:, local] = k.astype(BF16)
            kt_ref[local, :] = k.T
        elif part == 2:
            vf_ref[:, local] = acc
            vt_ref[local, :] = acc.T.astype(BF16)
        else:
            u_ref[:, local] = acc


def _in_proj_prompt(x, g_mix, w_in_bf, cos_t, sin_t, batch, seq, tm, q_scale):
    m, d = x.shape
    width = ATTN_WIDTH
    nt = seq // tm
    row = lambda i: (i, 0)
    tile = lambda i: (i, 0, 0)
    tab = lambda i: (i % nt, 0)
    outs = [jax.ShapeDtypeStruct((m // tm, width, tm), BF16),
            jax.ShapeDtypeStruct((m, width), BF16),
            jax.ShapeDtypeStruct((batch, width, seq), F32),
            jax.ShapeDtypeStruct((m // tm, width, tm), BF16),
            jax.ShapeDtypeStruct((m, width), F32),
            jax.ShapeDtypeStruct((m, width), F32)]
    return pl.pallas_call(
        functools.partial(_in_proj_prompt_kernel, q_scale=q_scale),
        out_shape=outs,
        grid=(m // tm,),
        in_specs=[pl.BlockSpec((tm, d), row),
                  pl.BlockSpec((1, d), lambda i: (0, 0)),
                  pl.BlockSpec(w_in_bf.shape, lambda i: (0, 0), pipeline_mode=pl.Buffered(1)),
                  pl.BlockSpec((tm, LANES), tab),
                  pl.BlockSpec((tm, LANES), tab)],
        out_specs=[pl.BlockSpec((None, width, tm), tile),
                   pl.BlockSpec((tm, width), row),
                   pl.BlockSpec((None, width, tm), lambda i: (i // nt, 0, i % nt)),
                   pl.BlockSpec((None, width, tm), tile),
                   pl.BlockSpec((tm, width), row),
                   pl.BlockSpec((tm, width), row)],
        scratch_shapes=[pltpu.VMEM((tm, d), BF16)],
        compiler_params=_cparams(("arbitrary",)),
        name="in_proj_prompt",
    )(x, g_mix, w_in_bf, cos_t, sin_t)


def _in_proj_sample_kernel(x_ref, g_ref, w_ref, cos_ref, sin_ref, q_ref, k_ref, v_ref, u_ref, xn_ref, *, q_scale):
    n = pl.program_id(1)

    @pl.when(n == 0)
    def _():
        xn_ref[...] = _rms(x_ref[...], g_ref[...]).astype(BF16)

    acc = jnp.dot(xn_ref[...], w_ref[...], preferred_element_type=F32)

    @pl.when(n == 0)
    def _():
        q_ref[...] = (_rope(acc, cos_ref, sin_ref) * q_scale).astype(BF16)

    @pl.when(n == 1)
    def _():
        k_ref[...] = _rope(acc, cos_ref, sin_ref)

    @pl.when(n == 2)
    def _():
        v_ref[...] = acc

    @pl.when(n == 3)
    def _():
        u_ref[...] = acc


def _in_proj_sample(x, g_mix, w_in_bf, cos_t, sin_t, q_scale):
    m, d = x.shape
    width = ATTN_WIDTH
    whole = lambda i, n: (0, 0)
    outs = [jax.ShapeDtypeStruct((m, width), BF16)] + [jax.ShapeDtypeStruct((m, width), F32)] * 3
    return pl.pallas_call(
        functools.partial(_in_proj_sample_kernel, q_scale=q_scale),
        out_shape=outs,
        grid=(1, 4),
        in_specs=[pl.BlockSpec((m, d), whole),
                  pl.BlockSpec((1, d), whole),
                  pl.BlockSpec((d, width), lambda i, n: (0, n)),
                  pl.BlockSpec((m, LANES), whole),
                  pl.BlockSpec((m, LANES), whole)],
        out_specs=[pl.BlockSpec((m, width), whole)] * 4,
        scratch_shapes=[pltpu.VMEM((m, d), BF16)],
        compiler_params=_cparams(("arbitrary", "arbitrary")),
        name="in_proj_sample",
    )(x, g_mix, w_in_bf, cos_t, sin_t)


def _lambda_value(lq1, lk1, lq2, lk2, lam_init):
    a = jnp.sum(lq1 * lk1, axis=-1, keepdims=True)
    b = jnp.sum(lq2 * lk2, axis=-1, keepdims=True)
    return jnp.exp(a) - jnp.exp(b) + lam_init


def _sub_norm(o, g_sub, lam_init):
    return _rms(o, g_sub) * (1.0 - lam_init)


def _prompt_attn_kernel(pt_ref, lq1_ref, lk1_ref, lq2_ref, lk2_ref, gs_ref, qt_ref, k_ref, vt_ref,
                        qrows_ref, kn_ref, *rest, n_key_pages, tq, lam_init):
    key_pages = rest[:n_key_pages]
    o_ref, wsel_ref, wnew_ref, m0_ref, l0_ref, a0_ref, m1_ref, l1_ref, a1_ref = rest[n_key_pages:]
    qi = pl.program_id(2)
    hq = tq // 2
    qt = qt_ref[...]
    drow = lax.broadcasted_iota(jnp.int32, qt.shape, 0)
    zero = jnp.zeros_like(qt)
    qts = (jnp.where(drow < HEAD_DIM, qt, zero), jnp.where(drow >= HEAD_DIM, qt, zero))
    stats = ((m0_ref, l0_ref, a0_ref), (m1_ref, l1_ref, a1_ref))

    for m_ref, l_ref, a_ref in stats:
        m_ref[...] = jnp.full(m_ref.shape, -jnp.inf, F32)
        l_ref[...] = jnp.zeros(l_ref.shape, F32)
        a_ref[...] = jnp.zeros(a_ref.shape, F32)

    def run_chains(chains):
        scores = [jnp.dot(kj, qc[:, lanes], preferred_element_type=F32)
                  for kj, _, qc, _, lanes, _ in chains]
        probs = []
        for st, (_, _, _, (m_ref, l_ref, _), lanes, keep) in zip(scores, chains):
            if keep is not None:
                st = jnp.where(keep, st, -jnp.inf)
            m_old = m_ref[:, lanes]
            m_new = jnp.maximum(m_old, jnp.max(st, axis=0, keepdims=True))
            pt = jnp.exp2(st - m_new)
            alpha = jnp.exp2(m_old - m_new)
            l_ref[:, lanes] = alpha * l_ref[:, lanes] + jnp.sum(pt, axis=0, keepdims=True)
            m_ref[:, lanes] = m_new
            probs.append((pt.astype(BF16), alpha))
        for (pt, alpha), (_, vtj, _, (_, _, a_ref), lanes, _) in zip(probs, chains):
            a_ref[:, lanes] = alpha * a_ref[:, lanes] + jnp.dot(vtj, pt, preferred_element_type=F32)

    def body(j, carry):
        kj = k_ref[pl.ds(pl.multiple_of(j * tq, tq), tq), :]
        vtj = vt_ref[j]
        run_chains([(kj, vtj, qc, refs, slice(half * hq, (half + 1) * hq), None)
                    for qc, refs in zip(qts, stats) for half in range(2)])
        return carry

    lax.fori_loop(0, qi, body, 0)

    start = pl.multiple_of(qi * tq, tq)
    k_lo = k_ref[pl.ds(start, hq), :]
    k_all = k_ref[pl.ds(start, tq), :]
    vt_all = vt_ref[qi]
    tri = (lax.broadcasted_iota(jnp.int32, (hq, hq), 0) <= lax.broadcasted_iota(jnp.int32, (hq, hq), 1))
    late = (lax.broadcasted_iota(jnp.int32, (tq, hq), 0) <= lax.broadcasted_iota(jnp.int32, (tq, hq), 1) + hq)
    diagonal = []
    for qc, refs in zip(qts, stats):
        diagonal.append((k_lo, vt_all[:, :hq], qc, refs, slice(0, hq), tri))
        diagonal.append((k_all, vt_all, qc, refs, slice(hq, tq), late))
    run_chains(diagonal)

    lam = _lambda_value(lq1_ref[...], lk1_ref[...], lq2_ref[...], lk2_ref[...], lam_init)
    wsel, w_new = _sequence_weights(qrows_ref[0], kn_ref[0], key_pages, lam)
    wsel_ref[...] = wsel
    wnew_ref[0] = jnp.broadcast_to(w_new, wnew_ref.shape[1:])

    ot = a0_ref[...] * (1.0 / l0_ref[...]) - a1_ref[...] * (lam / l1_ref[...])
    ms = jnp.mean(ot * ot, axis=0, keepdims=True)
    ot = ot * lax.rsqrt(ms + EPS) * gs_ref[...] * (1.0 - lam_init)
    o_ref[...] = ot.T.astype(o_ref.dtype)


def _prompt_attention(qt, k, vt, lams, g_sub_col, batch, seq, tq, lam_init,
                      page_table, qrows, k_new, cache_kt, li, wsel_rows):
    nq = seq // tq
    db, n_pages = page_table.shape
    page = cache_kt.shape[4]
    assert batch * N_HEADS * nq >= db, "one sample sequence rides on each grid step"
    seq_of = lambda b, h, i: jnp.minimum((b * N_HEADS + h) * nq + i, db - 1)
    small = lambda shape: pl.BlockSpec(shape, lambda b, h, i, pt: (0, 0))
    per_seq = lambda shape: pl.BlockSpec(shape, lambda b, h, i, pt: (seq_of(b, h, i), 0, 0))

    def key_page(s):
        return pl.BlockSpec((None, None, N_QK_HEADS, HEAD_DIM, page),
                            lambda b, h, i, pt: (li, pt[seq_of(b, h, i), s], 0, 0, 0))

    grid_spec = pltpu.PrefetchScalarGridSpec(
        num_scalar_prefetch=1,
        grid=(batch, N_HEADS, nq),
        in_specs=[small((1, HEAD_DIM))] * 4 + [small((V_DIM, 1))] + [
            pl.BlockSpec((None, V_DIM, tq), lambda b, h, i, pt: (b * nq + i, h, 0)),
            pl.BlockSpec((seq, V_DIM), lambda b, h, i, pt: (b, h)),
            pl.BlockSpec((nq, V_DIM, tq), lambda b, h, i, pt: (b, h, 0)),
            per_seq((1, N_QK_HEADS, ATTN_WIDTH)),
            per_seq((1, 1, ATTN_WIDTH))] + [key_page(s) for s in range(n_pages)],
        out_specs=[pl.BlockSpec((tq, V_DIM), lambda b, h, i, pt: (b * nq + i, h)),
                   pl.BlockSpec((n_pages * N_HEADS, page * N_HEADS), lambda b, h, i, pt: (seq_of(b, h, i), 0)),
                   per_seq((1, N_HEADS, V_DIM))],
        scratch_shapes=[pltpu.VMEM((1, tq), F32), pltpu.VMEM((1, tq), F32), pltpu.VMEM((V_DIM, tq), F32)] * 2,
    )
    kernel = functools.partial(_prompt_attn_kernel, n_key_pages=n_pages, tq=tq, lam_init=lam_init)
    return pl.pallas_call(
        kernel,
        out_shape=[jax.ShapeDtypeStruct(k.shape, BF16),
                   jax.ShapeDtypeStruct((wsel_rows, page * N_HEADS), F32),
                   jax.ShapeDtypeStruct((db, N_HEADS, V_DIM), F32)],
        grid_spec=grid_spec,
        compiler_params=_cparams(("arbitrary", "arbitrary", "arbitrary")),
        name="prompt_attention",
    )(page_table, *lams, g_sub_col, qt, k, vt, qrows, k_new, *([cache_kt] * n_pages))


def _sequence_weights(qrows, k_new, k_refs, lam):
    n_pages = len(k_refs)
    _, _, page = k_refs[0].shape
    s = jnp.concatenate([jnp.dot(qrows, k_ref[...].reshape(N_QK_HEADS * HEAD_DIM, page),
                                 preferred_element_type=F32) for k_ref in k_refs], axis=1)
    s_new = jnp.sum(qrows * k_new, axis=-1, keepdims=True)
    m = jnp.maximum(jnp.max(s, axis=-1, keepdims=True), s_new)
    p = jnp.exp(s - m)
    p_new = jnp.exp(s_new - m)
    l = jnp.sum(p, axis=-1, keepdims=True) + p_new
    a = p / l
    a_new = p_new / l
    w = a[:N_HEADS] - lam * a[N_HEADS:]
    w_new = a_new[:N_HEADS] - lam * a_new[N_HEADS:]
    w_rows = jnp.concatenate([w[:, i * page:(i + 1) * page] for i in range(n_pages)], axis=0)
    tok = lax.broadcasted_iota(jnp.int32, (page, page * N_HEADS), 0)
    dst = lax.broadcasted_iota(jnp.int32, (page, page * N_HEADS), 1)
    spread = (dst // N_HEADS == tok).astype(F32)
    wide = jnp.dot(w_rows, spread, preferred_element_type=F32)
    row = lax.broadcasted_iota(jnp.int32, wide.shape, 0)
    col = lax.broadcasted_iota(jnp.int32, wide.shape, 1)
    return jnp.where(row % N_HEADS == col % N_HEADS, wide, 0.0), w_new


def _zero_tail_kernel(x_ref, o_ref):
    o_ref[...] = jnp.zeros(o_ref.shape, o_ref.dtype)


def _zero_tail(x, block_rows, first_block):
    n_blocks = x.shape[0] // block_rows - first_block
    return pl.pallas_call(
        _zero_tail_kernel,
        out_shape=jax.ShapeDtypeStruct(x.shape, x.dtype),
        grid=(n_blocks,),
        in_specs=[pl.BlockSpec(memory_space=pl.ANY)],
        out_specs=pl.BlockSpec((block_rows, x.shape[1]), lambda i: (first_block + i, 0)),
        input_output_aliases={0: 0},
        compiler_params=_cparams(("arbitrary",)),
        name="zero_tail",
    )(x)


def _paged_values_step(step, page_ids_ref, wsel_ref, v_refs, oacc_ref, *, pages_per_seq, n_valid_pages):
    per_step = len(v_refs)
    page = v_refs[0].shape[0]

    @pl.when(step == 0)
    def _():
        oacc_ref[...] = jnp.zeros(oacc_ref.shape, F32)

    for i, v_ref in enumerate(v_refs):
        flat = jnp.minimum(step * per_step + i, n_valid_pages - 1)
        seq = flat // pages_per_seq
        wsel = wsel_ref[pl.ds(i * N_HEADS, N_HEADS), :]
        oacc_ref[seq] += jnp.dot(wsel, v_ref[...].reshape(page * N_HEADS, V_DIM), preferred_element_type=F32)


def _sample_finish_kernel(gs_ref, oacc_ref, wnew_ref, vn_ref, o_ref, *, lam_init):
    full = oacc_ref[...] + wnew_ref[...] * vn_ref[...]
    o_ref[...] = _sub_norm(full, gs_ref[...], lam_init).astype(o_ref.dtype)


def _sample_finish(g_sub, oacc, w_new, v_new, lam_init):
    db = oacc.shape[0]
    whole = pl.BlockSpec((db, N_HEADS, V_DIM), lambda i: (0, 0, 0))
    return pl.pallas_call(
        functools.partial(_sample_finish_kernel, lam_init=lam_init),
        out_shape=jax.ShapeDtypeStruct((db, N_HEADS, V_DIM), BF16),
        grid=(1,),
        in_specs=[pl.BlockSpec((1, V_DIM), lambda i: (0, 0)), whole, whole, whole],
        out_specs=whole,
        compiler_params=_cparams(("arbitrary",)),
        name="sample_finish",
    )(g_sub, oacc, w_new, v_new)


def _pool_matmul(pooled_groups, wp_ref, scale):
    outs = [jnp.dot(pg.astype(BF16), wp_ref[g], preferred_element_type=F32)
            for g, pg in enumerate(pooled_groups)]
    return jnp.concatenate(outs, axis=1) * scale


def _pool_prompt_kernel(u_ref, halo_ref, wp_ref, sc_ref, o_ref, ext_ref, *, ts):
    i = pl.program_id(1)
    group = wp_ref.shape[1]
    u = u_ref[...]
    ext_ref[pl.ds(POOL_HALO, ts), :] = u
    ext_ref[pl.ds(0, POOL_HALO), :] = jnp.where(i > 0, halo_ref[...], 0.0)
    pos1 = (i * ts + lax.broadcasted_iota(jnp.int32, (ts, 1), 0) + 1).astype(F32)
    pooled = []
    for g, win in enumerate(POOL_WINDOWS):
        cols = slice(g * group, (g + 1) * group)
        wsum = u[:, cols]
        for k in range(1, win):
            wsum = wsum + ext_ref[pl.ds(POOL_HALO - k, ts), cols]
        cnt = jnp.minimum(pos1, float(win))
        pooled.append(wsum / cnt - u[:, cols])
    o_ref[...] = _pool_matmul(pooled, wp_ref, sc_ref[...]).astype(o_ref.dtype)


def _pool_prompt(u, w_pool_bf, pool_scale, batch, seq, ts):
    m, width = u.shape
    nt = seq // ts
    r = ts // POOL_HALO
    return pl.pallas_call(
        functools.partial(_pool_prompt_kernel, ts=ts),
        out_shape=jax.ShapeDtypeStruct((m, width), BF16),
        grid=(batch, nt),
        in_specs=[pl.BlockSpec((ts, width), lambda b, i: (b * nt + i, 0)),
                  pl.BlockSpec((POOL_HALO, width), lambda b, i: (jnp.maximum((b * nt + i) * r - 1, 0), 0)),
                  pl.BlockSpec(w_pool_bf.shape, lambda b, i: (0, 0, 0)),
                  pl.BlockSpec((1, width), lambda b, i: (0, 0))],
        out_specs=pl.BlockSpec((ts, width), lambda b, i: (b * nt + i, 0)),
        scratch_shapes=[pltpu.VMEM((ts + POOL_HALO, width), F32)],
        compiler_params=_cparams(("arbitrary", "arbitrary")),
        name="pool_prompt",
    )(u, u, w_pool_bf, pool_scale)


def _pool_sample_kernel(u_ref, st_ref, wp_ref, sc_ref, o_ref):
    group = wp_ref.shape[1]
    u = u_ref[...]
    pooled = []
    for g, win in enumerate(POOL_WINDOWS):
        cols = slice(g * group, (g + 1) * group)
        wsum = u[:, cols]
        for k in range(1, win):
            wsum = wsum + st_ref[POOL_STATE - k, :, cols]
        pooled.append(wsum / float(win) - u[:, cols])
    o_ref[...] = _pool_matmul(pooled, wp_ref, sc_ref[...]).astype(o_ref.dtype)


def _pool_sample(u, state_t, w_pool_bf, pool_scale, tb):
    m, width = u.shape
    return pl.pallas_call(
        _pool_sample_kernel,
        out_shape=jax.ShapeDtypeStruct((m, width), BF16),
        grid=(m // tb,),
        in_specs=[pl.BlockSpec((tb, width), lambda i: (i, 0)),
                  pl.BlockSpec((POOL_STATE, tb, width), lambda i: (0, i, 0)),
                  pl.BlockSpec(w_pool_bf.shape, lambda i: (0, 0, 0)),
                  pl.BlockSpec((1, width), lambda i: (0, 0))],
        out_specs=pl.BlockSpec((tb, width), lambda i: (i, 0)),
        compiler_params=_cparams(("arbitrary",)),
        name="pool_sample",
    )(u, state_t, w_pool_bf, pool_scale)


def _out_proj_kernel(o_ref, p_ref, w_ref, h_ref, out_ref):
    half = o_ref.shape[1]
    acc = jnp.dot(o_ref[...], w_ref[pl.ds(0, half), :], preferred_element_type=F32)
    acc += jnp.dot(p_ref[...], w_ref[pl.ds(half, half), :], preferred_element_type=F32)
    out_ref[...] = h_ref[...] + acc


def _out_proj(o, pool, w_out_bf, h, tm):
    m, d = h.shape
    row = lambda i: (i, 0)
    return pl.pallas_call(
        _out_proj_kernel,
        out_shape=jax.ShapeDtypeStruct((m, d), F32),
        grid=(m // tm,),
        in_specs=[pl.BlockSpec((tm, o.shape[1]), row),
                  pl.BlockSpec((tm, pool.shape[1]), row),
                  pl.BlockSpec(w_out_bf.shape, lambda i: (0, 0)),
                  pl.BlockSpec((tm, d), row)],
        out_specs=pl.BlockSpec((tm, d), row),
        compiler_params=_cparams(("arbitrary",)),
        name="out_proj",
    )(o, pool, w_out_bf, h)


def _ffn_body(h_ref, g_ref, wg_ref, wu_ref, wd_ref, out_ref, xn_ref, acc_ref):
    f = pl.program_id(1)

    @pl.when(f == 0)
    def _():
        xn_ref[...] = _rms(h_ref[...], g_ref[...]).astype(BF16)
        acc_ref[...] = jnp.zeros(acc_ref.shape, F32)

    xn = xn_ref[...]
    gate = jnp.dot(xn, wg_ref[...], preferred_element_type=F32)
    up = jnp.dot(xn, wu_ref[...], preferred_element_type=F32)
    act = (gate * jax.nn.sigmoid(gate) * up).astype(BF16)
    acc_ref[...] += jnp.dot(act, wd_ref[...], preferred_element_type=F32)

    @pl.when(f == pl.num_programs(1) - 1)
    def _():
        out_ref[...] = h_ref[...] + acc_ref[...]


def _ffn_kernel(h_ref, g_ref, wg_ref, wu_ref, wd_ref, out_ref, xn_ref, acc_ref):
    _ffn_body(h_ref, g_ref, wg_ref, wu_ref, wd_ref, out_ref, xn_ref, acc_ref)


def _ffn_paged_kernel(page_ids_ref, h_ref, g_ref, wg_ref, wu_ref, wd_ref, wsel_ref, *rest,
                      per_step, pages_per_seq, n_valid_pages):
    v_refs = rest[:per_step]
    out_ref, oacc_ref, xn_ref, acc_ref = rest[per_step:]
    step = pl.program_id(0) * pl.num_programs(1) + pl.program_id(1)
    _paged_values_step(step, page_ids_ref, wsel_ref, v_refs, oacc_ref,
                       pages_per_seq=pages_per_seq, n_valid_pages=n_valid_pages)
    _ffn_body(h_ref, g_ref, wg_ref, wu_ref, wd_ref, out_ref, xn_ref, acc_ref)


def _ffn_specs(d, tm, tf, nf, index):
    return [pl.BlockSpec((tm, d), index(lambda i, f: (i, 0))),
            pl.BlockSpec((1, d), index(lambda i, f: (0, 0))),
            pl.BlockSpec((d, tf), index(lambda i, f: (0, f))),
            pl.BlockSpec((d, tf), index(lambda i, f: (0, nf + f))),
            pl.BlockSpec((tf, d), index(lambda i, f: (f, 0)))]


def _ffn(h, g_ffn, w_gate_up_bf, w_down_bf, tm, tf):
    m, d = h.shape
    nf = w_down_bf.shape[0] // tf
    plain = lambda fn: fn
    return pl.pallas_call(
        _ffn_kernel,
        out_shape=jax.ShapeDtypeStruct((m, d), F32),
        grid=(m // tm, nf),
        in_specs=_ffn_specs(d, tm, tf, nf, plain),
        out_specs=pl.BlockSpec((tm, d), lambda i, f: (i, 0)),
        scratch_shapes=[pltpu.VMEM((tm, d), BF16), pltpu.VMEM((tm, d), F32)],
        compiler_params=_cparams(("arbitrary", "arbitrary")),
        name="ffn",
    )(h, g_ffn, w_gate_up_bf, w_gate_up_bf, w_down_bf)


def _paged_plan(n_seq, pages_per_seq, n_steps):
    per_step = -(-n_seq * pages_per_seq // n_steps)
    weight_steps = -(-n_steps * per_step // pages_per_seq)
    return per_step, weight_steps


def _ffn_paged(h, g_ffn, w_gate_up_bf, w_down_bf, tm, tf, page_table, wsel, cache_v, li):
    m, d = h.shape
    nf = w_down_bf.shape[0] // tf
    n_steps = (m // tm) * nf
    db, pages_per_seq = page_table.shape
    page = cache_v.shape[2]
    n_valid = db * pages_per_seq
    per_step, _ = _paged_plan(db, pages_per_seq, n_steps)
    flat = page_table.reshape(-1)
    page_ids = jnp.concatenate([flat, jnp.broadcast_to(flat[-1:], (n_steps * per_step - n_valid,))])
    with_ids = lambda fn: (lambda i, f, ids: fn(i, f))

    def v_spec(s):
        return pl.BlockSpec((None, None, page, N_HEADS, V_DIM),
                            lambda i, f, ids: (li, ids[(i * nf + f) * per_step + s], 0, 0, 0))

    grid_spec = pltpu.PrefetchScalarGridSpec(
        num_scalar_prefetch=1,
        grid=(m // tm, nf),
        in_specs=_ffn_specs(d, tm, tf, nf, with_ids)
                 + [pl.BlockSpec((per_step * N_HEADS, page * N_HEADS), lambda i, f, ids: (i * nf + f, 0))]
                 + [v_spec(s) for s in range(per_step)],
        out_specs=[pl.BlockSpec((tm, d), lambda i, f, ids: (i, 0)),
                   pl.BlockSpec((db, N_HEADS, V_DIM), lambda i, f, ids: (0, 0, 0))],
        scratch_shapes=[pltpu.VMEM((tm, d), BF16), pltpu.VMEM((tm, d), F32)],
    )
    return pl.pallas_call(
        functools.partial(_ffn_paged_kernel, per_step=per_step, pages_per_seq=pages_per_seq,
                          n_valid_pages=n_valid),
        out_shape=[jax.ShapeDtypeStruct((m, d), F32), jax.ShapeDtypeStruct((db, N_HEADS, V_DIM), F32)],
        grid_spec=grid_spec,
        compiler_params=_cparams(("arbitrary", "arbitrary")),
        name="ffn_paged",
    )(page_ids, h, g_ffn, w_gate_up_bf, w_gate_up_bf, w_down_bf, wsel, *([cache_v] * per_step))


def _ple_kernel(h_ref, p_ref, gp_ref, wg_ref, wp_ref, gf_ref, y_ref, *, final_norm):
    h = h_ref[...]
    xn = _rms(h, gp_ref[...]).astype(BF16)
    gate = jax.nn.sigmoid(jnp.dot(xn, wg_ref[...], preferred_element_type=F32))
    proj = jnp.dot(p_ref[...].astype(BF16), wp_ref[...], preferred_element_type=F32)
    y = h + gate * proj
    y_ref[...] = _rms(y, gf_ref[...]) if final_norm else y


def _ple(h, p, g_ple, w_gate_bf, w_proj_bf, g_final, tm, final_norm):
    m, d = h.shape
    row = lambda i: (i, 0)
    const = lambda i: (0, 0)
    return pl.pallas_call(
        functools.partial(_ple_kernel, final_norm=final_norm),
        out_shape=jax.ShapeDtypeStruct((m, d), F32),
        grid=(m // tm,),
        in_specs=[pl.BlockSpec((tm, d), row),
                  pl.BlockSpec((tm, p.shape[1]), row),
                  pl.BlockSpec((1, d), const),
                  pl.BlockSpec(w_gate_bf.shape, const),
                  pl.BlockSpec(w_proj_bf.shape, const),
                  pl.BlockSpec((1, d), const)],
        out_specs=pl.BlockSpec((tm, d), row),
        compiler_params=_cparams(("arbitrary",)),
        name="ple_final",
    )(h, p, g_ple, w_gate_bf, w_proj_bf, g_final)


def _rope_tables(positions):
    half = HEAD_DIM // 2
    inv = ROPE_THETA ** (-jnp.arange(half, dtype=F32) / half)
    ang = positions.astype(F32)[:, None] * inv[None, :]
    cos, sin = jnp.cos(ang), jnp.sin(ang)
    reps = LANES // HEAD_DIM
    return jnp.tile(jnp.concatenate([cos, cos], axis=1), (1, reps)), \
        jnp.tile(jnp.concatenate([-sin, sin], axis=1), (1, reps))


def _row_tile(m, want):
    return want if m % want == 0 else m


def kernel(x_prompt, x_sample, p_prompt, p_sample, cache_k, cache_v, state_pool, page_table, g_mix, w_in, w_out, lambda_q1, lambda_k1, lambda_q2, lambda_k2, g_sub, w_pool, pool_scale, g_ffn, w_gate_up, w_down, g_ple, w_ple_gate, w_ple_proj, g_final):
    batch, seq, d = x_prompt.shape
    db, ds, _ = x_sample.shape
    depth = g_mix.shape[0]
    n_pages = page_table.shape[1]
    page = cache_k.shape[2]
    past = n_pages * page
    assert ds == 1 and d == 2 * ATTN_WIDTH

    hp = x_prompt.reshape(batch * seq, d)
    hs = x_sample.reshape(db * ds, d)
    cos_p, sin_p = _rope_tables(jnp.arange(seq))
    cos_s, sin_s = _rope_tables(jnp.full((db,), past))
    row2 = lambda a: a.reshape(1, -1)

    tm_p = _row_tile(batch * seq, 512)
    tq = _row_tile(seq, 512)
    ts = _row_tile(seq, 512)
    cache_kt = jnp.transpose(cache_k, (0, 1, 3, 4, 2))
    tf = 512
    qk_scale = HEAD_DIM ** -0.5

    outs = {k: [] for k in ("kp", "vp", "pp", "ks", "vs", "ps")}
    for li in range(depth):
        lam_init = 0.8 - 0.6 * math.exp(-0.3 * li)
        w_in_bf = w_in[li].astype(BF16)
        w_out_bf = w_out[li].astype(BF16)
        w_pool_bf = w_pool[li].astype(BF16)
        w_gu_bf = w_gate_up[li].astype(BF16)
        w_down_bf = w_down[li].astype(BF16)
        w_pg_bf = w_ple_gate[li].astype(BF16)
        w_pp_bf = w_ple_proj[li].astype(BF16)
        lams = tuple(row2(a[li]) for a in (lambda_q1, lambda_k1, lambda_q2, lambda_k2))
        gsub = row2(g_sub[li])

        qs, kfs, vfs, us = _in_proj_sample(hs, row2(g_mix[li]), w_in_bf, cos_s, sin_s, qk_scale)
        col_sub = jnp.arange(ATTN_WIDTH) // HEAD_DIM
        row_sub = (jnp.arange(N_QK_HEADS) % N_HEADS) * 2 + jnp.arange(N_QK_HEADS) // N_HEADS
        qrows = qs.astype(F32)[:, None, :] * (col_sub[None, :] == row_sub[:, None]).astype(F32)[None]
        ffn_steps = (batch * seq // tm_p) * (w_down_bf.shape[0] // tf)
        _, weight_blocks = _paged_plan(db, n_pages, ffn_steps)
        wsel_block = n_pages * N_HEADS

        qt, kb, kt, vt, vf, u = _in_proj_prompt(hp, row2(g_mix[li]), w_in_bf, cos_p, sin_p, batch, seq, tq,
                                                qk_scale * math.log2(math.e))
        o, wsel, w_new = _prompt_attention(qt, kb, vt, lams, g_sub[li].reshape(-1, 1), batch, seq, tq, lam_init,
                                           page_table, qrows, kfs.reshape(db, 1, ATTN_WIDTH), cache_kt, li,
                                           weight_blocks * wsel_block)
        if weight_blocks > db:
            wsel = _zero_tail(wsel, wsel_block, db)
        pool = _pool_prompt(u, w_pool_bf, row2(pool_scale[li]), batch, seq, ts)
        h1 = _out_proj(o, pool, w_out_bf, hp, tm_p)
        h2, oacc = _ffn_paged(h1, row2(g_ffn[li]), w_gu_bf, w_down_bf, tm_p, tf, page_table, wsel, cache_v, li)
        outs["kp"].append(jnp.transpose(kt.reshape(batch, N_QK_HEADS, HEAD_DIM, seq), (0, 3, 1, 2)))
        outs["vp"].append(vf.reshape(batch, seq, N_HEADS, V_DIM))
        outs["pp"].append(u.reshape(batch, seq, -1)[:, seq - POOL_STATE:])
        p_rows = p_prompt[li].reshape(batch * seq, -1)
        last = li == depth - 1
        gfin = row2(g_final)
        hp_next = _ple(h2, p_rows, row2(g_ple[li]), w_pg_bf, w_pp_bf, gfin, tm_p, last)

        o_s = _sample_finish(gsub, oacc, w_new, vfs.reshape(db, N_HEADS, V_DIM), lam_init).reshape(db, -1)
        state = state_pool[li]
        pool_s = _pool_sample(us, jnp.swapaxes(state, 0, 1), w_pool_bf, row2(pool_scale[li]), db)
        h1s = _out_proj(o_s, pool_s, w_out_bf, hs, db)
        h2s = _ffn(h1s, row2(g_ffn[li]), w_gu_bf, w_down_bf, db, tf)
        hs_next = _ple(h2s, p_sample[li].reshape(db * ds, -1), row2(g_ple[li]), w_pg_bf, w_pp_bf, gfin, db, last)
        outs["ks"].append(kfs.reshape(db, ds, N_QK_HEADS, HEAD_DIM))
        outs["vs"].append(vfs.reshape(db, ds, N_HEADS, V_DIM))
        outs["ps"].append(jnp.concatenate([state[:, 1:], us[:, None, :]], axis=1))
        hp, hs = hp_next, hs_next

    y_prompt = hp.reshape(batch, seq, d)
    y_sample = hs.reshape(db, ds, d)
    return (y_prompt, y_sample, jnp.stack(outs["kp"]), jnp.stack(outs["vp"]), jnp.stack(outs["pp"]),
            jnp.stack(outs["ks"]), jnp.stack(outs["vs"]), jnp.stack(outs["ps"]))
```

```python
import functools
import math

import jax
import jax.numpy as jnp
from jax import lax
from jax.experimental import pallas as pl
from jax.experimental.pallas import tpu as pltpu

F32 = jnp.float32
BF16 = jnp.bfloat16

N_HEADS = 8
HEAD_DIM = 64
V_DIM = 2 * HEAD_DIM
N_QK_HEADS = 2 * N_HEADS
ATTN_WIDTH = N_HEADS * V_DIM
POOL_WINDOWS = (2, 4, 8, 16)
POOL_STATE = max(POOL_WINDOWS) - 1
POOL_HALO = 16
ROPE_THETA = 10000.0
EPS = 1e-6
LANES = 128
MXU_COLS = 256
VMEM_LIMIT = 56 * 1024 * 1024


def _cparams(sem):
    return pltpu.CompilerParams(dimension_semantics=sem, vmem_limit_bytes=VMEM_LIMIT)


def _rms(x, g):
    ms = jnp.mean(x * x, axis=-1, keepdims=True)
    return x * lax.rsqrt(ms + EPS) * g


def _rope_chunk(x, cos, sin_signed, first_half):
    swapped = jnp.where(first_half, pltpu.roll(x, 96, 1), pltpu.roll(x, 32, 1))
    return x * cos + swapped * sin_signed


def _col_chunks(xn_ref, w_ref):
    xn = xn_ref[...]
    for c in range(w_ref.shape[1] // MXU_COLS):
        cols = slice(c * MXU_COLS, (c + 1) * MXU_COLS)
        yield cols, jnp.dot(xn, w_ref[:, cols], preferred_element_type=F32)


def _rope(a, cos_ref, sin_ref):
    cos = cos_ref[...]
    sin = sin_ref[...]
    lane = lax.broadcasted_iota(jnp.int32, cos.shape, 1)
    first_half = (lane % HEAD_DIM) < (HEAD_DIM // 2)
    parts = [_rope_chunk(a[:, c * LANES:(c + 1) * LANES], cos, sin, first_half)
             for c in range(a.shape[1] // LANES)]
    return jnp.concatenate(parts, axis=1)


def _in_proj_prompt_kernel(x_ref, g_ref, w_ref, cos_ref, sin_ref,
                           qt_ref, kb_ref, kt_ref, vt_ref, vf_ref, u_ref, xn_ref, *, q_scale):
    width = qt_ref.shape[0]
    xn_ref[...] = _rms(x_ref[...], g_ref[...]).astype(BF16)
    for cols, acc in _col_chunks(xn_ref, w_ref):
        part, local = cols.start // width, slice(cols.start % width, cols.start % width + MXU_COLS)
        if part == 0:
            qt_ref[local, :] = (_rope(acc, cos_ref, sin_ref) * q_scale).T.astype(BF16)
        elif part == 1:
            k = _rope(acc, cos_ref, sin_ref)
            kb_ref[:, local] = k.astype(BF16)
            kt_ref[local, :] = k.T
        elif part == 2:
            vf_ref[:, local] = acc
            vt_ref[local, :] = acc.T.astype(BF16)
        else:
            u_ref[:, local] = acc


def _in_proj_prompt(x, g_mix, w_in_bf, cos_t, sin_t, batch, seq, tm, q_scale):
    m, d = x.shape
    width = ATTN_WIDTH
    nt = seq // tm
    row = lambda i: (i, 0)
    tile = lambda i: (i, 0, 0)
    tab = lambda i: (i % nt, 0)
    outs = [jax.ShapeDtypeStruct((m // tm, width, tm), BF16),
            jax.ShapeDtypeStruct((m, width), BF16),
            jax.ShapeDtypeStruct((batch, width, seq), F32),
            jax.ShapeDtypeStruct((m // tm, width, tm), BF16),
            jax.ShapeDtypeStruct((m, width), F32),
            jax.ShapeDtypeStruct((m, width), F32)]
    return pl.pallas_call(
        functools.partial(_in_proj_prompt_kernel, q_scale=q_scale),
        out_shape=outs,
        grid=(m // tm,),
        in_specs=[pl.BlockSpec((tm, d), row),
                  pl.BlockSpec((1, d), lambda i: (0, 0)),
                  pl.BlockSpec(w_in_bf.shape, lambda i: (0, 0), pipeline_mode=pl.Buffered(1)),
                  pl.BlockSpec((tm, LANES), tab),
                  pl.BlockSpec((tm, LANES), tab)],
        out_specs=[pl.BlockSpec((None, width, tm), tile),
                   pl.BlockSpec((tm, width), row),
                   pl.BlockSpec((None, width, tm), lambda i: (i // nt, 0, i % nt)),
                   pl.BlockSpec((None, width, tm), tile),
                   pl.BlockSpec((tm, width), row),
                   pl.BlockSpec((tm, width), row)],
        scratch_shapes=[pltpu.VMEM((tm, d), BF16)],
        compiler_params=_cparams(("arbitrary",)),
        name="in_proj_prompt",
    )(x, g_mix, w_in_bf, cos_t, sin_t)


def _in_proj_sample_kernel(x_ref, g_ref, w_ref, cos_ref, sin_ref, q_ref, k_ref, v_ref, u_ref, xn_ref, *, q_scale):
    n = pl.program_id(1)

    @pl.when(n == 0)
    def _():
        xn_ref[...] = _rms(x_ref[...], g_ref[...]).astype(BF16)

    acc = jnp.dot(xn_ref[...], w_ref[...], preferred_element_type=F32)

    @pl.when(n == 0)
    def _():
        q_ref[...] = (_rope(acc, cos_ref, sin_ref) * q_scale).astype(BF16)

    @pl.when(n == 1)
    def _():
        k_ref[...] = _rope(acc, cos_ref, sin_ref)

    @pl.when(n == 2)
    def _():
        v_ref[...] = acc

    @pl.when(n == 3)
    def _():
        u_ref[...] = acc


def _in_proj_sample(x, g_mix, w_in_bf, cos_t, sin_t, q_scale):
    m, d = x.shape
    width = ATTN_WIDTH
    whole = lambda i, n: (0, 0)
    outs = [jax.ShapeDtypeStruct((m, width), BF16)] + [jax.ShapeDtypeStruct((m, width), F32)] * 3
    return pl.pallas_call(
        functools.partial(_in_proj_sample_kernel, q_scale=q_scale),
        out_shape=outs,
        grid=(1, 4),
        in_specs=[pl.BlockSpec((m, d), whole),
                  pl.BlockSpec((1, d), whole),
                  pl.BlockSpec((d, width), lambda i, n: (0, n)),
                  pl.BlockSpec((m, LANES), whole),
                  pl.BlockSpec((m, LANES), whole)],
        out_specs=[pl.BlockSpec((m, width), whole)] * 4,
        scratch_shapes=[pltpu.VMEM((m, d), BF16)],
        compiler_params=_cparams(("arbitrary", "arbitrary")),
        name="in_proj_sample",
    )(x, g_mix, w_in_bf, cos_t, sin_t)


def _lambda_value(lq1, lk1, lq2, lk2, lam_init):
    a = jnp.sum(lq1 * lk1, axis=-1, keepdims=True)
    b = jnp.sum(lq2 * lk2, axis=-1, keepdims=True)
    return jnp.exp(a) - jnp.exp(b) + lam_init


def _sub_norm(o, g_sub, lam_init):
    return _rms(o, g_sub) * (1.0 - lam_init)


def _prompt_attn_kernel(pt_ref, lq1_ref, lk1_ref, lq2_ref, lk2_ref, gs_ref, qt_ref, k_ref, vt_ref,
                        qrows_ref, kn_ref, *rest, n_key_pages, tq, lam_init):
    key_pages = rest[:n_key_pages]
    o_ref, wsel_ref, wnew_ref, m0_ref, l0_ref, a0_ref, m1_ref, l1_ref, a1_ref = rest[n_key_pages:]
    qi = pl.program_id(2)
    hq = tq // 2
    qt = qt_ref[...]
    drow = lax.broadcasted_iota(jnp.int32, qt.shape, 0)
    zero = jnp.zeros_like(qt)
    qts = (jnp.where(drow < HEAD_DIM, qt, zero), jnp.where(drow >= HEAD_DIM, qt, zero))
    stats = ((m0_ref, l0_ref, a0_ref), (m1_ref, l1_ref, a1_ref))

    for m_ref, l_ref, a_ref in stats:
        m_ref[...] = jnp.full(m_ref.shape, -jnp.inf, F32)
        l_ref[...] = jnp.zeros(l_ref.shape, F32)
        a_ref[...] = jnp.zeros(a_ref.shape, F32)

    def run_chains(chains):
        scores = [jnp.dot(kj, qc[:, lanes], preferred_element_type=F32)
                  for kj, _, qc, _, lanes, _ in chains]
        probs = []
        for st, (_, _, _, (m_ref, l_ref, _), lanes, keep) in zip(scores, chains):
            if keep is not None:
                st = jnp.where(keep, st, -jnp.inf)
            m_old = m_ref[:, lanes]
            m_new = jnp.maximum(m_old, jnp.max(st, axis=0, keepdims=True))
            pt = jnp.exp2(st - m_new)
            alpha = jnp.exp2(m_old - m_new)
            l_ref[:, lanes] = alpha * l_ref[:, lanes] + jnp.sum(pt, axis=0, keepdims=True)
            m_ref[:, lanes] = m_new
            probs.append((pt.astype(BF16), alpha))
        for (pt, alpha), (_, vtj, _, (_, _, a_ref), lanes, _) in zip(probs, chains):
            a_ref[:, lanes] = alpha * a_ref[:, lanes] + jnp.dot(vtj, pt, preferred_element_type=F32)

    def body(j, carry):
        kj = k_ref[pl.ds(pl.multiple_of(j * tq, tq), tq), :]
        vtj = vt_ref[j]
        run_chains([(kj, vtj, qc, refs, slice(half * hq, (half + 1) * hq), None)
                    for qc, refs in zip(qts, stats) for half in range(2)])
        return carry

    lax.fori_loop(0, qi, body, 0)

    start = pl.multiple_of(qi * tq, tq)
    k_lo = k_ref[pl.ds(start, hq), :]
    k_all = k_ref[pl.ds(start, tq), :]
    vt_all = vt_ref[qi]
    tri = (lax.broadcasted_iota(jnp.int32, (hq, hq), 0) <= lax.broadcasted_iota(jnp.int32, (hq, hq), 1))
    late = (lax.broadcasted_iota(jnp.int32, (tq, hq), 0) <= lax.broadcasted_iota(jnp.int32, (tq, hq), 1) + hq)
    diagonal = []
    for qc, refs in zip(qts, stats):
        diagonal.append((k_lo, vt_all[:, :hq], qc, refs, slice(0, hq), tri))
        diagonal.append((k_all, vt_all, qc, refs, slice(hq, tq), late))
    run_chains(diagonal)

    lam = _lambda_value(lq1_ref[...], lk1_ref[...], lq2_ref[...], lk2_ref[...], lam_init)
    wsel, w_new = _sequence_weights(qrows_ref[0], kn_ref[0], key_pages, lam)
    wsel_ref[...] = wsel
    wnew_ref[0] = jnp.broadcast_to(w_new, wnew_ref.shape[1:])

    ot = a0_ref[...] * (1.0 / l0_ref[...]) - a1_ref[...] * (lam / l1_ref[...])
    ms = jnp.mean(ot * ot, axis=0, keepdims=True)
    ot = ot * lax.rsqrt(ms + EPS) * gs_ref[...] * (1.0 - lam_init)
    o_ref[...] = ot.T.astype(o_ref.dtype)


def _prompt_attention(qt, k, vt, lams, g_sub_col, batch, seq, tq, lam_init,
                      page_table, qrows, k_new, cache_kt, li):
    nq = seq // tq
    db, n_pages = page_table.shape
    page = cache_kt.shape[4]
    assert batch * N_HEADS * nq >= db, "one sample sequence rides on each grid step"
    seq_of = lambda b, h, i: jnp.minimum((b * N_HEADS + h) * nq + i, db - 1)
    small = lambda shape: pl.BlockSpec(shape, lambda b, h, i, pt: (0, 0))
    per_seq = lambda shape: pl.BlockSpec(shape, lambda b, h, i, pt: (seq_of(b, h, i), 0, 0))

    def key_page(s):
        return pl.BlockSpec((None, None, N_QK_HEADS, HEAD_DIM, page),
                            lambda b, h, i, pt: (li, pt[seq_of(b, h, i), s], 0, 0, 0))

    grid_spec = pltpu.PrefetchScalarGridSpec(
        num_scalar_prefetch=1,
        grid=(batch, N_HEADS, nq),
        in_specs=[small((1, HEAD_DIM))] * 4 + [small((V_DIM, 1))] + [
            pl.BlockSpec((None, V_DIM, tq), lambda b, h, i, pt: (b * nq + i, h, 0)),
            pl.BlockSpec((seq, V_DIM), lambda b, h, i, pt: (b, h)),
            pl.BlockSpec((nq, V_DIM, tq), lambda b, h, i, pt: (b, h, 0)),
            per_seq((1, N_QK_HEADS, ATTN_WIDTH)),
            per_seq((1, 1, ATTN_WIDTH))] + [key_page(s) for s in range(n_pages)],
        out_specs=[pl.BlockSpec((tq, V_DIM), lambda b, h, i, pt: (b * nq + i, h)),
                   pl.BlockSpec((n_pages * N_HEADS, page * N_HEADS), lambda b, h, i, pt: (seq_of(b, h, i), 0)),
                   per_seq((1, N_HEADS, V_DIM))],
        scratch_shapes=[pltpu.VMEM((1, tq), F32), pltpu.VMEM((1, tq), F32), pltpu.VMEM((V_DIM, tq), F32)] * 2,
    )
    kernel = functools.partial(_prompt_attn_kernel, n_key_pages=n_pages, tq=tq, lam_init=lam_init)
    return pl.pallas_call(
        kernel,
        out_shape=[jax.ShapeDtypeStruct(k.shape, BF16),
                   jax.ShapeDtypeStruct((db * n_pages * N_HEADS, page * N_HEADS), F32),
                   jax.ShapeDtypeStruct((db, N_HEADS, V_DIM), F32)],
        grid_spec=grid_spec,
        compiler_params=_cparams(("arbitrary", "arbitrary", "arbitrary")),
        name="prompt_attention",
    )(page_table, *lams, g_sub_col, qt, k, vt, qrows, k_new, *([cache_kt] * n_pages))


def _sequence_weights(qrows, k_new, k_refs, lam):
    n_pages = len(k_refs)
    _, _, page = k_refs[0].shape
    s = jnp.concatenate([jnp.dot(qrows, k_ref[...].reshape(N_QK_HEADS * HEAD_DIM, page),
                                 preferred_element_type=F32) for k_ref in k_refs], axis=1)
    s_new = jnp.sum(qrows * k_new, axis=-1, keepdims=True)
    m = jnp.maximum(jnp.max(s, axis=-1, keepdims=True), s_new)
    p = jnp.exp(s - m)
    p_new = jnp.exp(s_new - m)
    l = jnp.sum(p, axis=-1, keepdims=True) + p_new
    a = p / l
    a_new = p_new / l
    w = a[:N_HEADS] - lam * a[N_HEADS:]
    w_new = a_new[:N_HEADS] - lam * a_new[N_HEADS:]
    w_rows = jnp.concatenate([w[:, i * page:(i + 1) * page] for i in range(n_pages)], axis=0)
    tok = lax.broadcasted_iota(jnp.int32, (page, page * N_HEADS), 0)
    dst = lax.broadcasted_iota(jnp.int32, (page, page * N_HEADS), 1)
    spread = (dst // N_HEADS == tok).astype(F32)
    wide = jnp.dot(w_rows, spread, preferred_element_type=F32)
    row = lax.broadcasted_iota(jnp.int32, wide.shape, 0)
    col = lax.broadcasted_iota(jnp.int32, wide.shape, 1)
    return jnp.where(row % N_HEADS == col % N_HEADS, wide, 0.0), w_new


def _paged_values_step(step, wsel_refs, v_refs, oacc_ref, *, pages_per_seq, n_valid_pages):
    per_step = len(v_refs)
    page = v_refs[0].shape[0]

    @pl.when(step == 0)
    def _():
        oacc_ref[...] = jnp.zeros(oacc_ref.shape, F32)

    for i, (wsel_ref, v_ref) in enumerate(zip(wsel_refs, v_refs)):
        flat = step * per_step + i
        seq = jnp.minimum(flat, n_valid_pages - 1) // pages_per_seq
        part = jnp.dot(wsel_ref[...], v_ref[...].reshape(page * N_HEADS, V_DIM), preferred_element_type=F32)
        oacc_ref[seq] += jnp.where(flat < n_valid_pages, part, 0.0)


def _sample_finish_kernel(gs_ref, oacc_ref, wnew_ref, vn_ref, o_ref, *, lam_init):
    full = oacc_ref[...] + wnew_ref[...] * vn_ref[...]
    o_ref[...] = _sub_norm(full, gs_ref[...], lam_init).astype(o_ref.dtype)


def _sample_finish(g_sub, oacc, w_new, v_new, lam_init):
    db = oacc.shape[0]
    whole = pl.BlockSpec((db, N_HEADS, V_DIM), lambda i: (0, 0, 0))
    return pl.pallas_call(
        functools.partial(_sample_finish_kernel, lam_init=lam_init),
        out_shape=jax.ShapeDtypeStruct((db, N_HEADS, V_DIM), BF16),
        grid=(1,),
        in_specs=[pl.BlockSpec((1, V_DIM), lambda i: (0, 0)), whole, whole, whole],
        out_specs=whole,
        compiler_params=_cparams(("arbitrary",)),
        name="sample_finish",
    )(g_sub, oacc, w_new, v_new)


def _pool_matmul(pooled_groups, wp_ref, scale):
    outs = [jnp.dot(pg.astype(BF16), wp_ref[g], preferred_element_type=F32)
            for g, pg in enumerate(pooled_groups)]
    return jnp.concatenate(outs, axis=1) * scale


def _pool_prompt_kernel(u_ref, halo_ref, wp_ref, sc_ref, o_ref, ext_ref, *, ts):
    i = pl.program_id(1)
    group = wp_ref.shape[1]
    u = u_ref[...]
    ext_ref[pl.ds(POOL_HALO, ts), :] = u
    ext_ref[pl.ds(0, POOL_HALO), :] = jnp.where(i > 0, halo_ref[...], 0.0)
    pos1 = (i * ts + lax.broadcasted_iota(jnp.int32, (ts, 1), 0) + 1).astype(F32)
    pooled = []
    for g, win in enumerate(POOL_WINDOWS):
        cols = slice(g * group, (g + 1) * group)
        wsum = u[:, cols]
        for k in range(1, win):
            wsum = wsum + ext_ref[pl.ds(POOL_HALO - k, ts), cols]
        cnt = jnp.minimum(pos1, float(win))
        pooled.append(wsum / cnt - u[:, cols])
    o_ref[...] = _pool_matmul(pooled, wp_ref, sc_ref[...]).astype(o_ref.dtype)


def _pool_prompt(u, w_pool_bf, pool_scale, batch, seq, ts):
    m, width = u.shape
    nt = seq // ts
    r = ts // POOL_HALO
    return pl.pallas_call(
        functools.partial(_pool_prompt_kernel, ts=ts),
        out_shape=jax.ShapeDtypeStruct((m, width), BF16),
        grid=(batch, nt),
        in_specs=[pl.BlockSpec((ts, width), lambda b, i: (b * nt + i, 0)),
                  pl.BlockSpec((POOL_HALO, width), lambda b, i: (jnp.maximum((b * nt + i) * r - 1, 0), 0)),
                  pl.BlockSpec(w_pool_bf.shape, lambda b, i: (0, 0, 0)),
                  pl.BlockSpec((1, width), lambda b, i: (0, 0))],
        out_specs=pl.BlockSpec((ts, width), lambda b, i: (b * nt + i, 0)),
        scratch_shapes=[pltpu.VMEM((ts + POOL_HALO, width), F32)],
        compiler_params=_cparams(("arbitrary", "arbitrary")),
        name="pool_prompt",
    )(u, u, w_pool_bf, pool_scale)


def _pool_sample_kernel(u_ref, st_ref, wp_ref, sc_ref, o_ref):
    group = wp_ref.shape[1]
    u = u_ref[...]
    pooled = []
    for g, win in enumerate(POOL_WINDOWS):
        cols = slice(g * group, (g + 1) * group)
        wsum = u[:, cols]
        for k in range(1, win):
            wsum = wsum + st_ref[POOL_STATE - k, :, cols]
        pooled.append(wsum / float(win) - u[:, cols])
    o_ref[...] = _pool_matmul(pooled, wp_ref, sc_ref[...]).astype(o_ref.dtype)


def _pool_sample(u, state_t, w_pool_bf, pool_scale, tb):
    m, width = u.shape
    return pl.pallas_call(
        _pool_sample_kernel,
        out_shape=jax.ShapeDtypeStruct((m, width), BF16),
        grid=(m // tb,),
        in_specs=[pl.BlockSpec((tb, width), lambda i: (i, 0)),
                  pl.BlockSpec((POOL_STATE, tb, width), lambda i: (0, i, 0)),
                  pl.BlockSpec(w_pool_bf.shape, lambda i: (0, 0, 0)),
                  pl.BlockSpec((1, width), lambda i: (0, 0))],
        out_specs=pl.BlockSpec((tb, width), lambda i: (i, 0)),
        compiler_params=_cparams(("arbitrary",)),
        name="pool_sample",
    )(u, state_t, w_pool_bf, pool_scale)


def _out_proj_kernel(o_ref, p_ref, w_ref, h_ref, out_ref):
    half = o_ref.shape[1]
    acc = jnp.dot(o_ref[...], w_ref[pl.ds(0, half), :], preferred_element_type=F32)
    acc += jnp.dot(p_ref[...], w_ref[pl.ds(half, half), :], preferred_element_type=F32)
    out_ref[...] = h_ref[...] + acc


def _out_proj(o, pool, w_out_bf, h, tm):
    m, d = h.shape
    row = lambda i: (i, 0)
    return pl.pallas_call(
        _out_proj_kernel,
        out_shape=jax.ShapeDtypeStruct((m, d), F32),
        grid=(m // tm,),
        in_specs=[pl.BlockSpec((tm, o.shape[1]), row),
                  pl.BlockSpec((tm, pool.shape[1]), row),
                  pl.BlockSpec(w_out_bf.shape, lambda i: (0, 0)),
                  pl.BlockSpec((tm, d), row)],
        out_specs=pl.BlockSpec((tm, d), row),
        compiler_params=_cparams(("arbitrary",)),
        name="out_proj",
    )(o, pool, w_out_bf, h)


def _ffn_body(h_ref, g_ref, wg_ref, wu_ref, wd_ref, out_ref, xn_ref, acc_ref):
    f = pl.program_id(1)

    @pl.when(f == 0)
    def _():
        xn_ref[...] = _rms(h_ref[...], g_ref[...]).astype(BF16)
        acc_ref[...] = jnp.zeros(acc_ref.shape, F32)

    xn = xn_ref[...]
    gate = jnp.dot(xn, wg_ref[...], preferred_element_type=F32)
    up = jnp.dot(xn, wu_ref[...], preferred_element_type=F32)
    act = (gate * jax.nn.sigmoid(gate) * up).astype(BF16)
    acc_ref[...] += jnp.dot(act, wd_ref[...], preferred_element_type=F32)

    @pl.when(f == pl.num_programs(1) - 1)
    def _():
        out_ref[...] = h_ref[...] + acc_ref[...]


def _ffn_kernel(h_ref, g_ref, wg_ref, wu_ref, wd_ref, out_ref, xn_ref, acc_ref):
    _ffn_body(h_ref, g_ref, wg_ref, wu_ref, wd_ref, out_ref, xn_ref, acc_ref)


def _ffn_paged_kernel(page_ids_ref, h_ref, g_ref, wg_ref, wu_ref, wd_ref, *rest,
                      per_step, pages_per_seq, n_valid_pages):
    wsel_refs = rest[:per_step]
    v_refs = rest[per_step:2 * per_step]
    out_ref, oacc_ref, xn_ref, acc_ref = rest[2 * per_step:]
    step = pl.program_id(0) * pl.num_programs(1) + pl.program_id(1)
    _paged_values_step(step, wsel_refs, v_refs, oacc_ref, pages_per_seq=pages_per_seq, n_valid_pages=n_valid_pages)
    _ffn_body(h_ref, g_ref, wg_ref, wu_ref, wd_ref, out_ref, xn_ref, acc_ref)


def _ffn_specs(d, tm, tf, nf, index):
    return [pl.BlockSpec((tm, d), index(lambda i, f: (i, 0))),
            pl.BlockSpec((1, d), index(lambda i, f: (0, 0))),
            pl.BlockSpec((d, tf), index(lambda i, f: (0, f))),
            pl.BlockSpec((d, tf), index(lambda i, f: (0, nf + f))),
            pl.BlockSpec((tf, d), index(lambda i, f: (f, 0)))]


def _ffn(h, g_ffn, w_gate_up_bf, w_down_bf, tm, tf):
    m, d = h.shape
    nf = w_down_bf.shape[0] // tf
    plain = lambda fn: fn
    return pl.pallas_call(
        _ffn_kernel,
        out_shape=jax.ShapeDtypeStruct((m, d), F32),
        grid=(m // tm, nf),
        in_specs=_ffn_specs(d, tm, tf, nf, plain),
        out_specs=pl.BlockSpec((tm, d), lambda i, f: (i, 0)),
        scratch_shapes=[pltpu.VMEM((tm, d), BF16), pltpu.VMEM((tm, d), F32)],
        compiler_params=_cparams(("arbitrary", "arbitrary")),
        name="ffn",
    )(h, g_ffn, w_gate_up_bf, w_gate_up_bf, w_down_bf)


def _ffn_paged(h, g_ffn, w_gate_up_bf, w_down_bf, tm, tf, page_table, wsel, cache_v, li):
    m, d = h.shape
    nf = w_down_bf.shape[0] // tf
    n_steps = (m // tm) * nf
    db, pages_per_seq = page_table.shape
    page = cache_v.shape[2]
    n_valid = db * pages_per_seq
    per_step = -(-n_valid // n_steps)
    page_ids = page_table.reshape(-1)
    wsel = wsel.reshape(n_valid, N_HEADS, page * N_HEADS)
    with_ids = lambda fn: (lambda i, f, ids: fn(i, f))
    slot_page = lambda i, f, s: jnp.minimum((i * nf + f) * per_step + s, n_valid - 1)

    def w_spec(s):
        return pl.BlockSpec((None, N_HEADS, page * N_HEADS), lambda i, f, ids: (slot_page(i, f, s), 0, 0))

    def v_spec(s):
        return pl.BlockSpec((None, None, page, N_HEADS, V_DIM),
                            lambda i, f, ids: (li, ids[slot_page(i, f, s)], 0, 0, 0))

    grid_spec = pltpu.PrefetchScalarGridSpec(
        num_scalar_prefetch=1,
        grid=(m // tm, nf),
        in_specs=_ffn_specs(d, tm, tf, nf, with_ids)
                 + [w_spec(s) for s in range(per_step)]
                 + [v_spec(s) for s in range(per_step)],
        out_specs=[pl.BlockSpec((tm, d), lambda i, f, ids: (i, 0)),
                   pl.BlockSpec((db, N_HEADS, V_DIM), lambda i, f, ids: (0, 0, 0))],
        scratch_shapes=[pltpu.VMEM((tm, d), BF16), pltpu.VMEM((tm, d), F32)],
    )
    return pl.pallas_call(
        functools.partial(_ffn_paged_kernel, per_step=per_step, pages_per_seq=pages_per_seq,
                          n_valid_pages=n_valid),
        out_shape=[jax.ShapeDtypeStruct((m, d), F32), jax.ShapeDtypeStruct((db, N_HEADS, V_DIM), F32)],
        grid_spec=grid_spec,
        compiler_params=_cparams(("arbitrary", "arbitrary")),
        name="ffn_paged",
    )(page_ids, h, g_ffn, w_gate_up_bf, w_gate_up_bf, w_down_bf, *([wsel] * per_step), *([cache_v] * per_step))


def _ple_kernel(h_ref, p_ref, gp_ref, wg_ref, wp_ref, gf_ref, y_ref, *, final_norm):
    h = h_ref[...]
    xn = _rms(h, gp_ref[...]).astype(BF16)
    gate = jax.nn.sigmoid(jnp.dot(xn, wg_ref[...], preferred_element_type=F32))
    proj = jnp.dot(p_ref[...].astype(BF16), wp_ref[...], preferred_element_type=F32)
    y = h + gate * proj
    y_ref[...] = _rms(y, gf_ref[...]) if final_norm else y


def _ple(h, p, g_ple, w_gate_bf, w_proj_bf, g_final, tm, final_norm):
    m, d = h.shape
    row = lambda i: (i, 0)
    const = lambda i: (0, 0)
    return pl.pallas_call(
        functools.partial(_ple_kernel, final_norm=final_norm),
        out_shape=jax.ShapeDtypeStruct((m, d), F32),
        grid=(m // tm,),
        in_specs=[pl.BlockSpec((tm, d), row),
                  pl.BlockSpec((tm, p.shape[1]), row),
                  pl.BlockSpec((1, d), const),
                  pl.BlockSpec(w_gate_bf.shape, const),
                  pl.BlockSpec(w_proj_bf.shape, const),
                  pl.BlockSpec((1, d), const)],
        out_specs=pl.BlockSpec((tm, d), row),
        compiler_params=_cparams(("arbitrary",)),
        name="ple_final",
    )(h, p, g_ple, w_gate_bf, w_proj_bf, g_final)


def _rope_tables(positions):
    half = HEAD_DIM // 2
    inv = ROPE_THETA ** (-jnp.arange(half, dtype=F32) / half)
    ang = positions.astype(F32)[:, None] * inv[None, :]
    cos, sin = jnp.cos(ang), jnp.sin(ang)
    reps = LANES // HEAD_DIM
    return jnp.tile(jnp.concatenate([cos, cos], axis=1), (1, reps)), \
        jnp.tile(jnp.concatenate([-sin, sin], axis=1), (1, reps))


def _row_tile(m, want):
    return want if m % want == 0 else m


def kernel(x_prompt, x_sample, p_prompt, p_sample, cache_k, cache_v, state_pool, page_table, g_mix, w_in, w_out, lambda_q1, lambda_k1, lambda_q2, lambda_k2, g_sub, w_pool, pool_scale, g_ffn, w_gate_up, w_down, g_ple, w_ple_gate, w_ple_proj, g_final):
    batch, seq, d = x_prompt.shape
    db, ds, _ = x_sample.shape
    depth = g_mix.shape[0]
    n_pages = page_table.shape[1]
    page = cache_k.shape[2]
    past = n_pages * page
    assert ds == 1 and d == 2 * ATTN_WIDTH

    hp = x_prompt.reshape(batch * seq, d)
    hs = x_sample.reshape(db * ds, d)
    cos_p, sin_p = _rope_tables(jnp.arange(seq))
    cos_s, sin_s = _rope_tables(jnp.full((db,), past))
    row2 = lambda a: a.reshape(1, -1)

    tm_p = _row_tile(batch * seq, 512)
    tq = _row_tile(seq, 512)
    ts = _row_tile(seq, 512)
    cache_kt = jnp.transpose(cache_k, (0, 1, 3, 4, 2))
    tf = 512
    qk_scale = HEAD_DIM ** -0.5

    outs = {k: [] for k in ("kp", "vp", "pp", "ks", "vs", "ps")}
    for li in range(depth):
        lam_init = 0.8 - 0.6 * math.exp(-0.3 * li)
        w_in_bf = w_in[li].astype(BF16)
        w_out_bf = w_out[li].astype(BF16)
        w_pool_bf = w_pool[li].astype(BF16)
        w_gu_bf = w_gate_up[li].astype(BF16)
        w_down_bf = w_down[li].astype(BF16)
        w_pg_bf = w_ple_gate[li].astype(BF16)
        w_pp_bf = w_ple_proj[li].astype(BF16)
        lams = tuple(row2(a[li]) for a in (lambda_q1, lambda_k1, lambda_q2, lambda_k2))
        gsub = row2(g_sub[li])

        qs, kfs, vfs, us = _in_proj_sample(hs, row2(g_mix[li]), w_in_bf, cos_s, sin_s, qk_scale)
        col_sub = jnp.arange(ATTN_WIDTH) // HEAD_DIM
        row_sub = (jnp.arange(N_QK_HEADS) % N_HEADS) * 2 + jnp.arange(N_QK_HEADS) // N_HEADS
        qrows = qs.astype(F32)[:, None, :] * (col_sub[None, :] == row_sub[:, None]).astype(F32)[None]
        qt, kb, kt, vt, vf, u = _in_proj_prompt(hp, row2(g_mix[li]), w_in_bf, cos_p, sin_p, batch, seq, tq,
                                                qk_scale * math.log2(math.e))
        o, wsel, w_new = _prompt_attention(qt, kb, vt, lams, g_sub[li].reshape(-1, 1), batch, seq, tq, lam_init,
                                           page_table, qrows, kfs.reshape(db, 1, ATTN_WIDTH), cache_kt, li)
        pool = _pool_prompt(u, w_pool_bf, row2(pool_scale[li]), batch, seq, ts)
        h1 = _out_proj(o, pool, w_out_bf, hp, tm_p)
        h2, oacc = _ffn_paged(h1, row2(g_ffn[li]), w_gu_bf, w_down_bf, tm_p, tf, page_table, wsel, cache_v, li)
        outs["kp"].append(jnp.transpose(kt.reshape(batch, N_QK_HEADS, HEAD_DIM, seq), (0, 3, 1, 2)))
        outs["vp"].append(vf.reshape(batch, seq, N_HEADS, V_DIM))
        outs["pp"].append(u.reshape(batch, seq, -1)[:, seq - POOL_STATE:])
        p_rows = p_prompt[li].reshape(batch * seq, -1)
        last = li == depth - 1
        gfin = row2(g_final)
        hp_next = _ple(h2, p_rows, row2(g_ple[li]), w_pg_bf, w_pp_bf, gfin, tm_p, last)

        o_s = _sample_finish(gsub, oacc, w_new, vfs.reshape(db, N_HEADS, V_DIM), lam_init).reshape(db, -1)
        state = state_pool[li]
        pool_s = _pool_sample(us, jnp.swapaxes(state, 0, 1), w_pool_bf, row2(pool_scale[li]), db)
        h1s = _out_proj(o_s, pool_s, w_out_bf, hs, db)
        h2s = _ffn(h1s, row2(g_ffn[li]), w_gu_bf, w_down_bf, db, tf)
        hs_next = _ple(h2s, p_sample[li].reshape(db * ds, -1), row2(g_ple[li]), w_pg_bf, w_pp_bf, gfin, db, last)
        outs["ks"].append(kfs.reshape(db, ds, N_QK_HEADS, HEAD_DIM))
        outs["vs"].append(vfs.reshape(db, ds, N_HEADS, V_DIM))
        outs["ps"].append(jnp.concatenate([state[:, 1:], us[:, None, :]], axis=1))
        hp, hs = hp_next, hs_next

    y_prompt = hp.reshape(batch, seq, d)
    y_sample = hs.reshape(db, ds, d)
    return (y_prompt, y_sample, jnp.stack(outs["kp"]), jnp.stack(outs["vp"]), jnp.stack(outs["pp"]),
            jnp.stack(outs["ks"]), jnp.stack(outs["vs"]), jnp.stack(outs["ps"]))
```

```python
import functools
import math

import jax
import jax.numpy as jnp
from jax import lax
from jax.experimental import pallas as pl
from jax.experimental.pallas import tpu as pltpu

F32 = jnp.float32
BF16 = jnp.bfloat16

N_HEADS = 8
HEAD_DIM = 64
V_DIM = 2 * HEAD_DIM
N_QK_HEADS = 2 * N_HEADS
ATTN_WIDTH = N_HEADS * V_DIM
POOL_WINDOWS = (2, 4, 8, 16)
POOL_STATE = max(POOL_WINDOWS) - 1
POOL_HALO = 16
ROPE_THETA = 10000.0
EPS = 1e-6
LANES = 128
MXU_COLS = 256
VMEM_LIMIT = 56 * 1024 * 1024


def _cparams(sem):
    return pltpu.CompilerParams(dimension_semantics=sem, vmem_limit_bytes=VMEM_LIMIT)


def _rms(x, g):
    ms = jnp.mean(x * x, axis=-1, keepdims=True)
    return x * lax.rsqrt(ms + EPS) * g


def _rope_chunk(x, cos, sin_signed, first_half):
    swapped = jnp.where(first_half, pltpu.roll(x, 96, 1), pltpu.roll(x, 32, 1))
    return x * cos + swapped * sin_signed


def _col_chunks(xn_ref, w_ref):
    xn = xn_ref[...]
    for c in range(w_ref.shape[1] // MXU_COLS):
        cols = slice(c * MXU_COLS, (c + 1) * MXU_COLS)
        yield cols, jnp.dot(xn, w_ref[:, cols], preferred_element_type=F32)


def _rope(a, cos_ref, sin_ref):
    cos = cos_ref[...]
    sin = sin_ref[...]
    lane = lax.broadcasted_iota(jnp.int32, cos.shape, 1)
    first_half = (lane % HEAD_DIM) < (HEAD_DIM // 2)
    parts = [_rope_chunk(a[:, c * LANES:(c + 1) * LANES], cos, sin, first_half)
             for c in range(a.shape[1] // LANES)]
    return jnp.concatenate(parts, axis=1)


def _in_proj_prompt_kernel(x_ref, g_ref, w_ref, cos_ref, sin_ref,
                           qt_ref, kb_ref, kt_ref, vt_ref, vf_ref, u_ref, xn_ref, *, q_scale):
    width = qt_ref.shape[0]
    xn_ref[...] = _rms(x_ref[...], g_ref[...]).astype(BF16)
    for cols, acc in _col_chunks(xn_ref, w_ref):
        part, local = cols.start // width, slice(cols.start % width, cols.start % width + MXU_COLS)
        if part == 0:
            qt_ref[local, :] = (_rope(acc, cos_ref, sin_ref) * q_scale).T.astype(BF16)
        elif part == 1:
            k = _rope(acc, cos_ref, sin_ref)
            kb_ref[:, local] = k.astype(BF16)
            kt_ref[local, :] = k.T
        elif part == 2:
            vf_ref[:, local] = acc
            vt_ref[local, :] = acc.T.astype(BF16)
        else:
            u_ref[:, local] = acc


def _in_proj_prompt(x, g_mix, w_in_bf, cos_t, sin_t, batch, seq, tm, q_scale):
    m, d = x.shape
    width = ATTN_WIDTH
    nt = seq // tm
    row = lambda i: (i, 0)
    tile = lambda i: (i, 0, 0)
    tab = lambda i: (i % nt, 0)
    outs = [jax.ShapeDtypeStruct((m // tm, width, tm), BF16),
            jax.ShapeDtypeStruct((m, width), BF16),
            jax.ShapeDtypeStruct((batch, width, seq), F32),
            jax.ShapeDtypeStruct((m // tm, width, tm), BF16),
            jax.ShapeDtypeStruct((m, width), F32),
            jax.ShapeDtypeStruct((m, width), F32)]
    return pl.pallas_call(
        functools.partial(_in_proj_prompt_kernel, q_scale=q_scale),
        out_shape=outs,
        grid=(m // tm,),
        in_specs=[pl.BlockSpec((tm, d), row),
                  pl.BlockSpec((1, d), lambda i: (0, 0)),
                  pl.BlockSpec(w_in_bf.shape, lambda i: (0, 0), pipeline_mode=pl.Buffered(1)),
                  pl.BlockSpec((tm, LANES), tab),
                  pl.BlockSpec((tm, LANES), tab)],
        out_specs=[pl.BlockSpec((None, width, tm), tile),
                   pl.BlockSpec((tm, width), row),
                   pl.BlockSpec((None, width, tm), lambda i: (i // nt, 0, i % nt)),
                   pl.BlockSpec((None, width, tm), tile),
                   pl.BlockSpec((tm, width), row),
                   pl.BlockSpec((tm, width), row)],
        scratch_shapes=[pltpu.VMEM((tm, d), BF16)],
        compiler_params=_cparams(("arbitrary",)),
        name="in_proj_prompt",
    )(x, g_mix, w_in_bf, cos_t, sin_t)


def _in_proj_sample_kernel(x_ref, g_ref, w_ref, cos_ref, sin_ref, q_ref, k_ref, v_ref, u_ref, xn_ref, *, q_scale):
    n = pl.program_id(1)

    @pl.when(n == 0)
    def _():
        xn_ref[...] = _rms(x_ref[...], g_ref[...]).astype(BF16)

    acc = jnp.dot(xn_ref[...], w_ref[...], preferred_element_type=F32)

    @pl.when(n == 0)
    def _():
        q_ref[...] = (_rope(acc, cos_ref, sin_ref) * q_scale).astype(BF16)

    @pl.when(n == 1)
    def _():
        k_ref[...] = _rope(acc, cos_ref, sin_ref)

    @pl.when(n == 2)
    def _():
        v_ref[...] = acc

    @pl.when(n == 3)
    def _():
        u_ref[...] = acc


def _in_proj_sample(x, g_mix, w_in_bf, cos_t, sin_t, q_scale):
    m, d = x.shape
    width = ATTN_WIDTH
    whole = lambda i, n: (0, 0)
    outs = [jax.ShapeDtypeStruct((m, width), BF16)] + [jax.ShapeDtypeStruct((m, width), F32)] * 3
    return pl.pallas_call(
        functools.partial(_in_proj_sample_kernel, q_scale=q_scale),
        out_shape=outs,
        grid=(1, 4),
        in_specs=[pl.BlockSpec((m, d), whole),
                  pl.BlockSpec((1, d), whole),
                  pl.BlockSpec((d, width), lambda i, n: (0, n)),
                  pl.BlockSpec((m, LANES), whole),
                  pl.BlockSpec((m, LANES), whole)],
        out_specs=[pl.BlockSpec((m, width), whole)] * 4,
        scratch_shapes=[pltpu.VMEM((m, d), BF16)],
        compiler_params=_cparams(("arbitrary", "arbitrary")),
        name="in_proj_sample",
    )(x, g_mix, w_in_bf, cos_t, sin_t)


def _lambda_value(lq1, lk1, lq2, lk2, lam_init):
    a = jnp.sum(lq1 * lk1, axis=-1, keepdims=True)
    b = jnp.sum(lq2 * lk2, axis=-1, keepdims=True)
    return jnp.exp(a) - jnp.exp(b) + lam_init


def _sub_norm(o, g_sub, lam_init):
    return _rms(o, g_sub) * (1.0 - lam_init)


def _prompt_attn_kernel(pt_ref, lq1_ref, lk1_ref, lq2_ref, lk2_ref, gs_ref, qt_ref, k_ref, vt_ref,
                        qrows_ref, kn_ref, *rest, n_key_pages, n_casts, tq, lam_init):
    key_pages = rest[:n_key_pages]
    cast_in = rest[n_key_pages:n_key_pages + n_casts]
    o_ref, wsel_ref, wnew_ref = rest[n_key_pages + n_casts:n_key_pages + n_casts + 3]
    cast_out = rest[n_key_pages + n_casts + 3:n_key_pages + 2 * n_casts + 3]
    m0_ref, l0_ref, a0_ref, m1_ref, l1_ref, a1_ref = rest[n_key_pages + 2 * n_casts + 3:]
    for src, dst in zip(cast_in, cast_out):
        dst[...] = src[...].astype(dst.dtype)
    qi = pl.program_id(2)
    hq = tq // 2
    qt = qt_ref[...]
    drow = lax.broadcasted_iota(jnp.int32, qt.shape, 0)
    zero = jnp.zeros_like(qt)
    qts = (jnp.where(drow < HEAD_DIM, qt, zero), jnp.where(drow >= HEAD_DIM, qt, zero))
    stats = ((m0_ref, l0_ref, a0_ref), (m1_ref, l1_ref, a1_ref))

    for m_ref, l_ref, a_ref in stats:
        m_ref[...] = jnp.full(m_ref.shape, -jnp.inf, F32)
        l_ref[...] = jnp.zeros(l_ref.shape, F32)
        a_ref[...] = jnp.zeros(a_ref.shape, F32)

    def run_chains(chains):
        scores = [jnp.dot(kj, qc[:, lanes], preferred_element_type=F32)
                  for kj, _, qc, _, lanes, _ in chains]
        probs = []
        for st, (_, _, _, (m_ref, l_ref, _), lanes, keep) in zip(scores, chains):
            if keep is not None:
                st = jnp.where(keep, st, -jnp.inf)
            m_old = m_ref[:, lanes]
            m_new = jnp.maximum(m_old, jnp.max(st, axis=0, keepdims=True))
            pt = jnp.exp2(st - m_new)
            alpha = jnp.exp2(m_old - m_new)
            l_ref[:, lanes] = alpha * l_ref[:, lanes] + jnp.sum(pt, axis=0, keepdims=True)
            m_ref[:, lanes] = m_new
            probs.append((pt.astype(BF16), alpha))
        for (pt, alpha), (_, vtj, _, (_, _, a_ref), lanes, _) in zip(probs, chains):
            a_ref[:, lanes] = alpha * a_ref[:, lanes] + jnp.dot(vtj, pt, preferred_element_type=F32)

    def body(j, carry):
        kj = k_ref[pl.ds(pl.multiple_of(j * tq, tq), tq), :]
        vtj = vt_ref[j]
        run_chains([(kj, vtj, qc, refs, slice(half * hq, (half + 1) * hq), None)
                    for qc, refs in zip(qts, stats) for half in range(2)])
        return carry

    lax.fori_loop(0, qi, body, 0)

    start = pl.multiple_of(qi * tq, tq)
    k_lo = k_ref[pl.ds(start, hq), :]
    k_all = k_ref[pl.ds(start, tq), :]
    vt_all = vt_ref[qi]
    tri = (lax.broadcasted_iota(jnp.int32, (hq, hq), 0) <= lax.broadcasted_iota(jnp.int32, (hq, hq), 1))
    late = (lax.broadcasted_iota(jnp.int32, (tq, hq), 0) <= lax.broadcasted_iota(jnp.int32, (tq, hq), 1) + hq)
    diagonal = []
    for qc, refs in zip(qts, stats):
        diagonal.append((k_lo, vt_all[:, :hq], qc, refs, slice(0, hq), tri))
        diagonal.append((k_all, vt_all, qc, refs, slice(hq, tq), late))
    run_chains(diagonal)

    lam = _lambda_value(lq1_ref[...], lk1_ref[...], lq2_ref[...], lk2_ref[...], lam_init)
    wsel, w_new = _sequence_weights(qrows_ref[0], kn_ref[0], key_pages, lam)
    wsel_ref[...] = wsel
    wnew_ref[0] = jnp.broadcast_to(w_new, wnew_ref.shape[1:])

    ot = a0_ref[...] * (1.0 / l0_ref[...]) - a1_ref[...] * (lam / l1_ref[...])
    ms = jnp.mean(ot * ot, axis=0, keepdims=True)
    ot = ot * lax.rsqrt(ms + EPS) * gs_ref[...] * (1.0 - lam_init)
    o_ref[...] = ot.T.astype(o_ref.dtype)


def _slab_rows(rows, n_steps):
    need = -(-rows // n_steps)
    return next(r for r in range(16, rows + 1, 16) if rows % r == 0 and r >= need)


def _prompt_attention(qt, k, vt, lams, g_sub_col, batch, seq, tq, lam_init,
                      page_table, qrows, k_new, cache_kt, li, f32_weights):
    nq = seq // tq
    db, n_pages = page_table.shape
    page = cache_kt.shape[4]
    n_steps = batch * N_HEADS * nq
    assert n_steps >= db, "one sample sequence rides on each grid step"
    seq_of = lambda b, h, i: jnp.minimum((b * N_HEADS + h) * nq + i, db - 1)

    def slab(w):
        rows = _slab_rows(w.shape[0], n_steps)
        last = w.shape[0] // rows - 1
        return pl.BlockSpec((rows, w.shape[1]),
                            lambda b, h, i, pt: (jnp.minimum((b * N_HEADS + h) * nq + i, last), 0))
    small = lambda shape: pl.BlockSpec(shape, lambda b, h, i, pt: (0, 0))
    per_seq = lambda shape: pl.BlockSpec(shape, lambda b, h, i, pt: (seq_of(b, h, i), 0, 0))

    def key_page(s):
        return pl.BlockSpec((None, None, N_QK_HEADS, HEAD_DIM, page),
                            lambda b, h, i, pt: (li, pt[seq_of(b, h, i), s], 0, 0, 0))

    grid_spec = pltpu.PrefetchScalarGridSpec(
        num_scalar_prefetch=1,
        grid=(batch, N_HEADS, nq),
        in_specs=[small((1, HEAD_DIM))] * 4 + [small((V_DIM, 1))] + [
            pl.BlockSpec((None, V_DIM, tq), lambda b, h, i, pt: (b * nq + i, h, 0)),
            pl.BlockSpec((seq, V_DIM), lambda b, h, i, pt: (b, h)),
            pl.BlockSpec((nq, V_DIM, tq), lambda b, h, i, pt: (b, h, 0)),
            per_seq((1, N_QK_HEADS, ATTN_WIDTH)),
            per_seq((1, 1, ATTN_WIDTH))] + [key_page(s) for s in range(n_pages)] + [slab(w) for w in f32_weights],
        out_specs=[pl.BlockSpec((tq, V_DIM), lambda b, h, i, pt: (b * nq + i, h)),
                   pl.BlockSpec((n_pages * N_HEADS, page * N_HEADS), lambda b, h, i, pt: (seq_of(b, h, i), 0)),
                   per_seq((1, N_HEADS, V_DIM))] + [slab(w) for w in f32_weights],
        scratch_shapes=[pltpu.VMEM((1, tq), F32), pltpu.VMEM((1, tq), F32), pltpu.VMEM((V_DIM, tq), F32)] * 2,
    )
    kernel = functools.partial(_prompt_attn_kernel, n_key_pages=n_pages, n_casts=len(f32_weights), tq=tq,
                               lam_init=lam_init)
    return pl.pallas_call(
        kernel,
        out_shape=[jax.ShapeDtypeStruct(k.shape, BF16),
                   jax.ShapeDtypeStruct((db * n_pages * N_HEADS, page * N_HEADS), F32),
                   jax.ShapeDtypeStruct((db, N_HEADS, V_DIM), F32)]
                  + [jax.ShapeDtypeStruct(w.shape, BF16) for w in f32_weights],
        grid_spec=grid_spec,
        compiler_params=_cparams(("arbitrary", "arbitrary", "arbitrary")),
        name="prompt_attention",
    )(page_table, *lams, g_sub_col, qt, k, vt, qrows, k_new, *([cache_kt] * n_pages), *f32_weights)


def _sequence_weights(qrows, k_new, k_refs, lam):
    n_pages = len(k_refs)
    _, _, page = k_refs[0].shape
    s = jnp.concatenate([jnp.dot(qrows, k_ref[...].reshape(N_QK_HEADS * HEAD_DIM, page),
                                 preferred_element_type=F32) for k_ref in k_refs], axis=1)
    s_new = jnp.sum(qrows * k_new, axis=-1, keepdims=True)
    m = jnp.maximum(jnp.max(s, axis=-1, keepdims=True), s_new)
    p = jnp.exp(s - m)
    p_new = jnp.exp(s_new - m)
    l = jnp.sum(p, axis=-1, keepdims=True) + p_new
    a = p / l
    a_new = p_new / l
    w = a[:N_HEADS] - lam * a[N_HEADS:]
    w_new = a_new[:N_HEADS] - lam * a_new[N_HEADS:]
    w_rows = jnp.concatenate([w[:, i * page:(i + 1) * page] for i in range(n_pages)], axis=0)
    tok = lax.broadcasted_iota(jnp.int32, (page, page * N_HEADS), 0)
    dst = lax.broadcasted_iota(jnp.int32, (page, page * N_HEADS), 1)
    spread = (dst // N_HEADS == tok).astype(F32)
    wide = jnp.dot(w_rows, spread, preferred_element_type=F32)
    row = lax.broadcasted_iota(jnp.int32, wide.shape, 0)
    col = lax.broadcasted_iota(jnp.int32, wide.shape, 1)
    return jnp.where(row % N_HEADS == col % N_HEADS, wide, 0.0), w_new


def _paged_values_slots(step, slots, wsel_refs, v_refs, oacc_ref, *, pages_per_seq, n_valid_pages):
    per_step = len(v_refs)
    page = v_refs[0].shape[0]
    for i in slots:
        flat = step * per_step + i
        seq = jnp.minimum(flat, n_valid_pages - 1) // pages_per_seq
        part = jnp.dot(wsel_refs[i][...], v_refs[i][...].reshape(page * N_HEADS, V_DIM),
                       preferred_element_type=F32)
        oacc_ref[seq] += jnp.where(flat < n_valid_pages, part, 0.0)


def _sample_finish_kernel(gs_ref, oacc_ref, wnew_ref, vn_ref, o_ref, *, lam_init):
    full = oacc_ref[...] + wnew_ref[...] * vn_ref[...]
    o_ref[...] = _sub_norm(full, gs_ref[...], lam_init).astype(o_ref.dtype)


def _sample_finish(g_sub, oacc, w_new, v_new, lam_init):
    db = oacc.shape[0]
    whole = pl.BlockSpec((db, N_HEADS, V_DIM), lambda i: (0, 0, 0))
    return pl.pallas_call(
        functools.partial(_sample_finish_kernel, lam_init=lam_init),
        out_shape=jax.ShapeDtypeStruct((db, N_HEADS, V_DIM), BF16),
        grid=(1,),
        in_specs=[pl.BlockSpec((1, V_DIM), lambda i: (0, 0)), whole, whole, whole],
        out_specs=whole,
        compiler_params=_cparams(("arbitrary",)),
        name="sample_finish",
    )(g_sub, oacc, w_new, v_new)


def _pool_matmul(pooled_groups, wp_ref, scale):
    outs = [jnp.dot(pg.astype(BF16), wp_ref[g], preferred_element_type=F32)
            for g, pg in enumerate(pooled_groups)]
    return jnp.concatenate(outs, axis=1) * scale


def _pool_prompt_kernel(u_ref, halo_ref, wp_ref, sc_ref, o_ref, ext_ref, *, ts):
    i = pl.program_id(1)
    group = wp_ref.shape[1]
    u = u_ref[...]
    ext_ref[pl.ds(POOL_HALO, ts), :] = u
    ext_ref[pl.ds(0, POOL_HALO), :] = jnp.where(i > 0, halo_ref[...], 0.0)
    pos1 = (i * ts + lax.broadcasted_iota(jnp.int32, (ts, 1), 0) + 1).astype(F32)
    pooled = []
    for g, win in enumerate(POOL_WINDOWS):
        cols = slice(g * group, (g + 1) * group)
        wsum = u[:, cols]
        for k in range(1, win):
            wsum = wsum + ext_ref[pl.ds(POOL_HALO - k, ts), cols]
        cnt = jnp.minimum(pos1, float(win))
        pooled.append(wsum / cnt - u[:, cols])
    o_ref[...] = _pool_matmul(pooled, wp_ref, sc_ref[...]).astype(o_ref.dtype)


def _pool_prompt(u, w_pool_bf, pool_scale, batch, seq, ts):
    m, width = u.shape
    nt = seq // ts
    r = ts // POOL_HALO
    return pl.pallas_call(
        functools.partial(_pool_prompt_kernel, ts=ts),
        out_shape=jax.ShapeDtypeStruct((m, width), BF16),
        grid=(batch, nt),
        in_specs=[pl.BlockSpec((ts, width), lambda b, i: (b * nt + i, 0)),
                  pl.BlockSpec((POOL_HALO, width), lambda b, i: (jnp.maximum((b * nt + i) * r - 1, 0), 0)),
                  pl.BlockSpec(w_pool_bf.shape, lambda b, i: (0, 0, 0)),
                  pl.BlockSpec((1, width), lambda b, i: (0, 0))],
        out_specs=pl.BlockSpec((ts, width), lambda b, i: (b * nt + i, 0)),
        scratch_shapes=[pltpu.VMEM((ts + POOL_HALO, width), F32)],
        compiler_params=_cparams(("arbitrary", "arbitrary")),
        name="pool_prompt",
    )(u, u, w_pool_bf, pool_scale)


def _pool_sample_kernel(u_ref, st_ref, wp_ref, sc_ref, o_ref):
    group = wp_ref.shape[1]
    u = u_ref[...]
    pooled = []
    for g, win in enumerate(POOL_WINDOWS):
        cols = slice(g * group, (g + 1) * group)
        wsum = u[:, cols]
        for k in range(1, win):
            wsum = wsum + st_ref[POOL_STATE - k, :, cols]
        pooled.append(wsum / float(win) - u[:, cols])
    o_ref[...] = _pool_matmul(pooled, wp_ref, sc_ref[...]).astype(o_ref.dtype)


def _pool_sample(u, state_t, w_pool_bf, pool_scale, tb):
    m, width = u.shape
    return pl.pallas_call(
        _pool_sample_kernel,
        out_shape=jax.ShapeDtypeStruct((m, width), BF16),
        grid=(m // tb,),
        in_specs=[pl.BlockSpec((tb, width), lambda i: (i, 0)),
                  pl.BlockSpec((POOL_STATE, tb, width), lambda i: (0, i, 0)),
                  pl.BlockSpec(w_pool_bf.shape, lambda i: (0, 0, 0)),
                  pl.BlockSpec((1, width), lambda i: (0, 0))],
        out_specs=pl.BlockSpec((tb, width), lambda i: (i, 0)),
        compiler_params=_cparams(("arbitrary",)),
        name="pool_sample",
    )(u, state_t, w_pool_bf, pool_scale)


def _out_proj_kernel(o_ref, p_ref, w_ref, h_ref, out_ref):
    half = o_ref.shape[1]
    acc = jnp.dot(o_ref[...], w_ref[pl.ds(0, half), :], preferred_element_type=F32)
    acc += jnp.dot(p_ref[...], w_ref[pl.ds(half, half), :], preferred_element_type=F32)
    out_ref[...] = h_ref[...] + acc


def _out_proj(o, pool, w_out_bf, h, tm):
    m, d = h.shape
    row = lambda i: (i, 0)
    return pl.pallas_call(
        _out_proj_kernel,
        out_shape=jax.ShapeDtypeStruct((m, d), F32),
        grid=(m // tm,),
        in_specs=[pl.BlockSpec((tm, o.shape[1]), row),
                  pl.BlockSpec((tm, pool.shape[1]), row),
                  pl.BlockSpec(w_out_bf.shape, lambda i: (0, 0)),
                  pl.BlockSpec((tm, d), row)],
        out_specs=pl.BlockSpec((tm, d), row),
        compiler_params=_cparams(("arbitrary",)),
        name="out_proj",
    )(o, pool, w_out_bf, h)


def _ffn_body(h_ref, g_ref, wg_ref, wu_ref, wd_ref, out_ref, xn_ref, acc_ref, riders=()):
    f = pl.program_id(1)
    riders = list(riders)

    def ride():
        if riders:
            riders.pop(0)()

    @pl.when(f == 0)
    def _():
        xn_ref[...] = _rms(h_ref[...], g_ref[...]).astype(BF16)
        acc_ref[...] = jnp.zeros(acc_ref.shape, F32)

    xn = xn_ref[...]
    acts = []
    for c in range(wg_ref.shape[1] // MXU_COLS):
        cols = slice(c * MXU_COLS, (c + 1) * MXU_COLS)
        gate = jnp.dot(xn, wg_ref[:, cols], preferred_element_type=F32)
        ride()
        up = jnp.dot(xn, wu_ref[:, cols], preferred_element_type=F32)
        ride()
        acts.append((gate * jax.nn.sigmoid(gate) * up).astype(BF16))
    acc_ref[...] += jnp.dot(jnp.concatenate(acts, axis=1), wd_ref[...], preferred_element_type=F32)
    while riders:
        ride()

    @pl.when(f == pl.num_programs(1) - 1)
    def _():
        out_ref[...] = h_ref[...] + acc_ref[...]


def _ffn_kernel(h_ref, g_ref, wg_ref, wu_ref, wd_ref, out_ref, xn_ref, acc_ref):
    _ffn_body(h_ref, g_ref, wg_ref, wu_ref, wd_ref, out_ref, xn_ref, acc_ref)


def _ffn_paged_kernel(page_ids_ref, h_ref, g_ref, wg_ref, wu_ref, wd_ref, *rest,
                      per_step, pages_per_seq, n_valid_pages):
    wsel_refs = rest[:per_step]
    v_refs = rest[per_step:2 * per_step]
    out_ref, oacc_ref, xn_ref, acc_ref = rest[2 * per_step:]
    step = pl.program_id(0) * pl.num_programs(1) + pl.program_id(1)

    @pl.when(step == 0)
    def _():
        oacc_ref[...] = jnp.zeros(oacc_ref.shape, F32)

    n_groups = 2 * (wg_ref.shape[1] // MXU_COLS) + 1
    size = -(-per_step // n_groups)
    riders = [functools.partial(_paged_values_slots, step, range(lo, min(lo + size, per_step)), wsel_refs, v_refs,
                                oacc_ref, pages_per_seq=pages_per_seq, n_valid_pages=n_valid_pages)
              for lo in range(0, per_step, size)]
    _ffn_body(h_ref, g_ref, wg_ref, wu_ref, wd_ref, out_ref, xn_ref, acc_ref, riders)


def _ffn_specs(d, tm, tf, nf, index):
    return [pl.BlockSpec((tm, d), index(lambda i, f: (i, 0))),
            pl.BlockSpec((1, d), index(lambda i, f: (0, 0))),
            pl.BlockSpec((d, tf), index(lambda i, f: (0, f))),
            pl.BlockSpec((d, tf), index(lambda i, f: (0, nf + f))),
            pl.BlockSpec((tf, d), index(lambda i, f: (f, 0)))]


def _ffn(h, g_ffn, w_gate_up_bf, w_down_bf, tm, tf):
    m, d = h.shape
    nf = w_down_bf.shape[0] // tf
    plain = lambda fn: fn
    return pl.pallas_call(
        _ffn_kernel,
        out_shape=jax.ShapeDtypeStruct((m, d), F32),
        grid=(m // tm, nf),
        in_specs=_ffn_specs(d, tm, tf, nf, plain),
        out_specs=pl.BlockSpec((tm, d), lambda i, f: (i, 0)),
        scratch_shapes=[pltpu.VMEM((tm, d), BF16), pltpu.VMEM((tm, d), F32)],
        compiler_params=_cparams(("arbitrary", "arbitrary")),
        name="ffn",
    )(h, g_ffn, w_gate_up_bf, w_gate_up_bf, w_down_bf)


def _ffn_paged(h, g_ffn, w_gate_up_bf, w_down_bf, tm, tf, page_table, wsel, cache_v, li):
    m, d = h.shape
    nf = w_down_bf.shape[0] // tf
    n_steps = (m // tm) * nf
    db, pages_per_seq = page_table.shape
    page = cache_v.shape[2]
    n_valid = db * pages_per_seq
    per_step = -(-n_valid // n_steps)
    page_ids = page_table.reshape(-1)
    wsel = wsel.reshape(n_valid, N_HEADS, page * N_HEADS)
    with_ids = lambda fn: (lambda i, f, ids: fn(i, f))
    slot_page = lambda i, f, s: jnp.minimum((i * nf + f) * per_step + s, n_valid - 1)

    def w_spec(s):
        return pl.BlockSpec((None, N_HEADS, page * N_HEADS), lambda i, f, ids: (slot_page(i, f, s), 0, 0))

    def v_spec(s):
        return pl.BlockSpec((None, None, page, N_HEADS, V_DIM),
                            lambda i, f, ids: (li, ids[slot_page(i, f, s)], 0, 0, 0))

    grid_spec = pltpu.PrefetchScalarGridSpec(
        num_scalar_prefetch=1,
        grid=(m // tm, nf),
        in_specs=_ffn_specs(d, tm, tf, nf, with_ids)
                 + [w_spec(s) for s in range(per_step)]
                 + [v_spec(s) for s in range(per_step)],
        out_specs=[pl.BlockSpec((tm, d), lambda i, f, ids: (i, 0)),
                   pl.BlockSpec((db, N_HEADS, V_DIM), lambda i, f, ids: (0, 0, 0))],
        scratch_shapes=[pltpu.VMEM((tm, d), BF16), pltpu.VMEM((tm, d), F32)],
    )
    return pl.pallas_call(
        functools.partial(_ffn_paged_kernel, per_step=per_step, pages_per_seq=pages_per_seq,
                          n_valid_pages=n_valid),
        out_shape=[jax.ShapeDtypeStruct((m, d), F32), jax.ShapeDtypeStruct((db, N_HEADS, V_DIM), F32)],
        grid_spec=grid_spec,
        compiler_params=_cparams(("arbitrary", "arbitrary")),
        name="ffn_paged",
    )(page_ids, h, g_ffn, w_gate_up_bf, w_gate_up_bf, w_down_bf, *([wsel] * per_step), *([cache_v] * per_step))


def _ple_kernel(h_ref, p_ref, gp_ref, wg_ref, wp_ref, gf_ref, y_ref, *, final_norm):
    h = h_ref[...]
    xn = _rms(h, gp_ref[...]).astype(BF16)
    gate = jax.nn.sigmoid(jnp.dot(xn, wg_ref[...], preferred_element_type=F32))
    proj = jnp.dot(p_ref[...].astype(BF16), wp_ref[...], preferred_element_type=F32)
    y = h + gate * proj
    y_ref[...] = _rms(y, gf_ref[...]) if final_norm else y


def _ple(h, p, g_ple, w_gate_bf, w_proj_bf, g_final, tm, final_norm):
    m, d = h.shape
    row = lambda i: (i, 0)
    const = lambda i: (0, 0)
    return pl.pallas_call(
        functools.partial(_ple_kernel, final_norm=final_norm),
        out_shape=jax.ShapeDtypeStruct((m, d), F32),
        grid=(m // tm,),
        in_specs=[pl.BlockSpec((tm, d), row),
                  pl.BlockSpec((tm, p.shape[1]), row),
                  pl.BlockSpec((1, d), const),
                  pl.BlockSpec(w_gate_bf.shape, const),
                  pl.BlockSpec(w_proj_bf.shape, const),
                  pl.BlockSpec((1, d), const)],
        out_specs=pl.BlockSpec((tm, d), row),
        compiler_params=_cparams(("arbitrary",)),
        name="ple_final",
    )(h, p, g_ple, w_gate_bf, w_proj_bf, g_final)


def _rope_tables(positions):
    half = HEAD_DIM // 2
    inv = ROPE_THETA ** (-jnp.arange(half, dtype=F32) / half)
    ang = positions.astype(F32)[:, None] * inv[None, :]
    cos, sin = jnp.cos(ang), jnp.sin(ang)
    reps = LANES // HEAD_DIM
    return jnp.tile(jnp.concatenate([cos, cos], axis=1), (1, reps)), \
        jnp.tile(jnp.concatenate([-sin, sin], axis=1), (1, reps))


def _row_tile(m, want):
    return want if m % want == 0 else m


def kernel(x_prompt, x_sample, p_prompt, p_sample, cache_k, cache_v, state_pool, page_table, g_mix, w_in, w_out, lambda_q1, lambda_k1, lambda_q2, lambda_k2, g_sub, w_pool, pool_scale, g_ffn, w_gate_up, w_down, g_ple, w_ple_gate, w_ple_proj, g_final):
    batch, seq, d = x_prompt.shape
    db, ds, _ = x_sample.shape
    depth = g_mix.shape[0]
    n_pages = page_table.shape[1]
    page = cache_k.shape[2]
    past = n_pages * page
    assert ds == 1 and d == 2 * ATTN_WIDTH

    hp = x_prompt.reshape(batch * seq, d)
    hs = x_sample.reshape(db * ds, d)
    cos_p, sin_p = _rope_tables(jnp.arange(seq))
    cos_s, sin_s = _rope_tables(jnp.full((db,), past))
    row2 = lambda a: a.reshape(1, -1)

    tm_p = _row_tile(batch * seq, 512)
    tq = _row_tile(seq, 512)
    ts = _row_tile(seq, 512)
    cache_kt = jnp.transpose(cache_k, (0, 1, 3, 4, 2))
    tf = 512
    qk_scale = HEAD_DIM ** -0.5

    outs = {k: [] for k in ("kp", "vp", "pp", "ks", "vs", "ps")}
    for li in range(depth):
        lam_init = 0.8 - 0.6 * math.exp(-0.3 * li)
        w_in_bf = w_in[li].astype(BF16)
        w_pool_bf = w_pool[li].astype(BF16)
        w_pp_bf = w_ple_proj[li].astype(BF16)
        lams = tuple(row2(a[li]) for a in (lambda_q1, lambda_k1, lambda_q2, lambda_k2))
        gsub = row2(g_sub[li])

        qs, kfs, vfs, us = _in_proj_sample(hs, row2(g_mix[li]), w_in_bf, cos_s, sin_s, qk_scale)
        col_sub = jnp.arange(ATTN_WIDTH) // HEAD_DIM
        row_sub = (jnp.arange(N_QK_HEADS) % N_HEADS) * 2 + jnp.arange(N_QK_HEADS) // N_HEADS
        qrows = qs.astype(F32)[:, None, :] * (col_sub[None, :] == row_sub[:, None]).astype(F32)[None]
        qt, kb, kt, vt, vf, u = _in_proj_prompt(hp, row2(g_mix[li]), w_in_bf, cos_p, sin_p, batch, seq, tq,
                                                qk_scale * math.log2(math.e))
        o, wsel, w_new, w_out_bf, w_gu_bf, w_down_bf, w_pg_bf = _prompt_attention(
            qt, kb, vt, lams, g_sub[li].reshape(-1, 1), batch, seq, tq, lam_init,
            page_table, qrows, kfs.reshape(db, 1, ATTN_WIDTH), cache_kt, li,
            [w_out[li], w_gate_up[li], w_down[li], w_ple_gate[li]])
        pool = _pool_prompt(u, w_pool_bf, row2(pool_scale[li]), batch, seq, ts)
        h1 = _out_proj(o, pool, w_out_bf, hp, tm_p)
        h2, oacc = _ffn_paged(h1, row2(g_ffn[li]), w_gu_bf, w_down_bf, tm_p, tf, page_table, wsel, cache_v, li)
        outs["kp"].append(jnp.transpose(kt.reshape(batch, N_QK_HEADS, HEAD_DIM, seq), (0, 3, 1, 2)))
        outs["vp"].append(vf.reshape(batch, seq, N_HEADS, V_DIM))
        outs["pp"].append(u.reshape(batch, seq, -1)[:, seq - POOL_STATE:])
        p_rows = p_prompt[li].reshape(batch * seq, -1)
        last = li == depth - 1
        gfin = row2(g_final)
        hp_next = _ple(h2, p_rows, row2(g_ple[li]), w_pg_bf, w_pp_bf, gfin, tm_p, last)

        o_s = _sample_finish(gsub, oacc, w_new, vfs.reshape(db, N_HEADS, V_DIM), lam_init).reshape(db, -1)
        state = state_pool[li]
        pool_s = _pool_sample(us, jnp.swapaxes(state, 0, 1), w_pool_bf, row2(pool_scale[li]), db)
        h1s = _out_proj(o_s, pool_s, w_out_bf, hs, db)
        h2s = _ffn(h1s, row2(g_ffn[li]), w_gu_bf, w_down_bf, db, tf)
        hs_next = _ple(h2s, p_sample[li].reshape(db * ds, -1), row2(g_ple[li]), w_pg_bf, w_pp_bf, gfin, db, last)
        outs["ks"].append(kfs.reshape(db, ds, N_QK_HEADS, HEAD_DIM))
        outs["vs"].append(vfs.reshape(db, ds, N_HEADS, V_DIM))
        outs["ps"].append(jnp.concatenate([state[:, 1:], us[:, None, :]], axis=1))
        hp, hs = hp_next, hs_next

    y_prompt = hp.reshape(batch, seq, d)
    y_sample = hs.reshape(db, ds, d)
    return (y_prompt, y_sample, jnp.stack(outs["kp"]), jnp.stack(outs["vp"]), jnp.stack(outs["pp"]),
            jnp.stack(outs["ks"]), jnp.stack(outs["vs"]), jnp.stack(outs["ps"]))
```

```python
import functools
import math

import jax
import jax.numpy as jnp
from jax import lax
from jax.experimental import pallas as pl
from jax.experimental.pallas import tpu as pltpu

F32 = jnp.float32
BF16 = jnp.bfloat16

N_HEADS = 8
HEAD_DIM = 64
V_DIM = 2 * HEAD_DIM
N_QK_HEADS = 2 * N_HEADS
ATTN_WIDTH = N_HEADS * V_DIM
POOL_WINDOWS = (2, 4, 8, 16)
POOL_STATE = max(POOL_WINDOWS) - 1
POOL_HALO = 16
ROPE_THETA = 10000.0
EPS = 1e-6
LANES = 128
MXU_COLS = 256
VMEM_LIMIT = 56 * 1024 * 1024


def _cparams(sem):
    return pltpu.CompilerParams(dimension_semantics=sem, vmem_limit_bytes=VMEM_LIMIT)


def _rms(x, g):
    ms = jnp.mean(x * x, axis=-1, keepdims=True)
    return x * lax.rsqrt(ms + EPS) * g


def _rope_chunk(x, cos, sin_signed, first_half):
    swapped = jnp.where(first_half, pltpu.roll(x, 96, 1), pltpu.roll(x, 32, 1))
    return x * cos + swapped * sin_signed


def _col_chunks(xn_ref, w_ref):
    xn = xn_ref[...]
    for c in range(w_ref.shape[1] // MXU_COLS):
        cols = slice(c * MXU_COLS, (c + 1) * MXU_COLS)
        yield cols, jnp.dot(xn, w_ref[:, cols], preferred_element_type=F32)


def _rope(a, cos_ref, sin_ref):
    cos = cos_ref[...]
    sin = sin_ref[...]
    lane = lax.broadcasted_iota(jnp.int32, cos.shape, 1)
    first_half = (lane % HEAD_DIM) < (HEAD_DIM // 2)
    parts = [_rope_chunk(a[:, c * LANES:(c + 1) * LANES], cos, sin, first_half)
             for c in range(a.shape[1] // LANES)]
    return jnp.concatenate(parts, axis=1)


def _in_proj_prompt_kernel(x_ref, g_ref, w_ref, cos_ref, sin_ref,
                           qt_ref, kb_ref, kt_ref, vt_ref, vf_ref, u_ref, xn_ref, *, q_scale):
    width = qt_ref.shape[0]
    xn_ref[...] = _rms(x_ref[...], g_ref[...]).astype(BF16)
    for cols, acc in _col_chunks(xn_ref, w_ref):
        part, local = cols.start // width, slice(cols.start % width, cols.start % width + MXU_COLS)
        if part == 0:
            qt_ref[local, :] = (_rope(acc, cos_ref, sin_ref) * q_scale).T.astype(BF16)
        elif part == 1:
            k = _rope(acc, cos_ref, sin_ref)
            kb_ref[:, local] = k.astype(BF16)
            kt_ref[local, :] = k.T
        elif part == 2:
            vf_ref[:, local] = acc
            vt_ref[local, :] = acc.T.astype(BF16)
        else:
            u_ref[:, local] = acc


def _in_proj_prompt(x, g_mix, w_in_bf, cos_t, sin_t, batch, seq, tm, q_scale):
    m, d = x.shape
    width = ATTN_WIDTH
    nt = seq // tm
    row = lambda i: (i, 0)
    tile = lambda i: (i, 0, 0)
    tab = lambda i: (i % nt, 0)
    outs = [jax.ShapeDtypeStruct((m // tm, width, tm), BF16),
            jax.ShapeDtypeStruct((m, width), BF16),
            jax.ShapeDtypeStruct((batch, width, seq), F32),
            jax.ShapeDtypeStruct((m // tm, width, tm), BF16),
            jax.ShapeDtypeStruct((m, width), F32),
            jax.ShapeDtypeStruct((m, width), F32)]
    return pl.pallas_call(
        functools.partial(_in_proj_prompt_kernel, q_scale=q_scale),
        out_shape=outs,
        grid=(m // tm,),
        in_specs=[pl.BlockSpec((tm, d), row),
                  pl.BlockSpec((1, d), lambda i: (0, 0)),
                  pl.BlockSpec(w_in_bf.shape, lambda i: (0, 0), pipeline_mode=pl.Buffered(1)),
                  pl.BlockSpec((tm, LANES), tab),
                  pl.BlockSpec((tm, LANES), tab)],
        out_specs=[pl.BlockSpec((None, width, tm), tile),
                   pl.BlockSpec((tm, width), row),
                   pl.BlockSpec((None, width, tm), lambda i: (i // nt, 0, i % nt)),
                   pl.BlockSpec((None, width, tm), tile),
                   pl.BlockSpec((tm, width), row),
                   pl.BlockSpec((tm, width), row)],
        scratch_shapes=[pltpu.VMEM((tm, d), BF16)],
        compiler_params=_cparams(("arbitrary",)),
        name="in_proj_prompt",
    )(x, g_mix, w_in_bf, cos_t, sin_t)


def _in_proj_sample_kernel(x_ref, g_ref, w_ref, cos_ref, sin_ref, q_ref, k_ref, v_ref, u_ref, xn_ref, *, q_scale):
    n = pl.program_id(1)

    @pl.when(n == 0)
    def _():
        xn_ref[...] = _rms(x_ref[...], g_ref[...]).astype(BF16)

    acc = jnp.dot(xn_ref[...], w_ref[...], preferred_element_type=F32)

    @pl.when(n == 0)
    def _():
        q_ref[...] = (_rope(acc, cos_ref, sin_ref) * q_scale).astype(BF16)

    @pl.when(n == 1)
    def _():
        k_ref[...] = _rope(acc, cos_ref, sin_ref)

    @pl.when(n == 2)
    def _():
        v_ref[...] = acc

    @pl.when(n == 3)
    def _():
        u_ref[...] = acc


def _in_proj_sample(x, g_mix, w_in_bf, cos_t, sin_t, q_scale):
    m, d = x.shape
    width = ATTN_WIDTH
    whole = lambda i, n: (0, 0)
    outs = [jax.ShapeDtypeStruct((m, width), BF16)] + [jax.ShapeDtypeStruct((m, width), F32)] * 3
    return pl.pallas_call(
        functools.partial(_in_proj_sample_kernel, q_scale=q_scale),
        out_shape=outs,
        grid=(1, 4),
        in_specs=[pl.BlockSpec((m, d), whole),
                  pl.BlockSpec((1, d), whole),
                  pl.BlockSpec((d, width), lambda i, n: (0, n)),
                  pl.BlockSpec((m, LANES), whole),
                  pl.BlockSpec((m, LANES), whole)],
        out_specs=[pl.BlockSpec((m, width), whole)] * 4,
        scratch_shapes=[pltpu.VMEM((m, d), BF16)],
        compiler_params=_cparams(("arbitrary", "arbitrary")),
        name="in_proj_sample",
    )(x, g_mix, w_in_bf, cos_t, sin_t)


def _lambda_value(lq1, lk1, lq2, lk2, lam_init):
    a = jnp.sum(lq1 * lk1, axis=-1, keepdims=True)
    b = jnp.sum(lq2 * lk2, axis=-1, keepdims=True)
    return jnp.exp(a) - jnp.exp(b) + lam_init


def _sub_norm(o, g_sub, lam_init):
    return _rms(o, g_sub) * (1.0 - lam_init)


def _prompt_attn_kernel(pt_ref, lq1_ref, lk1_ref, lq2_ref, lk2_ref, gs_ref, qt_ref, k_ref, vt_ref,
                        qrows_ref, kn_ref, *rest, n_key_pages, n_casts, tq, lam_init):
    key_pages = rest[:n_key_pages]
    cast_in = rest[n_key_pages:n_key_pages + n_casts]
    o_ref, wsel_ref, wnew_ref = rest[n_key_pages + n_casts:n_key_pages + n_casts + 3]
    cast_out = rest[n_key_pages + n_casts + 3:n_key_pages + 2 * n_casts + 3]
    m0_ref, l0_ref, a0_ref, m1_ref, l1_ref, a1_ref = rest[n_key_pages + 2 * n_casts + 3:]
    qi = pl.program_id(2)

    @pl.when(qi == 0)
    def _():
        for src, dst in zip(cast_in, cast_out):
            dst[...] = src[...].astype(dst.dtype)

    hq = tq // 2
    qt = qt_ref[...]
    drow = lax.broadcasted_iota(jnp.int32, qt.shape, 0)
    zero = jnp.zeros_like(qt)
    qts = (jnp.where(drow < HEAD_DIM, qt, zero), jnp.where(drow >= HEAD_DIM, qt, zero))
    stats = ((m0_ref, l0_ref, a0_ref), (m1_ref, l1_ref, a1_ref))

    for m_ref, l_ref, a_ref in stats:
        m_ref[...] = jnp.full(m_ref.shape, -jnp.inf, F32)
        l_ref[...] = jnp.zeros(l_ref.shape, F32)
        a_ref[...] = jnp.zeros(a_ref.shape, F32)

    def run_chains(chains):
        scores = [jnp.dot(kj, qc[:, lanes], preferred_element_type=F32)
                  for kj, _, qc, _, lanes, _ in chains]
        probs = []
        for st, (_, _, _, (m_ref, l_ref, _), lanes, keep) in zip(scores, chains):
            if keep is not None:
                st = jnp.where(keep, st, -jnp.inf)
            m_old = m_ref[:, lanes]
            m_new = jnp.maximum(m_old, jnp.max(st, axis=0, keepdims=True))
            pt = jnp.exp2(st - m_new)
            alpha = jnp.exp2(m_old - m_new)
            l_ref[:, lanes] = alpha * l_ref[:, lanes] + jnp.sum(pt, axis=0, keepdims=True)
            m_ref[:, lanes] = m_new
            probs.append((pt.astype(BF16), alpha))
        for (pt, alpha), (_, vtj, _, (_, _, a_ref), lanes, _) in zip(probs, chains):
            a_ref[:, lanes] = alpha * a_ref[:, lanes] + jnp.dot(vtj, pt, preferred_element_type=F32)

    def body(j, carry):
        kj = k_ref[pl.ds(pl.multiple_of(j * tq, tq), tq), :]
        vtj = vt_ref[j]
        run_chains([(kj, vtj, qc, refs, slice(half * hq, (half + 1) * hq), None)
                    for qc, refs in zip(qts, stats) for half in range(2)])
        return carry

    lax.fori_loop(0, qi, body, 0)

    start = pl.multiple_of(qi * tq, tq)
    k_lo = k_ref[pl.ds(start, hq), :]
    k_all = k_ref[pl.ds(start, tq), :]
    vt_all = vt_ref[qi]
    tri = (lax.broadcasted_iota(jnp.int32, (hq, hq), 0) <= lax.broadcasted_iota(jnp.int32, (hq, hq), 1))
    late = (lax.broadcasted_iota(jnp.int32, (tq, hq), 0) <= lax.broadcasted_iota(jnp.int32, (tq, hq), 1) + hq)
    diagonal = []
    for qc, refs in zip(qts, stats):
        diagonal.append((k_lo, vt_all[:, :hq], qc, refs, slice(0, hq), tri))
        diagonal.append((k_all, vt_all, qc, refs, slice(hq, tq), late))
    run_chains(diagonal)

    lam = _lambda_value(lq1_ref[...], lk1_ref[...], lq2_ref[...], lk2_ref[...], lam_init)
    wsel, w_new = _sequence_weights(qrows_ref[0], kn_ref[0], key_pages, lam)
    wsel_ref[...] = wsel
    wnew_ref[0] = jnp.broadcast_to(w_new, wnew_ref.shape[1:])

    ot = a0_ref[...] * (1.0 / l0_ref[...]) - a1_ref[...] * (lam / l1_ref[...])
    ms = jnp.mean(ot * ot, axis=0, keepdims=True)
    ot = ot * lax.rsqrt(ms + EPS) * gs_ref[...] * (1.0 - lam_init)
    o_ref[...] = ot.T.astype(o_ref.dtype)


def _slab_rows(rows, n_steps):
    need = -(-rows // n_steps)
    return next(r for r in range(16, rows + 1, 16) if rows % r == 0 and r >= need)


def _prompt_attention(qt, k, vt, lams, g_sub_col, batch, seq, tq, lam_init,
                      page_table, qrows, k_new, cache_kt, li, f32_weights):
    nq = seq // tq
    db, n_pages = page_table.shape
    page = cache_kt.shape[4]
    n_steps = batch * N_HEADS * nq
    assert n_steps >= db, "one sample sequence rides on each grid step"
    seq_of = lambda b, h, i: jnp.minimum((b * N_HEADS + h) * nq + i, db - 1)

    def slab(w):
        rows = _slab_rows(w.shape[0], batch * N_HEADS)
        last = w.shape[0] // rows - 1
        return pl.BlockSpec((rows, w.shape[1]), lambda b, h, i, pt: (jnp.minimum(b * N_HEADS + h, last), 0))
    small = lambda shape: pl.BlockSpec(shape, lambda b, h, i, pt: (0, 0))
    per_seq = lambda shape: pl.BlockSpec(shape, lambda b, h, i, pt: (seq_of(b, h, i), 0, 0))

    def key_page(s):
        return pl.BlockSpec((None, None, N_QK_HEADS, HEAD_DIM, page),
                            lambda b, h, i, pt: (li, pt[seq_of(b, h, i), s], 0, 0, 0))

    grid_spec = pltpu.PrefetchScalarGridSpec(
        num_scalar_prefetch=1,
        grid=(batch, N_HEADS, nq),
        in_specs=[small((1, HEAD_DIM))] * 4 + [small((V_DIM, 1))] + [
            pl.BlockSpec((None, V_DIM, tq), lambda b, h, i, pt: (b * nq + i, h, 0)),
            pl.BlockSpec((seq, V_DIM), lambda b, h, i, pt: (b, h)),
            pl.BlockSpec((nq, V_DIM, tq), lambda b, h, i, pt: (b, h, 0)),
            per_seq((1, N_QK_HEADS, ATTN_WIDTH)),
            per_seq((1, 1, ATTN_WIDTH))] + [key_page(s) for s in range(n_pages)] + [slab(w) for w in f32_weights],
        out_specs=[pl.BlockSpec((tq, V_DIM), lambda b, h, i, pt: (b * nq + i, h)),
                   pl.BlockSpec((n_pages * N_HEADS, page * N_HEADS), lambda b, h, i, pt: (seq_of(b, h, i), 0)),
                   per_seq((1, N_HEADS, V_DIM))] + [slab(w) for w in f32_weights],
        scratch_shapes=[pltpu.VMEM((1, tq), F32), pltpu.VMEM((1, tq), F32), pltpu.VMEM((V_DIM, tq), F32)] * 2,
    )
    kernel = functools.partial(_prompt_attn_kernel, n_key_pages=n_pages, n_casts=len(f32_weights), tq=tq,
                               lam_init=lam_init)
    return pl.pallas_call(
        kernel,
        out_shape=[jax.ShapeDtypeStruct(k.shape, BF16),
                   jax.ShapeDtypeStruct((db * n_pages * N_HEADS, page * N_HEADS), F32),
                   jax.ShapeDtypeStruct((db, N_HEADS, V_DIM), F32)]
                  + [jax.ShapeDtypeStruct(w.shape, BF16) for w in f32_weights],
        grid_spec=grid_spec,
        compiler_params=_cparams(("arbitrary", "arbitrary", "arbitrary")),
        name="prompt_attention",
    )(page_table, *lams, g_sub_col, qt, k, vt, qrows, k_new, *([cache_kt] * n_pages), *f32_weights)


def _sequence_weights(qrows, k_new, k_refs, lam):
    n_pages = len(k_refs)
    _, _, page = k_refs[0].shape
    s = jnp.concatenate([jnp.dot(qrows, k_ref[...].reshape(N_QK_HEADS * HEAD_DIM, page),
                                 preferred_element_type=F32) for k_ref in k_refs], axis=1)
    s_new = jnp.sum(qrows * k_new, axis=-1, keepdims=True)
    m = jnp.maximum(jnp.max(s, axis=-1, keepdims=True), s_new)
    p = jnp.exp(s - m)
    p_new = jnp.exp(s_new - m)
    l = jnp.sum(p, axis=-1, keepdims=True) + p_new
    a = p / l
    a_new = p_new / l
    w = a[:N_HEADS] - lam * a[N_HEADS:]
    w_new = a_new[:N_HEADS] - lam * a_new[N_HEADS:]
    w_rows = jnp.concatenate([w[:, i * page:(i + 1) * page] for i in range(n_pages)], axis=0)
    tok = lax.broadcasted_iota(jnp.int32, (page, page * N_HEADS), 0)
    dst = lax.broadcasted_iota(jnp.int32, (page, page * N_HEADS), 1)
    spread = (dst // N_HEADS == tok).astype(F32)
    wide = jnp.dot(w_rows, spread, preferred_element_type=F32)
    row = lax.broadcasted_iota(jnp.int32, wide.shape, 0)
    col = lax.broadcasted_iota(jnp.int32, wide.shape, 1)
    return jnp.where(row % N_HEADS == col % N_HEADS, wide, 0.0), w_new


def _paged_values_slots(step, slots, wsel_refs, v_refs, oacc_ref, *, pages_per_seq, n_valid_pages):
    per_step = len(v_refs)
    page = v_refs[0].shape[0]
    for i in slots:
        flat = step * per_step + i
        seq = jnp.minimum(flat, n_valid_pages - 1) // pages_per_seq
        part = jnp.dot(wsel_refs[i][...], v_refs[i][...].reshape(page * N_HEADS, V_DIM),
                       preferred_element_type=F32)
        oacc_ref[seq] += jnp.where(flat < n_valid_pages, part, 0.0)


def _sample_finish_kernel(gs_ref, oacc_ref, wnew_ref, vn_ref, o_ref, *, lam_init):
    full = oacc_ref[...] + wnew_ref[...] * vn_ref[...]
    o_ref[...] = _sub_norm(full, gs_ref[...], lam_init).astype(o_ref.dtype)


def _sample_finish(g_sub, oacc, w_new, v_new, lam_init):
    db = oacc.shape[0]
    whole = pl.BlockSpec((db, N_HEADS, V_DIM), lambda i: (0, 0, 0))
    return pl.pallas_call(
        functools.partial(_sample_finish_kernel, lam_init=lam_init),
        out_shape=jax.ShapeDtypeStruct((db, N_HEADS, V_DIM), BF16),
        grid=(1,),
        in_specs=[pl.BlockSpec((1, V_DIM), lambda i: (0, 0)), whole, whole, whole],
        out_specs=whole,
        compiler_params=_cparams(("arbitrary",)),
        name="sample_finish",
    )(g_sub, oacc, w_new, v_new)


def _pool_matmul(pooled_groups, wp_ref, scale):
    outs = [jnp.dot(pg.astype(BF16), wp_ref[g], preferred_element_type=F32)
            for g, pg in enumerate(pooled_groups)]
    return jnp.concatenate(outs, axis=1) * scale


def _pool_prompt_kernel(u_ref, halo_ref, wp_ref, sc_ref, o_ref, ext_ref, *, ts):
    i = pl.program_id(1)
    group = wp_ref.shape[1]
    u = u_ref[...]
    ext_ref[pl.ds(POOL_HALO, ts), :] = u
    ext_ref[pl.ds(0, POOL_HALO), :] = jnp.where(i > 0, halo_ref[...], 0.0)
    pos1 = (i * ts + lax.broadcasted_iota(jnp.int32, (ts, 1), 0) + 1).astype(F32)
    pooled = []
    for g, win in enumerate(POOL_WINDOWS):
        cols = slice(g * group, (g + 1) * group)
        wsum = u[:, cols]
        for k in range(1, win):
            wsum = wsum + ext_ref[pl.ds(POOL_HALO - k, ts), cols]
        cnt = jnp.minimum(pos1, float(win))
        pooled.append(wsum / cnt - u[:, cols])
    o_ref[...] = _pool_matmul(pooled, wp_ref, sc_ref[...]).astype(o_ref.dtype)


def _pool_prompt(u, w_pool_bf, pool_scale, batch, seq, ts):
    m, width = u.shape
    nt = seq // ts
    r = ts // POOL_HALO
    return pl.pallas_call(
        functools.partial(_pool_prompt_kernel, ts=ts),
        out_shape=jax.ShapeDtypeStruct((m, width), BF16),
        grid=(batch, nt),
        in_specs=[pl.BlockSpec((ts, width), lambda b, i: (b * nt + i, 0)),
                  pl.BlockSpec((POOL_HALO, width), lambda b, i: (jnp.maximum((b * nt + i) * r - 1, 0), 0)),
                  pl.BlockSpec(w_pool_bf.shape, lambda b, i: (0, 0, 0)),
                  pl.BlockSpec((1, width), lambda b, i: (0, 0))],
        out_specs=pl.BlockSpec((ts, width), lambda b, i: (b * nt + i, 0)),
        scratch_shapes=[pltpu.VMEM((ts + POOL_HALO, width), F32)],
        compiler_params=_cparams(("arbitrary", "arbitrary")),
        name="pool_prompt",
    )(u, u, w_pool_bf, pool_scale)


def _pool_sample_kernel(u_ref, st_ref, wp_ref, sc_ref, o_ref):
    group = wp_ref.shape[1]
    u = u_ref[...]
    pooled = []
    for g, win in enumerate(POOL_WINDOWS):
        cols = slice(g * group, (g + 1) * group)
        wsum = u[:, cols]
        for k in range(1, win):
            wsum = wsum + st_ref[POOL_STATE - k, :, cols]
        pooled.append(wsum / float(win) - u[:, cols])
    o_ref[...] = _pool_matmul(pooled, wp_ref, sc_ref[...]).astype(o_ref.dtype)


def _pool_sample(u, state_t, w_pool_bf, pool_scale, tb):
    m, width = u.shape
    return pl.pallas_call(
        _pool_sample_kernel,
        out_shape=jax.ShapeDtypeStruct((m, width), BF16),
        grid=(m // tb,),
        in_specs=[pl.BlockSpec((tb, width), lambda i: (i, 0)),
                  pl.BlockSpec((POOL_STATE, tb, width), lambda i: (0, i, 0)),
                  pl.BlockSpec(w_pool_bf.shape, lambda i: (0, 0, 0)),
                  pl.BlockSpec((1, width), lambda i: (0, 0))],
        out_specs=pl.BlockSpec((tb, width), lambda i: (i, 0)),
        compiler_params=_cparams(("arbitrary",)),
        name="pool_sample",
    )(u, state_t, w_pool_bf, pool_scale)


def _out_proj_kernel(o_ref, p_ref, w_ref, h_ref, out_ref):
    half = o_ref.shape[1]
    acc = jnp.dot(o_ref[...], w_ref[pl.ds(0, half), :], preferred_element_type=F32)
    acc += jnp.dot(p_ref[...], w_ref[pl.ds(half, half), :], preferred_element_type=F32)
    out_ref[...] = h_ref[...] + acc


def _out_proj(o, pool, w_out_bf, h, tm):
    m, d = h.shape
    row = lambda i: (i, 0)
    return pl.pallas_call(
        _out_proj_kernel,
        out_shape=jax.ShapeDtypeStruct((m, d), F32),
        grid=(m // tm,),
        in_specs=[pl.BlockSpec((tm, o.shape[1]), row),
                  pl.BlockSpec((tm, pool.shape[1]), row),
                  pl.BlockSpec(w_out_bf.shape, lambda i: (0, 0)),
                  pl.BlockSpec((tm, d), row)],
        out_specs=pl.BlockSpec((tm, d), row),
        compiler_params=_cparams(("arbitrary",)),
        name="out_proj",
    )(o, pool, w_out_bf, h)


def _ffn_body(h_ref, g_ref, wg_ref, wu_ref, wd_ref, out_ref, xn_ref, acc_ref, riders=()):
    f = pl.program_id(1)
    riders = list(riders)

    def ride():
        if riders:
            riders.pop(0)()

    @pl.when(f == 0)
    def _():
        xn_ref[...] = _rms(h_ref[...], g_ref[...]).astype(BF16)
        acc_ref[...] = jnp.zeros(acc_ref.shape, F32)

    xn = xn_ref[...]
    acts = []
    for c in range(wg_ref.shape[1] // MXU_COLS):
        cols = slice(c * MXU_COLS, (c + 1) * MXU_COLS)
        gate = jnp.dot(xn, wg_ref[:, cols], preferred_element_type=F32)
        ride()
        up = jnp.dot(xn, wu_ref[:, cols], preferred_element_type=F32)
        ride()
        acts.append((gate * jax.nn.sigmoid(gate) * up).astype(BF16))
    acc_ref[...] += jnp.dot(jnp.concatenate(acts, axis=1), wd_ref[...], preferred_element_type=F32)
    while riders:
        ride()

    @pl.when(f == pl.num_programs(1) - 1)
    def _():
        out_ref[...] = h_ref[...] + acc_ref[...]


def _ffn_kernel(h_ref, g_ref, wg_ref, wu_ref, wd_ref, out_ref, xn_ref, acc_ref):
    _ffn_body(h_ref, g_ref, wg_ref, wu_ref, wd_ref, out_ref, xn_ref, acc_ref)


def _ffn_paged_kernel(page_ids_ref, h_ref, g_ref, wg_ref, wu_ref, wd_ref, *rest,
                      per_step, pages_per_seq, n_valid_pages):
    wsel_refs = rest[:per_step]
    v_refs = rest[per_step:2 * per_step]
    out_ref, oacc_ref, xn_ref, acc_ref = rest[2 * per_step:]
    step = pl.program_id(0) * pl.num_programs(1) + pl.program_id(1)

    @pl.when(step == 0)
    def _():
        oacc_ref[...] = jnp.zeros(oacc_ref.shape, F32)

    n_groups = 2 * (wg_ref.shape[1] // MXU_COLS) + 1
    size = -(-per_step // n_groups)
    riders = [functools.partial(_paged_values_slots, step, range(lo, min(lo + size, per_step)), wsel_refs, v_refs,
                                oacc_ref, pages_per_seq=pages_per_seq, n_valid_pages=n_valid_pages)
              for lo in range(0, per_step, size)]
    _ffn_body(h_ref, g_ref, wg_ref, wu_ref, wd_ref, out_ref, xn_ref, acc_ref, riders)


def _ffn_specs(d, tm, tf, nf, index):
    return [pl.BlockSpec((tm, d), index(lambda i, f: (i, 0))),
            pl.BlockSpec((1, d), index(lambda i, f: (0, 0))),
            pl.BlockSpec((d, tf), index(lambda i, f: (0, f))),
            pl.BlockSpec((d, tf), index(lambda i, f: (0, nf + f))),
            pl.BlockSpec((tf, d), index(lambda i, f: (f, 0)))]


def _ffn(h, g_ffn, w_gate_up_bf, w_down_bf, tm, tf):
    m, d = h.shape
    nf = w_down_bf.shape[0] // tf
    plain = lambda fn: fn
    return pl.pallas_call(
        _ffn_kernel,
        out_shape=jax.ShapeDtypeStruct((m, d), F32),
        grid=(m // tm, nf),
        in_specs=_ffn_specs(d, tm, tf, nf, plain),
        out_specs=pl.BlockSpec((tm, d), lambda i, f: (i, 0)),
        scratch_shapes=[pltpu.VMEM((tm, d), BF16), pltpu.VMEM((tm, d), F32)],
        compiler_params=_cparams(("arbitrary", "arbitrary")),
        name="ffn",
    )(h, g_ffn, w_gate_up_bf, w_gate_up_bf, w_down_bf)


def _ffn_paged(h, g_ffn, w_gate_up_bf, w_down_bf, tm, tf, page_table, wsel, cache_v, li):
    m, d = h.shape
    nf = w_down_bf.shape[0] // tf
    n_steps = (m // tm) * nf
    db, pages_per_seq = page_table.shape
    page = cache_v.shape[2]
    n_valid = db * pages_per_seq
    per_step = -(-n_valid // n_steps)
    page_ids = page_table.reshape(-1)
    wsel = wsel.reshape(n_valid, N_HEADS, page * N_HEADS)
    with_ids = lambda fn: (lambda i, f, ids: fn(i, f))
    slot_page = lambda i, f, s: jnp.minimum((i * nf + f) * per_step + s, n_valid - 1)

    def w_spec(s):
        return pl.BlockSpec((None, N_HEADS, page * N_HEADS), lambda i, f, ids: (slot_page(i, f, s), 0, 0))

    def v_spec(s):
        return pl.BlockSpec((None, None, page, N_HEADS, V_DIM),
                            lambda i, f, ids: (li, ids[slot_page(i, f, s)], 0, 0, 0))

    grid_spec = pltpu.PrefetchScalarGridSpec(
        num_scalar_prefetch=1,
        grid=(m // tm, nf),
        in_specs=_ffn_specs(d, tm, tf, nf, with_ids)
                 + [w_spec(s) for s in range(per_step)]
                 + [v_spec(s) for s in range(per_step)],
        out_specs=[pl.BlockSpec((tm, d), lambda i, f, ids: (i, 0)),
                   pl.BlockSpec((db, N_HEADS, V_DIM), lambda i, f, ids: (0, 0, 0))],
        scratch_shapes=[pltpu.VMEM((tm, d), BF16), pltpu.VMEM((tm, d), F32)],
    )
    return pl.pallas_call(
        functools.partial(_ffn_paged_kernel, per_step=per_step, pages_per_seq=pages_per_seq,
                          n_valid_pages=n_valid),
        out_shape=[jax.ShapeDtypeStruct((m, d), F32), jax.ShapeDtypeStruct((db, N_HEADS, V_DIM), F32)],
        grid_spec=grid_spec,
        compiler_params=_cparams(("arbitrary", "arbitrary")),
        name="ffn_paged",
    )(page_ids, h, g_ffn, w_gate_up_bf, w_gate_up_bf, w_down_bf, *([wsel] * per_step), *([cache_v] * per_step))


def _ple_kernel(h_ref, p_ref, gp_ref, wg_ref, wp_ref, gf_ref, y_ref, *, final_norm):
    h = h_ref[...]
    xn = _rms(h, gp_ref[...]).astype(BF16)
    gate = jax.nn.sigmoid(jnp.dot(xn, wg_ref[...], preferred_element_type=F32))
    proj = jnp.dot(p_ref[...].astype(BF16), wp_ref[...], preferred_element_type=F32)
    y = h + gate * proj
    y_ref[...] = _rms(y, gf_ref[...]) if final_norm else y


def _ple(h, p, g_ple, w_gate_bf, w_proj_bf, g_final, tm, final_norm):
    m, d = h.shape
    row = lambda i: (i, 0)
    const = lambda i: (0, 0)
    return pl.pallas_call(
        functools.partial(_ple_kernel, final_norm=final_norm),
        out_shape=jax.ShapeDtypeStruct((m, d), F32),
        grid=(m // tm,),
        in_specs=[pl.BlockSpec((tm, d), row),
                  pl.BlockSpec((tm, p.shape[1]), row),
                  pl.BlockSpec((1, d), const),
                  pl.BlockSpec(w_gate_bf.shape, const),
                  pl.BlockSpec(w_proj_bf.shape, const),
                  pl.BlockSpec((1, d), const)],
        out_specs=pl.BlockSpec((tm, d), row),
        compiler_params=_cparams(("arbitrary",)),
        name="ple_final",
    )(h, p, g_ple, w_gate_bf, w_proj_bf, g_final)


def _rope_tables(positions):
    half = HEAD_DIM // 2
    inv = ROPE_THETA ** (-jnp.arange(half, dtype=F32) / half)
    ang = positions.astype(F32)[:, None] * inv[None, :]
    cos, sin = jnp.cos(ang), jnp.sin(ang)
    reps = LANES // HEAD_DIM
    return jnp.tile(jnp.concatenate([cos, cos], axis=1), (1, reps)), \
        jnp.tile(jnp.concatenate([-sin, sin], axis=1), (1, reps))


def _row_tile(m, want):
    return want if m % want == 0 else m


def kernel(x_prompt, x_sample, p_prompt, p_sample, cache_k, cache_v, state_pool, page_table, g_mix, w_in, w_out, lambda_q1, lambda_k1, lambda_q2, lambda_k2, g_sub, w_pool, pool_scale, g_ffn, w_gate_up, w_down, g_ple, w_ple_gate, w_ple_proj, g_final):
    batch, seq, d = x_prompt.shape
    db, ds, _ = x_sample.shape
    depth = g_mix.shape[0]
    n_pages = page_table.shape[1]
    page = cache_k.shape[2]
    past = n_pages * page
    assert ds == 1 and d == 2 * ATTN_WIDTH

    hp = x_prompt.reshape(batch * seq, d)
    hs = x_sample.reshape(db * ds, d)
    cos_p, sin_p = _rope_tables(jnp.arange(seq))
    cos_s, sin_s = _rope_tables(jnp.full((db,), past))
    row2 = lambda a: a.reshape(1, -1)

    tm_p = _row_tile(batch * seq, 512)
    tq = _row_tile(seq, 512)
    ts = _row_tile(seq, 512)
    cache_kt = jnp.transpose(cache_k, (0, 1, 3, 4, 2))
    tf = 512
    qk_scale = HEAD_DIM ** -0.5

    outs = {k: [] for k in ("kp", "vp", "pp", "ks", "vs", "ps")}
    for li in range(depth):
        lam_init = 0.8 - 0.6 * math.exp(-0.3 * li)
        w_in_bf = w_in[li].astype(BF16)
        w_pool_bf = w_pool[li].astype(BF16)
        w_pp_bf = w_ple_proj[li].astype(BF16)
        lams = tuple(row2(a[li]) for a in (lambda_q1, lambda_k1, lambda_q2, lambda_k2))
        gsub = row2(g_sub[li])

        qs, kfs, vfs, us = _in_proj_sample(hs, row2(g_mix[li]), w_in_bf, cos_s, sin_s, qk_scale)
        col_sub = jnp.arange(ATTN_WIDTH) // HEAD_DIM
        row_sub = (jnp.arange(N_QK_HEADS) % N_HEADS) * 2 + jnp.arange(N_QK_HEADS) // N_HEADS
        qrows = qs.astype(F32)[:, None, :] * (col_sub[None, :] == row_sub[:, None]).astype(F32)[None]
        qt, kb, kt, vt, vf, u = _in_proj_prompt(hp, row2(g_mix[li]), w_in_bf, cos_p, sin_p, batch, seq, tq,
                                                qk_scale * math.log2(math.e))
        o, wsel, w_new, w_out_bf, w_gu_bf, w_down_bf, w_pg_bf = _prompt_attention(
            qt, kb, vt, lams, g_sub[li].reshape(-1, 1), batch, seq, tq, lam_init,
            page_table, qrows, kfs.reshape(db, 1, ATTN_WIDTH), cache_kt, li,
            [w_out[li], w_gate_up[li], w_down[li], w_ple_gate[li]])
        pool = _pool_prompt(u, w_pool_bf, row2(pool_scale[li]), batch, seq, ts)
        h1 = _out_proj(o, pool, w_out_bf, hp, tm_p)
        h2, oacc = _ffn_paged(h1, row2(g_ffn[li]), w_gu_bf, w_down_bf, tm_p, tf, page_table, wsel, cache_v, li)
        outs["kp"].append(jnp.transpose(kt.reshape(batch, N_QK_HEADS, HEAD_DIM, seq), (0, 3, 1, 2)))
        outs["vp"].append(vf.reshape(batch, seq, N_HEADS, V_DIM))
        outs["pp"].append(u.reshape(batch, seq, -1)[:, seq - POOL_STATE:])
        p_rows = p_prompt[li].reshape(batch * seq, -1)
        last = li == depth - 1
        gfin = row2(g_final)
        hp_next = _ple(h2, p_rows, row2(g_ple[li]), w_pg_bf, w_pp_bf, gfin, tm_p, last)

        o_s = _sample_finish(gsub, oacc, w_new, vfs.reshape(db, N_HEADS, V_DIM), lam_init).reshape(db, -1)
        state = state_pool[li]
        pool_s = _pool_sample(us, jnp.swapaxes(state, 0, 1), w_pool_bf, row2(pool_scale[li]), db)
        h1s = _out_proj(o_s, pool_s, w_out_bf, hs, db)
        h2s = _ffn(h1s, row2(g_ffn[li]), w_gu_bf, w_down_bf, db, tf)
        hs_next = _ple(h2s, p_sample[li].reshape(db * ds, -1), row2(g_ple[li]), w_pg_bf, w_pp_bf, gfin, db, last)
        outs["ks"].append(kfs.reshape(db, ds, N_QK_HEADS, HEAD_DIM))
        outs["vs"].append(vfs.reshape(db, ds, N_HEADS, V_DIM))
        outs["ps"].append(jnp.concatenate([state[:, 1:], us[:, None, :]], axis=1))
        hp, hs = hp_next, hs_next

    y_prompt = hp.reshape(batch, seq, d)
    y_sample = hs.reshape(db, ds, d)
    return (y_prompt, y_sample, jnp.stack(outs["kp"]), jnp.stack(outs["vp"]), jnp.stack(outs["pp"]),
            jnp.stack(outs["ks"]), jnp.stack(outs["vs"]), jnp.stack(outs["ps"]))
```

```python
import functools
import math

import jax
import jax.numpy as jnp
from jax import lax
from jax.experimental import pallas as pl
from jax.experimental.pallas import tpu as pltpu

F32 = jnp.float32
BF16 = jnp.bfloat16

N_HEADS = 8
HEAD_DIM = 64
V_DIM = 2 * HEAD_DIM
N_QK_HEADS = 2 * N_HEADS
ATTN_WIDTH = N_HEADS * V_DIM
POOL_WINDOWS = (2, 4, 8, 16)
POOL_STATE = max(POOL_WINDOWS) - 1
POOL_HALO = 16
ROPE_THETA = 10000.0
EPS = 1e-6
LANES = 128
MXU_COLS = 256
VMEM_LIMIT = 56 * 1024 * 1024


def _cparams(sem):
    return pltpu.CompilerParams(dimension_semantics=sem, vmem_limit_bytes=VMEM_LIMIT)


def _rms(x, g):
    ms = jnp.mean(x * x, axis=-1, keepdims=True)
    return x * lax.rsqrt(ms + EPS) * g


def _rope_chunk(x, cos, sin_signed, first_half):
    swapped = jnp.where(first_half, pltpu.roll(x, 96, 1), pltpu.roll(x, 32, 1))
    return x * cos + swapped * sin_signed


def _col_chunks(xn_ref, w_ref):
    xn = xn_ref[...]
    for c in range(w_ref.shape[1] // MXU_COLS):
        cols = slice(c * MXU_COLS, (c + 1) * MXU_COLS)
        yield cols, jnp.dot(xn, w_ref[:, cols], preferred_element_type=F32)


def _rope(a, cos_ref, sin_ref):
    cos = cos_ref[...]
    sin = sin_ref[...]
    lane = lax.broadcasted_iota(jnp.int32, cos.shape, 1)
    first_half = (lane % HEAD_DIM) < (HEAD_DIM // 2)
    parts = [_rope_chunk(a[:, c * LANES:(c + 1) * LANES], cos, sin, first_half)
             for c in range(a.shape[1] // LANES)]
    return jnp.concatenate(parts, axis=1)


def _in_proj_prompt_kernel(x_ref, g_ref, w_ref, cos_ref, sin_ref,
                           qt_ref, kb_ref, kt_ref, vt_ref, vf_ref, u_ref, xn_ref, *, q_scale):
    width = qt_ref.shape[0]
    xn_ref[...] = _rms(x_ref[...], g_ref[...]).astype(BF16)
    for cols, acc in _col_chunks(xn_ref, w_ref):
        part, local = cols.start // width, slice(cols.start % width, cols.start % width + MXU_COLS)
        if part == 0:
            qt_ref[local, :] = (_rope(acc, cos_ref, sin_ref) * q_scale).T.astype(BF16)
        elif part == 1:
            k = _rope(acc, cos_ref, sin_ref)
            kb_ref[:, local] = k.astype(BF16)
            kt_ref[local, :] = k.T
        elif part == 2:
            vf_ref[:, local] = acc
            vt_ref[local, :] = acc.T.astype(BF16)
        else:
            u_ref[:, local] = acc


def _in_proj_prompt(x, g_mix, w_in_bf, cos_t, sin_t, batch, seq, tm, q_scale):
    m, d = x.shape
    width = ATTN_WIDTH
    nt = seq // tm
    row = lambda i: (i, 0)
    tile = lambda i: (i, 0, 0)
    tab = lambda i: (i % nt, 0)
    outs = [jax.ShapeDtypeStruct((m // tm, width, tm), BF16),
            jax.ShapeDtypeStruct((m, width), BF16),
            jax.ShapeDtypeStruct((batch, width, seq), F32),
            jax.ShapeDtypeStruct((m // tm, width, tm), BF16),
            jax.ShapeDtypeStruct((m, width), F32),
            jax.ShapeDtypeStruct((m, width), F32)]
    return pl.pallas_call(
        functools.partial(_in_proj_prompt_kernel, q_scale=q_scale),
        out_shape=outs,
        grid=(m // tm,),
        in_specs=[pl.BlockSpec((tm, d), row),
                  pl.BlockSpec((1, d), lambda i: (0, 0)),
                  pl.BlockSpec(w_in_bf.shape, lambda i: (0, 0), pipeline_mode=pl.Buffered(1)),
                  pl.BlockSpec((tm, LANES), tab),
                  pl.BlockSpec((tm, LANES), tab)],
        out_specs=[pl.BlockSpec((None, width, tm), tile),
                   pl.BlockSpec((tm, width), row),
                   pl.BlockSpec((None, width, tm), lambda i: (i // nt, 0, i % nt)),
                   pl.BlockSpec((None, width, tm), tile),
                   pl.BlockSpec((tm, width), row),
                   pl.BlockSpec((tm, width), row)],
        scratch_shapes=[pltpu.VMEM((tm, d), BF16)],
        compiler_params=_cparams(("arbitrary",)),
        name="in_proj_prompt",
    )(x, g_mix, w_in_bf, cos_t, sin_t)


def _in_proj_sample_kernel(x_ref, g_ref, w_ref, cos_ref, sin_ref, q_ref, k_ref, v_ref, u_ref, xn_ref, *, q_scale):
    n = pl.program_id(1)

    @pl.when(n == 0)
    def _():
        xn_ref[...] = _rms(x_ref[...], g_ref[...]).astype(BF16)

    acc = jnp.dot(xn_ref[...], w_ref[...], preferred_element_type=F32)

    @pl.when(n == 0)
    def _():
        q_ref[...] = (_rope(acc, cos_ref, sin_ref) * q_scale).astype(BF16)

    @pl.when(n == 1)
    def _():
        k_ref[...] = _rope(acc, cos_ref, sin_ref)

    @pl.when(n == 2)
    def _():
        v_ref[...] = acc

    @pl.when(n == 3)
    def _():
        u_ref[...] = acc


def _in_proj_sample(x, g_mix, w_in_bf, cos_t, sin_t, q_scale):
    m, d = x.shape
    width = ATTN_WIDTH
    whole = lambda i, n: (0, 0)
    outs = [jax.ShapeDtypeStruct((m, width), BF16)] + [jax.ShapeDtypeStruct((m, width), F32)] * 3
    return pl.pallas_call(
        functools.partial(_in_proj_sample_kernel, q_scale=q_scale),
        out_shape=outs,
        grid=(1, 4),
        in_specs=[pl.BlockSpec((m, d), whole),
                  pl.BlockSpec((1, d), whole),
                  pl.BlockSpec((d, width), lambda i, n: (0, n)),
                  pl.BlockSpec((m, LANES), whole),
                  pl.BlockSpec((m, LANES), whole)],
        out_specs=[pl.BlockSpec((m, width), whole)] * 4,
        scratch_shapes=[pltpu.VMEM((m, d), BF16)],
        compiler_params=_cparams(("arbitrary", "arbitrary")),
        name="in_proj_sample",
    )(x, g_mix, w_in_bf, cos_t, sin_t)


def _lambda_value(lq1, lk1, lq2, lk2, lam_init):
    a = jnp.sum(lq1 * lk1, axis=-1, keepdims=True)
    b = jnp.sum(lq2 * lk2, axis=-1, keepdims=True)
    return jnp.exp(a) - jnp.exp(b) + lam_init


def _sub_norm(o, g_sub, lam_init):
    return _rms(o, g_sub) * (1.0 - lam_init)


def _prompt_attn_kernel(pt_ref, lq1_ref, lk1_ref, lq2_ref, lk2_ref, gs_ref, qt_ref, k_ref, vt_ref,
                        qrows_ref, kn_ref, *rest, n_key_pages, n_casts, tq, lam_init):
    key_pages = rest[:n_key_pages]
    cast_in = rest[n_key_pages:n_key_pages + n_casts]
    o_ref, wsel_ref, wnew_ref = rest[n_key_pages + n_casts:n_key_pages + n_casts + 3]
    cast_out = rest[n_key_pages + n_casts + 3:n_key_pages + 2 * n_casts + 3]
    m0_ref, l0_ref, a0_ref, m1_ref, l1_ref, a1_ref = rest[n_key_pages + 2 * n_casts + 3:]
    qi = pl.program_id(2)

    @pl.when(qi == 0)
    def _():
        for src, dst in zip(cast_in, cast_out):
            dst[...] = src[...].astype(dst.dtype)

    hq = tq // 2
    qt = qt_ref[...]
    drow = lax.broadcasted_iota(jnp.int32, qt.shape, 0)
    zero = jnp.zeros_like(qt)
    qts = (jnp.where(drow < HEAD_DIM, qt, zero), jnp.where(drow >= HEAD_DIM, qt, zero))
    stats = ((m0_ref, l0_ref, a0_ref), (m1_ref, l1_ref, a1_ref))

    for m_ref, l_ref, a_ref in stats:
        m_ref[...] = jnp.full(m_ref.shape, -jnp.inf, F32)
        l_ref[...] = jnp.zeros(l_ref.shape, F32)
        a_ref[...] = jnp.zeros(a_ref.shape, F32)

    def run_chains(chains):
        scores = [jnp.dot(kj, qc[:, lanes], preferred_element_type=F32)
                  for kj, _, qc, _, lanes, _ in chains]
        probs = []
        for st, (_, _, _, (m_ref, l_ref, _), lanes, keep) in zip(scores, chains):
            if keep is not None:
                st = jnp.where(keep, st, -jnp.inf)
            m_old = m_ref[:, lanes]
            m_new = jnp.maximum(m_old, jnp.max(st, axis=0, keepdims=True))
            pt = jnp.exp2(st - m_new)
            alpha = jnp.exp2(m_old - m_new)
            l_ref[:, lanes] = alpha * l_ref[:, lanes] + jnp.sum(pt, axis=0, keepdims=True)
            m_ref[:, lanes] = m_new
            probs.append((pt.astype(BF16), alpha))
        for (pt, alpha), (_, vtj, _, (_, _, a_ref), lanes, _) in zip(probs, chains):
            a_ref[:, lanes] = alpha * a_ref[:, lanes] + jnp.dot(vtj, pt, preferred_element_type=F32)

    def body(j, carry):
        kj = k_ref[pl.ds(pl.multiple_of(j * tq, tq), tq), :]
        vtj = vt_ref[j]
        run_chains([(kj, vtj, qc, refs, slice(half * hq, (half + 1) * hq), None)
                    for qc, refs in zip(qts, stats) for half in range(2)])
        return carry

    lax.fori_loop(0, qi, body, 0)

    start = pl.multiple_of(qi * tq, tq)
    k_lo = k_ref[pl.ds(start, hq), :]
    k_all = k_ref[pl.ds(start, tq), :]
    vt_all = vt_ref[qi]
    tri = (lax.broadcasted_iota(jnp.int32, (hq, hq), 0) <= lax.broadcasted_iota(jnp.int32, (hq, hq), 1))
    late = (lax.broadcasted_iota(jnp.int32, (tq, hq), 0) <= lax.broadcasted_iota(jnp.int32, (tq, hq), 1) + hq)
    diagonal = []
    for qc, refs in zip(qts, stats):
        diagonal.append((k_lo, vt_all[:, :hq], qc, refs, slice(0, hq), tri))
        diagonal.append((k_all, vt_all, qc, refs, slice(hq, tq), late))
    run_chains(diagonal)

    lam = _lambda_value(lq1_ref[...], lk1_ref[...], lq2_ref[...], lk2_ref[...], lam_init)
    w_past, w_new = _sequence_weights(qrows_ref[0], kn_ref[0], key_pages, lam)
    wsel_ref[0] = w_past
    wnew_ref[0] = jnp.broadcast_to(w_new, wnew_ref.shape[1:])

    ot = a0_ref[...] * (1.0 / l0_ref[...]) - a1_ref[...] * (lam / l1_ref[...])
    ms = jnp.mean(ot * ot, axis=0, keepdims=True)
    ot = ot * lax.rsqrt(ms + EPS) * gs_ref[...] * (1.0 - lam_init)
    o_ref[...] = ot.T.astype(o_ref.dtype)


def _slab_rows(rows, n_steps):
    need = -(-rows // n_steps)
    return next(r for r in range(16, rows + 1, 16) if rows % r == 0 and r >= need)


def _prompt_attention(qt, k, vt, lams, g_sub_col, batch, seq, tq, lam_init,
                      page_table, qrows, k_new, cache_kt, li, f32_weights):
    nq = seq // tq
    db, n_pages = page_table.shape
    page = cache_kt.shape[4]
    n_steps = batch * N_HEADS * nq
    assert n_steps >= db, "one sample sequence rides on each grid step"
    seq_of = lambda b, h, i: jnp.minimum((b * N_HEADS + h) * nq + i, db - 1)

    def slab(w):
        rows = _slab_rows(w.shape[0], batch * N_HEADS)
        last = w.shape[0] // rows - 1
        return pl.BlockSpec((rows, w.shape[1]), lambda b, h, i, pt: (jnp.minimum(b * N_HEADS + h, last), 0))
    small = lambda shape: pl.BlockSpec(shape, lambda b, h, i, pt: (0, 0))
    per_seq = lambda shape: pl.BlockSpec(shape, lambda b, h, i, pt: (seq_of(b, h, i), 0, 0))

    def key_page(s):
        return pl.BlockSpec((None, None, N_QK_HEADS, HEAD_DIM, page),
                            lambda b, h, i, pt: (li, pt[seq_of(b, h, i), s], 0, 0, 0))

    grid_spec = pltpu.PrefetchScalarGridSpec(
        num_scalar_prefetch=1,
        grid=(batch, N_HEADS, nq),
        in_specs=[small((1, HEAD_DIM))] * 4 + [small((V_DIM, 1))] + [
            pl.BlockSpec((None, V_DIM, tq), lambda b, h, i, pt: (b * nq + i, h, 0)),
            pl.BlockSpec((seq, V_DIM), lambda b, h, i, pt: (b, h)),
            pl.BlockSpec((nq, V_DIM, tq), lambda b, h, i, pt: (b, h, 0)),
            per_seq((1, N_QK_HEADS, ATTN_WIDTH)),
            per_seq((1, 1, ATTN_WIDTH))] + [key_page(s) for s in range(n_pages)] + [slab(w) for w in f32_weights],
        out_specs=[pl.BlockSpec((tq, V_DIM), lambda b, h, i, pt: (b * nq + i, h)),
                   per_seq((1, N_HEADS, n_pages * page)),
                   per_seq((1, N_HEADS, V_DIM))] + [slab(w) for w in f32_weights],
        scratch_shapes=[pltpu.VMEM((1, tq), F32), pltpu.VMEM((1, tq), F32), pltpu.VMEM((V_DIM, tq), F32)] * 2,
    )
    kernel = functools.partial(_prompt_attn_kernel, n_key_pages=n_pages, n_casts=len(f32_weights), tq=tq,
                               lam_init=lam_init)
    return pl.pallas_call(
        kernel,
        out_shape=[jax.ShapeDtypeStruct(k.shape, BF16),
                   jax.ShapeDtypeStruct((db, N_HEADS, n_pages * page), F32),
                   jax.ShapeDtypeStruct((db, N_HEADS, V_DIM), F32)]
                  + [jax.ShapeDtypeStruct(w.shape, BF16) for w in f32_weights],
        grid_spec=grid_spec,
        compiler_params=_cparams(("arbitrary", "arbitrary", "arbitrary")),
        name="prompt_attention",
    )(page_table, *lams, g_sub_col, qt, k, vt, qrows, k_new, *([cache_kt] * n_pages), *f32_weights)


def _sequence_weights(qrows, k_new, k_refs, lam):
    n_pages = len(k_refs)
    _, _, page = k_refs[0].shape
    s = jnp.concatenate([jnp.dot(qrows, k_ref[...].reshape(N_QK_HEADS * HEAD_DIM, page),
                                 preferred_element_type=F32) for k_ref in k_refs], axis=1)
    s_new = jnp.sum(qrows * k_new, axis=-1, keepdims=True)
    m = jnp.maximum(jnp.max(s, axis=-1, keepdims=True), s_new)
    p = jnp.exp(s - m)
    p_new = jnp.exp(s_new - m)
    l = jnp.sum(p, axis=-1, keepdims=True) + p_new
    a = p / l
    a_new = p_new / l
    w = a[:N_HEADS] - lam * a[N_HEADS:]
    w_new = a_new[:N_HEADS] - lam * a_new[N_HEADS:]
    return w, w_new


def _paged_values_slots(step, slots, wsel_refs, v_refs, oacc_ref, *, pages_per_seq, n_valid_pages):
    per_step = len(v_refs)
    page = v_refs[0].shape[0]
    n_chains = 4
    for i in slots:
        flat = step * per_step + i
        seq = jnp.minimum(flat, n_valid_pages - 1) // pages_per_seq
        w = wsel_refs[i][...]
        parts = [jnp.zeros((N_HEADS, V_DIM), F32)] * n_chains
        for t in range(page):
            parts[t % n_chains] = parts[t % n_chains] + v_refs[i][t] * jnp.broadcast_to(w[:, t:t + 1],
                                                                                        (N_HEADS, V_DIM))
        part = (parts[0] + parts[1]) + (parts[2] + parts[3])
        oacc_ref[seq] += jnp.where(flat < n_valid_pages, part, 0.0)


def _sample_finish_kernel(gs_ref, oacc_ref, wnew_ref, vn_ref, o_ref, *, lam_init):
    full = oacc_ref[...] + wnew_ref[...] * vn_ref[...]
    o_ref[...] = _sub_norm(full, gs_ref[...], lam_init).astype(o_ref.dtype)


def _sample_finish(g_sub, oacc, w_new, v_new, lam_init):
    db = oacc.shape[0]
    whole = pl.BlockSpec((db, N_HEADS, V_DIM), lambda i: (0, 0, 0))
    return pl.pallas_call(
        functools.partial(_sample_finish_kernel, lam_init=lam_init),
        out_shape=jax.ShapeDtypeStruct((db, N_HEADS, V_DIM), BF16),
        grid=(1,),
        in_specs=[pl.BlockSpec((1, V_DIM), lambda i: (0, 0)), whole, whole, whole],
        out_specs=whole,
        compiler_params=_cparams(("arbitrary",)),
        name="sample_finish",
    )(g_sub, oacc, w_new, v_new)


def _pool_matmul(pooled_groups, wp_ref, scale):
    outs = [jnp.dot(pg.astype(BF16), wp_ref[g], preferred_element_type=F32)
            for g, pg in enumerate(pooled_groups)]
    return jnp.concatenate(outs, axis=1) * scale


def _pool_prompt_kernel(u_ref, halo_ref, wp_ref, sc_ref, o_ref, ext_ref, *, ts):
    i = pl.program_id(1)
    group = wp_ref.shape[1]
    u = u_ref[...]
    ext_ref[pl.ds(POOL_HALO, ts), :] = u
    ext_ref[pl.ds(0, POOL_HALO), :] = jnp.where(i > 0, halo_ref[...], 0.0)
    pos1 = (i * ts + lax.broadcasted_iota(jnp.int32, (ts, 1), 0) + 1).astype(F32)
    pooled = []
    for g, win in enumerate(POOL_WINDOWS):
        cols = slice(g * group, (g + 1) * group)
        wsum = u[:, cols]
        for k in range(1, win):
            wsum = wsum + ext_ref[pl.ds(POOL_HALO - k, ts), cols]
        cnt = jnp.minimum(pos1, float(win))
        pooled.append(wsum / cnt - u[:, cols])
    o_ref[...] = _pool_matmul(pooled, wp_ref, sc_ref[...]).astype(o_ref.dtype)


def _pool_prompt(u, w_pool_bf, pool_scale, batch, seq, ts):
    m, width = u.shape
    nt = seq // ts
    r = ts // POOL_HALO
    return pl.pallas_call(
        functools.partial(_pool_prompt_kernel, ts=ts),
        out_shape=jax.ShapeDtypeStruct((m, width), BF16),
        grid=(batch, nt),
        in_specs=[pl.BlockSpec((ts, width), lambda b, i: (b * nt + i, 0)),
                  pl.BlockSpec((POOL_HALO, width), lambda b, i: (jnp.maximum((b * nt + i) * r - 1, 0), 0)),
                  pl.BlockSpec(w_pool_bf.shape, lambda b, i: (0, 0, 0)),
                  pl.BlockSpec((1, width), lambda b, i: (0, 0))],
        out_specs=pl.BlockSpec((ts, width), lambda b, i: (b * nt + i, 0)),
        scratch_shapes=[pltpu.VMEM((ts + POOL_HALO, width), F32)],
        compiler_params=_cparams(("arbitrary", "arbitrary")),
        name="pool_prompt",
    )(u, u, w_pool_bf, pool_scale)


def _pool_sample_kernel(u_ref, st_ref, wp_ref, sc_ref, o_ref):
    group = wp_ref.shape[1]
    u = u_ref[...]
    pooled = []
    for g, win in enumerate(POOL_WINDOWS):
        cols = slice(g * group, (g + 1) * group)
        wsum = u[:, cols]
        for k in range(1, win):
            wsum = wsum + st_ref[POOL_STATE - k, :, cols]
        pooled.append(wsum / float(win) - u[:, cols])
    o_ref[...] = _pool_matmul(pooled, wp_ref, sc_ref[...]).astype(o_ref.dtype)


def _pool_sample(u, state_t, w_pool_bf, pool_scale, tb):
    m, width = u.shape
    return pl.pallas_call(
        _pool_sample_kernel,
        out_shape=jax.ShapeDtypeStruct((m, width), BF16),
        grid=(m // tb,),
        in_specs=[pl.BlockSpec((tb, width), lambda i: (i, 0)),
                  pl.BlockSpec((POOL_STATE, tb, width), lambda i: (0, i, 0)),
                  pl.BlockSpec(w_pool_bf.shape, lambda i: (0, 0, 0)),
                  pl.BlockSpec((1, width), lambda i: (0, 0))],
        out_specs=pl.BlockSpec((tb, width), lambda i: (i, 0)),
        compiler_params=_cparams(("arbitrary",)),
        name="pool_sample",
    )(u, state_t, w_pool_bf, pool_scale)


def _out_proj_kernel(o_ref, p_ref, w_ref, h_ref, out_ref):
    half = o_ref.shape[1]
    acc = jnp.dot(o_ref[...], w_ref[pl.ds(0, half), :], preferred_element_type=F32)
    acc += jnp.dot(p_ref[...], w_ref[pl.ds(half, half), :], preferred_element_type=F32)
    out_ref[...] = h_ref[...] + acc


def _out_proj(o, pool, w_out_bf, h, tm):
    m, d = h.shape
    row = lambda i: (i, 0)
    return pl.pallas_call(
        _out_proj_kernel,
        out_shape=jax.ShapeDtypeStruct((m, d), F32),
        grid=(m // tm,),
        in_specs=[pl.BlockSpec((tm, o.shape[1]), row),
                  pl.BlockSpec((tm, pool.shape[1]), row),
                  pl.BlockSpec(w_out_bf.shape, lambda i: (0, 0)),
                  pl.BlockSpec((tm, d), row)],
        out_specs=pl.BlockSpec((tm, d), row),
        compiler_params=_cparams(("arbitrary",)),
        name="out_proj",
    )(o, pool, w_out_bf, h)


def _ffn_body(h_ref, g_ref, wg_ref, wu_ref, wd_ref, out_ref, xn_ref, acc_ref, riders=()):
    f = pl.program_id(1)
    riders = list(riders)

    def ride():
        if riders:
            riders.pop(0)()

    @pl.when(f == 0)
    def _():
        xn_ref[...] = _rms(h_ref[...], g_ref[...]).astype(BF16)
        acc_ref[...] = jnp.zeros(acc_ref.shape, F32)

    xn = xn_ref[...]
    acts = []
    for c in range(wg_ref.shape[1] // MXU_COLS):
        cols = slice(c * MXU_COLS, (c + 1) * MXU_COLS)
        gate = jnp.dot(xn, wg_ref[:, cols], preferred_element_type=F32)
        ride()
        up = jnp.dot(xn, wu_ref[:, cols], preferred_element_type=F32)
        ride()
        acts.append((gate * jax.nn.sigmoid(gate) * up).astype(BF16))
    acc_ref[...] += jnp.dot(jnp.concatenate(acts, axis=1), wd_ref[...], preferred_element_type=F32)
    while riders:
        ride()

    @pl.when(f == pl.num_programs(1) - 1)
    def _():
        out_ref[...] = h_ref[...] + acc_ref[...]


def _ffn_kernel(h_ref, g_ref, wg_ref, wu_ref, wd_ref, out_ref, xn_ref, acc_ref):
    _ffn_body(h_ref, g_ref, wg_ref, wu_ref, wd_ref, out_ref, xn_ref, acc_ref)


def _ffn_paged_kernel(page_ids_ref, h_ref, g_ref, wg_ref, wu_ref, wd_ref, *rest,
                      per_step, pages_per_seq, n_valid_pages):
    wsel_refs = rest[:per_step]
    v_refs = rest[per_step:2 * per_step]
    out_ref, oacc_ref, xn_ref, acc_ref = rest[2 * per_step:]
    step = pl.program_id(0) * pl.num_programs(1) + pl.program_id(1)

    @pl.when(step == 0)
    def _():
        oacc_ref[...] = jnp.zeros(oacc_ref.shape, F32)

    n_groups = 2 * (wg_ref.shape[1] // MXU_COLS) + 1
    size = -(-per_step // n_groups)
    riders = [functools.partial(_paged_values_slots, step, range(lo, min(lo + size, per_step)), wsel_refs, v_refs,
                                oacc_ref, pages_per_seq=pages_per_seq, n_valid_pages=n_valid_pages)
              for lo in range(0, per_step, size)]
    _ffn_body(h_ref, g_ref, wg_ref, wu_ref, wd_ref, out_ref, xn_ref, acc_ref, riders)


def _ffn_specs(d, tm, tf, nf, index):
    return [pl.BlockSpec((tm, d), index(lambda i, f: (i, 0))),
            pl.BlockSpec((1, d), index(lambda i, f: (0, 0))),
            pl.BlockSpec((d, tf), index(lambda i, f: (0, f))),
            pl.BlockSpec((d, tf), index(lambda i, f: (0, nf + f))),
            pl.BlockSpec((tf, d), index(lambda i, f: (f, 0)))]


def _ffn(h, g_ffn, w_gate_up_bf, w_down_bf, tm, tf):
    m, d = h.shape
    nf = w_down_bf.shape[0] // tf
    plain = lambda fn: fn
    return pl.pallas_call(
        _ffn_kernel,
        out_shape=jax.ShapeDtypeStruct((m, d), F32),
        grid=(m // tm, nf),
        in_specs=_ffn_specs(d, tm, tf, nf, plain),
        out_specs=pl.BlockSpec((tm, d), lambda i, f: (i, 0)),
        scratch_shapes=[pltpu.VMEM((tm, d), BF16), pltpu.VMEM((tm, d), F32)],
        compiler_params=_cparams(("arbitrary", "arbitrary")),
        name="ffn",
    )(h, g_ffn, w_gate_up_bf, w_gate_up_bf, w_down_bf)


def _ffn_paged(h, g_ffn, w_gate_up_bf, w_down_bf, tm, tf, page_table, wsel, cache_v, li):
    m, d = h.shape
    nf = w_down_bf.shape[0] // tf
    n_steps = (m // tm) * nf
    db, pages_per_seq = page_table.shape
    page = cache_v.shape[2]
    n_valid = db * pages_per_seq
    per_step = -(-n_valid // n_steps)
    page_ids = page_table.reshape(-1)
    with_ids = lambda fn: (lambda i, f, ids: fn(i, f))
    slot_page = lambda i, f, s: jnp.minimum((i * nf + f) * per_step + s, n_valid - 1)

    def w_spec(s):
        return pl.BlockSpec((None, N_HEADS, page), lambda i, f, ids: (slot_page(i, f, s) // pages_per_seq, 0,
                                                                      slot_page(i, f, s) % pages_per_seq))

    def v_spec(s):
        return pl.BlockSpec((None, None, page, N_HEADS, V_DIM),
                            lambda i, f, ids: (li, ids[slot_page(i, f, s)], 0, 0, 0))

    grid_spec = pltpu.PrefetchScalarGridSpec(
        num_scalar_prefetch=1,
        grid=(m // tm, nf),
        in_specs=_ffn_specs(d, tm, tf, nf, with_ids)
                 + [w_spec(s) for s in range(per_step)]
                 + [v_spec(s) for s in range(per_step)],
        out_specs=[pl.BlockSpec((tm, d), lambda i, f, ids: (i, 0)),
                   pl.BlockSpec((db, N_HEADS, V_DIM), lambda i, f, ids: (0, 0, 0))],
        scratch_shapes=[pltpu.VMEM((tm, d), BF16), pltpu.VMEM((tm, d), F32)],
    )
    return pl.pallas_call(
        functools.partial(_ffn_paged_kernel, per_step=per_step, pages_per_seq=pages_per_seq,
                          n_valid_pages=n_valid),
        out_shape=[jax.ShapeDtypeStruct((m, d), F32), jax.ShapeDtypeStruct((db, N_HEADS, V_DIM), F32)],
        grid_spec=grid_spec,
        compiler_params=_cparams(("arbitrary", "arbitrary")),
        name="ffn_paged",
    )(page_ids, h, g_ffn, w_gate_up_bf, w_gate_up_bf, w_down_bf, *([wsel] * per_step), *([cache_v] * per_step))


def _ple_kernel(h_ref, p_ref, gp_ref, wg_ref, wp_ref, gf_ref, y_ref, *, final_norm):
    h = h_ref[...]
    xn = _rms(h, gp_ref[...]).astype(BF16)
    gate = jax.nn.sigmoid(jnp.dot(xn, wg_ref[...], preferred_element_type=F32))
    proj = jnp.dot(p_ref[...].astype(BF16), wp_ref[...], preferred_element_type=F32)
    y = h + gate * proj
    y_ref[...] = _rms(y, gf_ref[...]) if final_norm else y


def _ple(h, p, g_ple, w_gate_bf, w_proj_bf, g_final, tm, final_norm):
    m, d = h.shape
    row = lambda i: (i, 0)
    const = lambda i: (0, 0)
    return pl.pallas_call(
        functools.partial(_ple_kernel, final_norm=final_norm),
        out_shape=jax.ShapeDtypeStruct((m, d), F32),
        grid=(m // tm,),
        in_specs=[pl.BlockSpec((tm, d), row),
                  pl.BlockSpec((tm, p.shape[1]), row),
                  pl.BlockSpec((1, d), const),
                  pl.BlockSpec(w_gate_bf.shape, const),
                  pl.BlockSpec(w_proj_bf.shape, const),
                  pl.BlockSpec((1, d), const)],
        out_specs=pl.BlockSpec((tm, d), row),
        compiler_params=_cparams(("arbitrary",)),
        name="ple_final",
    )(h, p, g_ple, w_gate_bf, w_proj_bf, g_final)


def _rope_tables(positions):
    half = HEAD_DIM // 2
    inv = ROPE_THETA ** (-jnp.arange(half, dtype=F32) / half)
    ang = positions.astype(F32)[:, None] * inv[None, :]
    cos, sin = jnp.cos(ang), jnp.sin(ang)
    reps = LANES // HEAD_DIM
    return jnp.tile(jnp.concatenate([cos, cos], axis=1), (1, reps)), \
        jnp.tile(jnp.concatenate([-sin, sin], axis=1), (1, reps))


def _row_tile(m, want):
    return want if m % want == 0 else m


def kernel(x_prompt, x_sample, p_prompt, p_sample, cache_k, cache_v, state_pool, page_table, g_mix, w_in, w_out, lambda_q1, lambda_k1, lambda_q2, lambda_k2, g_sub, w_pool, pool_scale, g_ffn, w_gate_up, w_down, g_ple, w_ple_gate, w_ple_proj, g_final):
    batch, seq, d = x_prompt.shape
    db, ds, _ = x_sample.shape
    depth = g_mix.shape[0]
    n_pages = page_table.shape[1]
    page = cache_k.shape[2]
    past = n_pages * page
    assert ds == 1 and d == 2 * ATTN_WIDTH

    hp = x_prompt.reshape(batch * seq, d)
    hs = x_sample.reshape(db * ds, d)
    cos_p, sin_p = _rope_tables(jnp.arange(seq))
    cos_s, sin_s = _rope_tables(jnp.full((db,), past))
    row2 = lambda a: a.reshape(1, -1)

    tm_p = _row_tile(batch * seq, 512)
    tq = _row_tile(seq, 512)
    ts = _row_tile(seq, 512)
    cache_kt = jnp.transpose(cache_k, (0, 1, 3, 4, 2))
    tf = 512
    qk_scale = HEAD_DIM ** -0.5

    outs = {k: [] for k in ("kp", "vp", "pp", "ks", "vs", "ps")}
    for li in range(depth):
        lam_init = 0.8 - 0.6 * math.exp(-0.3 * li)
        w_in_bf = w_in[li].astype(BF16)
        w_pool_bf = w_pool[li].astype(BF16)
        w_pp_bf = w_ple_proj[li].astype(BF16)
        lams = tuple(row2(a[li]) for a in (lambda_q1, lambda_k1, lambda_q2, lambda_k2))
        gsub = row2(g_sub[li])

        qs, kfs, vfs, us = _in_proj_sample(hs, row2(g_mix[li]), w_in_bf, cos_s, sin_s, qk_scale)
        col_sub = jnp.arange(ATTN_WIDTH) // HEAD_DIM
        row_sub = (jnp.arange(N_QK_HEADS) % N_HEADS) * 2 + jnp.arange(N_QK_HEADS) // N_HEADS
        qrows = qs.astype(F32)[:, None, :] * (col_sub[None, :] == row_sub[:, None]).astype(F32)[None]
        qt, kb, kt, vt, vf, u = _in_proj_prompt(hp, row2(g_mix[li]), w_in_bf, cos_p, sin_p, batch, seq, tq,
                                                qk_scale * math.log2(math.e))
        o, wsel, w_new, w_out_bf, w_gu_bf, w_down_bf, w_pg_bf = _prompt_attention(
            qt, kb, vt, lams, g_sub[li].reshape(-1, 1), batch, seq, tq, lam_init,
            page_table, qrows, kfs.reshape(db, 1, ATTN_WIDTH), cache_kt, li,
            [w_out[li], w_gate_up[li], w_down[li], w_ple_gate[li]])
        pool = _pool_prompt(u, w_pool_bf, row2(pool_scale[li]), batch, seq, ts)
        h1 = _out_proj(o, pool, w_out_bf, hp, tm_p)
        h2, oacc = _ffn_paged(h1, row2(g_ffn[li]), w_gu_bf, w_down_bf, tm_p, tf, page_table, wsel, cache_v, li)
        outs["kp"].append(jnp.transpose(kt.reshape(batch, N_QK_HEADS, HEAD_DIM, seq), (0, 3, 1, 2)))
        outs["vp"].append(vf.reshape(batch, seq, N_HEADS, V_DIM))
        outs["pp"].append(u.reshape(batch, seq, -1)[:, seq - POOL_STATE:])
        p_rows = p_prompt[li].reshape(batch * seq, -1)
        last = li == depth - 1
        gfin = row2(g_final)
        hp_next = _ple(h2, p_rows, row2(g_ple[li]), w_pg_bf, w_pp_bf, gfin, tm_p, last)

        o_s = _sample_finish(gsub, oacc, w_new, vfs.reshape(db, N_HEADS, V_DIM), lam_init).reshape(db, -1)
        state = state_pool[li]
        pool_s = _pool_sample(us, jnp.swapaxes(state, 0, 1), w_pool_bf, row2(pool_scale[li]), db)
        h1s = _out_proj(o_s, pool_s, w_out_bf, hs, db)
        h2s = _ffn(h1s, row2(g_ffn[li]), w_gu_bf, w_down_bf, db, tf)
        hs_next = _ple(h2s, p_sample[li].reshape(db * ds, -1), row2(g_ple[li]), w_pg_bf, w_pp_bf, gfin, db, last)
        outs["ks"].append(kfs.reshape(db, ds, N_QK_HEADS, HEAD_DIM))
        outs["vs"].append(vfs.reshape(db, ds, N_HEADS, V_DIM))
        outs["ps"].append(jnp.concatenate([state[:, 1:], us[:, None, :]], axis=1))
        hp, hs = hp_next, hs_next

    y_prompt = hp.reshape(batch, seq, d)
    y_sample = hs.reshape(db, ds, d)
    return (y_prompt, y_sample, jnp.stack(outs["kp"]), jnp.stack(outs["vp"]), jnp.stack(outs["pp"]),
            jnp.stack(outs["ks"]), jnp.stack(outs["vs"]), jnp.stack(outs["ps"]))
```

```python
import functools
import math

import jax
import jax.numpy as jnp
from jax import lax
from jax.experimental import pallas as pl
from jax.experimental.pallas import tpu as pltpu

F32 = jnp.float32
BF16 = jnp.bfloat16

N_HEADS = 8
HEAD_DIM = 64
V_DIM = 2 * HEAD_DIM
N_QK_HEADS = 2 * N_HEADS
ATTN_WIDTH = N_HEADS * V_DIM
POOL_WINDOWS = (2, 4, 8, 16)
POOL_STATE = max(POOL_WINDOWS) - 1
POOL_HALO = 16
ROPE_THETA = 10000.0
EPS = 1e-6
LANES = 128
MXU_COLS = 256
VMEM_LIMIT = 56 * 1024 * 1024


def _cparams(sem):
    return pltpu.CompilerParams(dimension_semantics=sem, vmem_limit_bytes=VMEM_LIMIT)


def _rms(x, g):
    ms = jnp.mean(x * x, axis=-1, keepdims=True)
    return x * lax.rsqrt(ms + EPS) * g


def _rope_chunk(x, cos, sin_signed, first_half):
    swapped = jnp.where(first_half, pltpu.roll(x, 96, 1), pltpu.roll(x, 32, 1))
    return x * cos + swapped * sin_signed


def _col_chunks(xn_ref, w_ref):
    xn = xn_ref[...]
    for c in range(w_ref.shape[1] // MXU_COLS):
        cols = slice(c * MXU_COLS, (c + 1) * MXU_COLS)
        yield cols, jnp.dot(xn, w_ref[:, cols], preferred_element_type=F32)


def _rope(a, cos_ref, sin_ref):
    cos = cos_ref[...]
    sin = sin_ref[...]
    lane = lax.broadcasted_iota(jnp.int32, cos.shape, 1)
    first_half = (lane % HEAD_DIM) < (HEAD_DIM // 2)
    parts = [_rope_chunk(a[:, c * LANES:(c + 1) * LANES], cos, sin, first_half)
             for c in range(a.shape[1] // LANES)]
    return jnp.concatenate(parts, axis=1)


def _in_proj_prompt_kernel(x_ref, g_ref, w_ref, cos_ref, sin_ref,
                           qt_ref, kb_ref, kt_ref, vt_ref, vf_ref, u_ref, xn_ref, *, q_scale):
    width = qt_ref.shape[0]
    xn_ref[...] = _rms(x_ref[...], g_ref[...]).astype(BF16)
    for cols, acc in _col_chunks(xn_ref, w_ref):
        part, local = cols.start // width, slice(cols.start % width, cols.start % width + MXU_COLS)
        if part == 0:
            qt_ref[local, :] = (_rope(acc, cos_ref, sin_ref) * q_scale).T.astype(BF16)
        elif part == 1:
            k = _rope(acc, cos_ref, sin_ref)
            kb_ref[:, local] = k.astype(BF16)
            kt_ref[local, :] = k.T
        elif part == 2:
            vf_ref[:, local] = acc
            vt_ref[local, :] = acc.T.astype(BF16)
        else:
            u_ref[:, local] = acc


def _in_proj_prompt(x, g_mix, w_in_bf, cos_t, sin_t, batch, seq, tm, q_scale):
    m, d = x.shape
    width = ATTN_WIDTH
    nt = seq // tm
    row = lambda i: (i, 0)
    tile = lambda i: (i, 0, 0)
    tab = lambda i: (i % nt, 0)
    outs = [jax.ShapeDtypeStruct((m // tm, width, tm), BF16),
            jax.ShapeDtypeStruct((m, width), BF16),
            jax.ShapeDtypeStruct((batch, width, seq), F32),
            jax.ShapeDtypeStruct((m // tm, width, tm), BF16),
            jax.ShapeDtypeStruct((m, width), F32),
            jax.ShapeDtypeStruct((m, width), F32)]
    return pl.pallas_call(
        functools.partial(_in_proj_prompt_kernel, q_scale=q_scale),
        out_shape=outs,
        grid=(m // tm,),
        in_specs=[pl.BlockSpec((tm, d), row),
                  pl.BlockSpec((1, d), lambda i: (0, 0)),
                  pl.BlockSpec(w_in_bf.shape, lambda i: (0, 0), pipeline_mode=pl.Buffered(1)),
                  pl.BlockSpec((tm, LANES), tab),
                  pl.BlockSpec((tm, LANES), tab)],
        out_specs=[pl.BlockSpec((None, width, tm), tile),
                   pl.BlockSpec((tm, width), row),
                   pl.BlockSpec((None, width, tm), lambda i: (i // nt, 0, i % nt)),
                   pl.BlockSpec((None, width, tm), tile),
                   pl.BlockSpec((tm, width), row),
                   pl.BlockSpec((tm, width), row)],
        scratch_shapes=[pltpu.VMEM((tm, d), BF16)],
        compiler_params=_cparams(("arbitrary",)),
        name="in_proj_prompt",
    )(x, g_mix, w_in_bf, cos_t, sin_t)


def _in_proj_sample_kernel(x_ref, g_ref, w_ref, cos_ref, sin_ref, q_ref, k_ref, v_ref, u_ref, xn_ref, *, q_scale):
    n = pl.program_id(1)

    @pl.when(n == 0)
    def _():
        xn_ref[...] = _rms(x_ref[...], g_ref[...]).astype(BF16)

    acc = jnp.dot(xn_ref[...], w_ref[...], preferred_element_type=F32)

    @pl.when(n == 0)
    def _():
        q_ref[...] = (_rope(acc, cos_ref, sin_ref) * q_scale).astype(BF16)

    @pl.when(n == 1)
    def _():
        k_ref[...] = _rope(acc, cos_ref, sin_ref)

    @pl.when(n == 2)
    def _():
        v_ref[...] = acc

    @pl.when(n == 3)
    def _():
        u_ref[...] = acc


def _in_proj_sample(x, g_mix, w_in_bf, cos_t, sin_t, q_scale):
    m, d = x.shape
    width = ATTN_WIDTH
    whole = lambda i, n: (0, 0)
    outs = [jax.ShapeDtypeStruct((m, width), BF16)] + [jax.ShapeDtypeStruct((m, width), F32)] * 3
    return pl.pallas_call(
        functools.partial(_in_proj_sample_kernel, q_scale=q_scale),
        out_shape=outs,
        grid=(1, 4),
        in_specs=[pl.BlockSpec((m, d), whole),
                  pl.BlockSpec((1, d), whole),
                  pl.BlockSpec((d, width), lambda i, n: (0, n)),
                  pl.BlockSpec((m, LANES), whole),
                  pl.BlockSpec((m, LANES), whole)],
        out_specs=[pl.BlockSpec((m, width), whole)] * 4,
        scratch_shapes=[pltpu.VMEM((m, d), BF16)],
        compiler_params=_cparams(("arbitrary", "arbitrary")),
        name="in_proj_sample",
    )(x, g_mix, w_in_bf, cos_t, sin_t)


def _lambda_value(lq1, lk1, lq2, lk2, lam_init):
    a = jnp.sum(lq1 * lk1, axis=-1, keepdims=True)
    b = jnp.sum(lq2 * lk2, axis=-1, keepdims=True)
    return jnp.exp(a) - jnp.exp(b) + lam_init


def _sub_norm(o, g_sub, lam_init):
    return _rms(o, g_sub) * (1.0 - lam_init)


def _prompt_attn_kernel(pt_ref, lq1_ref, lk1_ref, lq2_ref, lk2_ref, gs_ref, qt_ref, k_ref, vt_ref,
                        qrows_ref, kn_ref, *rest, n_key_pages, n_casts, tq, lam_init):
    key_pages = rest[:n_key_pages]
    cast_in = rest[n_key_pages:n_key_pages + n_casts]
    o_ref, wsel_ref, wnew_ref = rest[n_key_pages + n_casts:n_key_pages + n_casts + 3]
    cast_out = rest[n_key_pages + n_casts + 3:n_key_pages + 2 * n_casts + 3]
    m0_ref, l0_ref, a0_ref, m1_ref, l1_ref, a1_ref = rest[n_key_pages + 2 * n_casts + 3:]
    qi = pl.program_id(2)

    @pl.when(qi == 0)
    def _():
        for src, dst in zip(cast_in, cast_out):
            dst[...] = src[...].astype(dst.dtype)

    hq = tq // 2
    qt = qt_ref[...]
    drow = lax.broadcasted_iota(jnp.int32, qt.shape, 0)
    zero = jnp.zeros_like(qt)
    qts = (jnp.where(drow < HEAD_DIM, qt, zero), jnp.where(drow >= HEAD_DIM, qt, zero))
    stats = ((m0_ref, l0_ref, a0_ref), (m1_ref, l1_ref, a1_ref))

    for m_ref, l_ref, a_ref in stats:
        m_ref[...] = jnp.full(m_ref.shape, -jnp.inf, F32)
        l_ref[...] = jnp.zeros(l_ref.shape, F32)
        a_ref[...] = jnp.zeros(a_ref.shape, F32)

    def run_chains(chains):
        scores = [jnp.dot(kj, qc[:, lanes], preferred_element_type=F32)
                  for kj, _, qc, _, lanes, _ in chains]
        probs = []
        for st, (_, _, _, (m_ref, l_ref, _), lanes, keep) in zip(scores, chains):
            if keep is not None:
                st = jnp.where(keep, st, -jnp.inf)
            m_old = m_ref[:, lanes]
            m_new = jnp.maximum(m_old, jnp.max(st, axis=0, keepdims=True))
            pt = jnp.exp2(st - m_new)
            alpha = jnp.exp2(m_old - m_new)
            l_ref[:, lanes] = alpha * l_ref[:, lanes] + jnp.sum(pt, axis=0, keepdims=True)
            m_ref[:, lanes] = m_new
            probs.append((pt.astype(BF16), alpha))
        for (pt, alpha), (_, vtj, _, (_, _, a_ref), lanes, _) in zip(probs, chains):
            a_ref[:, lanes] = alpha * a_ref[:, lanes] + jnp.dot(vtj, pt, preferred_element_type=F32)

    def body(j, carry):
        kj = k_ref[pl.ds(pl.multiple_of(j * tq, tq), tq), :]
        vtj = vt_ref[j]
        run_chains([(kj, vtj, qc, refs, slice(half * hq, (half + 1) * hq), None)
                    for qc, refs in zip(qts, stats) for half in range(2)])
        return carry

    lax.fori_loop(0, qi, body, 0)

    start = pl.multiple_of(qi * tq, tq)
    k_lo = k_ref[pl.ds(start, hq), :]
    k_all = k_ref[pl.ds(start, tq), :]
    vt_all = vt_ref[qi]
    tri = (lax.broadcasted_iota(jnp.int32, (hq, hq), 0) <= lax.broadcasted_iota(jnp.int32, (hq, hq), 1))
    late = (lax.broadcasted_iota(jnp.int32, (tq, hq), 0) <= lax.broadcasted_iota(jnp.int32, (tq, hq), 1) + hq)
    diagonal = []
    for qc, refs in zip(qts, stats):
        diagonal.append((k_lo, vt_all[:, :hq], qc, refs, slice(0, hq), tri))
        diagonal.append((k_all, vt_all, qc, refs, slice(hq, tq), late))
    run_chains(diagonal)

    lam = _lambda_value(lq1_ref[...], lk1_ref[...], lq2_ref[...], lk2_ref[...], lam_init)
    w_past, w_new = _sequence_weights(qrows_ref[0], kn_ref[0], key_pages, lam)
    wsel_ref[0] = w_past
    wnew_ref[0] = jnp.broadcast_to(w_new, wnew_ref.shape[1:])

    ot = a0_ref[...] * (1.0 / l0_ref[...]) - a1_ref[...] * (lam / l1_ref[...])
    ms = jnp.mean(ot * ot, axis=0, keepdims=True)
    ot = ot * lax.rsqrt(ms + EPS) * gs_ref[...] * (1.0 - lam_init)
    o_ref[...] = ot.T.astype(o_ref.dtype)


def _slab_rows(rows, n_steps):
    need = -(-rows // n_steps)
    return next(r for r in range(16, rows + 1, 16) if rows % r == 0 and r >= need)


def _prompt_attention(qt, k, vt, lams, g_sub_col, batch, seq, tq, lam_init,
                      page_table, qrows, k_new, cache_kt, li, f32_weights):
    nq = seq // tq
    db, n_pages = page_table.shape
    page = cache_kt.shape[4]
    n_steps = batch * N_HEADS * nq
    assert n_steps >= db, "one sample sequence rides on each grid step"
    seq_of = lambda b, h, i: jnp.minimum((b * N_HEADS + h) * nq + i, db - 1)

    def slab(w):
        rows = _slab_rows(w.shape[0], batch * N_HEADS)
        last = w.shape[0] // rows - 1
        return pl.BlockSpec((rows, w.shape[1]), lambda b, h, i, pt: (jnp.minimum(b * N_HEADS + h, last), 0))
    small = lambda shape: pl.BlockSpec(shape, lambda b, h, i, pt: (0, 0))
    per_seq = lambda shape: pl.BlockSpec(shape, lambda b, h, i, pt: (seq_of(b, h, i), 0, 0))

    def key_page(s):
        return pl.BlockSpec((None, None, N_QK_HEADS, HEAD_DIM, page),
                            lambda b, h, i, pt: (li, pt[seq_of(b, h, i), s], 0, 0, 0))

    grid_spec = pltpu.PrefetchScalarGridSpec(
        num_scalar_prefetch=1,
        grid=(batch, N_HEADS, nq),
        in_specs=[small((1, HEAD_DIM))] * 4 + [small((V_DIM, 1))] + [
            pl.BlockSpec((None, V_DIM, tq), lambda b, h, i, pt: (b * nq + i, h, 0)),
            pl.BlockSpec((seq, V_DIM), lambda b, h, i, pt: (b, h)),
            pl.BlockSpec((nq, V_DIM, tq), lambda b, h, i, pt: (b, h, 0)),
            per_seq((1, N_QK_HEADS, ATTN_WIDTH)),
            per_seq((1, 1, ATTN_WIDTH))] + [key_page(s) for s in range(n_pages)] + [slab(w) for w in f32_weights],
        out_specs=[pl.BlockSpec((tq, V_DIM), lambda b, h, i, pt: (b * nq + i, h)),
                   per_seq((1, N_HEADS, n_pages * page)),
                   per_seq((1, N_HEADS, V_DIM))] + [slab(w) for w in f32_weights],
        scratch_shapes=[pltpu.VMEM((1, tq), F32), pltpu.VMEM((1, tq), F32), pltpu.VMEM((V_DIM, tq), F32)] * 2,
    )
    kernel = functools.partial(_prompt_attn_kernel, n_key_pages=n_pages, n_casts=len(f32_weights), tq=tq,
                               lam_init=lam_init)
    return pl.pallas_call(
        kernel,
        out_shape=[jax.ShapeDtypeStruct(k.shape, BF16),
                   jax.ShapeDtypeStruct((db, N_HEADS, n_pages * page), F32),
                   jax.ShapeDtypeStruct((db, N_HEADS, V_DIM), F32)]
                  + [jax.ShapeDtypeStruct(w.shape, BF16) for w in f32_weights],
        grid_spec=grid_spec,
        compiler_params=_cparams(("arbitrary", "arbitrary", "arbitrary")),
        name="prompt_attention",
    )(page_table, *lams, g_sub_col, qt, k, vt, qrows, k_new, *([cache_kt] * n_pages), *f32_weights)


def _sequence_weights(qrows, k_new, k_refs, lam):
    n_pages = len(k_refs)
    _, _, page = k_refs[0].shape
    s = jnp.concatenate([jnp.dot(qrows, k_ref[...].reshape(N_QK_HEADS * HEAD_DIM, page),
                                 preferred_element_type=F32) for k_ref in k_refs], axis=1)
    s_new = jnp.sum(qrows * k_new, axis=-1, keepdims=True)
    m = jnp.maximum(jnp.max(s, axis=-1, keepdims=True), s_new)
    p = jnp.exp(s - m)
    p_new = jnp.exp(s_new - m)
    l = jnp.sum(p, axis=-1, keepdims=True) + p_new
    a = p / l
    a_new = p_new / l
    w = a[:N_HEADS] - lam * a[N_HEADS:]
    w_new = a_new[:N_HEADS] - lam * a_new[N_HEADS:]
    return w, w_new


def _paged_values_slots(step, slots, wsel_refs, v_refs, oacc_ref, *, pages_per_seq, n_valid_pages):
    per_step = len(v_refs)
    page = v_refs[0].shape[0]
    n_chains = 2
    for i in slots:
        flat = step * per_step + i
        seq = jnp.minimum(flat, n_valid_pages - 1) // pages_per_seq
        w = wsel_refs[i][...]
        parts = [jnp.zeros((N_HEADS, V_DIM), F32)] * n_chains
        for t in range(page):
            parts[t % n_chains] = parts[t % n_chains] + v_refs[i][t] * jnp.broadcast_to(w[:, t:t + 1],
                                                                                        (N_HEADS, V_DIM))
        part = functools.reduce(lambda a, b: a + b, parts)
        oacc_ref[seq] += jnp.where(flat < n_valid_pages, part, 0.0)


def _sample_finish_kernel(gs_ref, oacc_ref, wnew_ref, vn_ref, o_ref, *, lam_init):
    full = oacc_ref[...] + wnew_ref[...] * vn_ref[...]
    o_ref[...] = _sub_norm(full, gs_ref[...], lam_init).astype(o_ref.dtype)


def _sample_finish(g_sub, oacc, w_new, v_new, lam_init):
    db = oacc.shape[0]
    whole = pl.BlockSpec((db, N_HEADS, V_DIM), lambda i: (0, 0, 0))
    return pl.pallas_call(
        functools.partial(_sample_finish_kernel, lam_init=lam_init),
        out_shape=jax.ShapeDtypeStruct((db, N_HEADS, V_DIM), BF16),
        grid=(1,),
        in_specs=[pl.BlockSpec((1, V_DIM), lambda i: (0, 0)), whole, whole, whole],
        out_specs=whole,
        compiler_params=_cparams(("arbitrary",)),
        name="sample_finish",
    )(g_sub, oacc, w_new, v_new)


def _pool_matmul(pooled_groups, wp_ref, scale):
    outs = [jnp.dot(pg.astype(BF16), wp_ref[g], preferred_element_type=F32)
            for g, pg in enumerate(pooled_groups)]
    return jnp.concatenate(outs, axis=1) * scale


def _pool_prompt_kernel(u_ref, halo_ref, wp_ref, sc_ref, o_ref, ext_ref, *, ts):
    i = pl.program_id(1)
    group = wp_ref.shape[1]
    u = u_ref[...]
    ext_ref[pl.ds(POOL_HALO, ts), :] = u
    ext_ref[pl.ds(0, POOL_HALO), :] = jnp.where(i > 0, halo_ref[...], 0.0)
    pos1 = (i * ts + lax.broadcasted_iota(jnp.int32, (ts, 1), 0) + 1).astype(F32)
    pooled = []
    for g, win in enumerate(POOL_WINDOWS):
        cols = slice(g * group, (g + 1) * group)
        wsum = u[:, cols]
        for k in range(1, win):
            wsum = wsum + ext_ref[pl.ds(POOL_HALO - k, ts), cols]
        cnt = jnp.minimum(pos1, float(win))
        pooled.append(wsum / cnt - u[:, cols])
    o_ref[...] = _pool_matmul(pooled, wp_ref, sc_ref[...]).astype(o_ref.dtype)


def _pool_prompt(u, w_pool_bf, pool_scale, batch, seq, ts):
    m, width = u.shape
    nt = seq // ts
    r = ts // POOL_HALO
    return pl.pallas_call(
        functools.partial(_pool_prompt_kernel, ts=ts),
        out_shape=jax.ShapeDtypeStruct((m, width), BF16),
        grid=(batch, nt),
        in_specs=[pl.BlockSpec((ts, width), lambda b, i: (b * nt + i, 0)),
                  pl.BlockSpec((POOL_HALO, width), lambda b, i: (jnp.maximum((b * nt + i) * r - 1, 0), 0)),
                  pl.BlockSpec(w_pool_bf.shape, lambda b, i: (0, 0, 0)),
                  pl.BlockSpec((1, width), lambda b, i: (0, 0))],
        out_specs=pl.BlockSpec((ts, width), lambda b, i: (b * nt + i, 0)),
        scratch_shapes=[pltpu.VMEM((ts + POOL_HALO, width), F32)],
        compiler_params=_cparams(("arbitrary", "arbitrary")),
        name="pool_prompt",
    )(u, u, w_pool_bf, pool_scale)


def _pool_sample_kernel(u_ref, st_ref, wp_ref, sc_ref, o_ref):
    group = wp_ref.shape[1]
    u = u_ref[...]
    pooled = []
    for g, win in enumerate(POOL_WINDOWS):
        cols = slice(g * group, (g + 1) * group)
        wsum = u[:, cols]
        for k in range(1, win):
            wsum = wsum + st_ref[POOL_STATE - k, :, cols]
        pooled.append(wsum / float(win) - u[:, cols])
    o_ref[...] = _pool_matmul(pooled, wp_ref, sc_ref[...]).astype(o_ref.dtype)


def _pool_sample(u, state_t, w_pool_bf, pool_scale, tb):
    m, width = u.shape
    return pl.pallas_call(
        _pool_sample_kernel,
        out_shape=jax.ShapeDtypeStruct((m, width), BF16),
        grid=(m // tb,),
        in_specs=[pl.BlockSpec((tb, width), lambda i: (i, 0)),
                  pl.BlockSpec((POOL_STATE, tb, width), lambda i: (0, i, 0)),
                  pl.BlockSpec(w_pool_bf.shape, lambda i: (0, 0, 0)),
                  pl.BlockSpec((1, width), lambda i: (0, 0))],
        out_specs=pl.BlockSpec((tb, width), lambda i: (i, 0)),
        compiler_params=_cparams(("arbitrary",)),
        name="pool_sample",
    )(u, state_t, w_pool_bf, pool_scale)


def _out_proj_kernel(o_ref, p_ref, w_ref, h_ref, out_ref):
    half = o_ref.shape[1]
    acc = jnp.dot(o_ref[...], w_ref[pl.ds(0, half), :], preferred_element_type=F32)
    acc += jnp.dot(p_ref[...], w_ref[pl.ds(half, half), :], preferred_element_type=F32)
    out_ref[...] = h_ref[...] + acc


def _out_proj(o, pool, w_out_bf, h, tm):
    m, d = h.shape
    row = lambda i: (i, 0)
    return pl.pallas_call(
        _out_proj_kernel,
        out_shape=jax.ShapeDtypeStruct((m, d), F32),
        grid=(m // tm,),
        in_specs=[pl.BlockSpec((tm, o.shape[1]), row),
                  pl.BlockSpec((tm, pool.shape[1]), row),
                  pl.BlockSpec(w_out_bf.shape, lambda i: (0, 0)),
                  pl.BlockSpec((tm, d), row)],
        out_specs=pl.BlockSpec((tm, d), row),
        compiler_params=_cparams(("arbitrary",)),
        name="out_proj",
    )(o, pool, w_out_bf, h)


def _ffn_body(h_ref, g_ref, wg_ref, wu_ref, wd_ref, out_ref, xn_ref, acc_ref, riders=()):
    f = pl.program_id(1)
    riders = list(riders)

    def ride():
        if riders:
            riders.pop(0)()

    @pl.when(f == 0)
    def _():
        xn_ref[...] = _rms(h_ref[...], g_ref[...]).astype(BF16)
        acc_ref[...] = jnp.zeros(acc_ref.shape, F32)

    xn = xn_ref[...]
    acts = []
    for c in range(wg_ref.shape[1] // MXU_COLS):
        cols = slice(c * MXU_COLS, (c + 1) * MXU_COLS)
        gate = jnp.dot(xn, wg_ref[:, cols], preferred_element_type=F32)
        ride()
        up = jnp.dot(xn, wu_ref[:, cols], preferred_element_type=F32)
        ride()
        acts.append((gate * jax.nn.sigmoid(gate) * up).astype(BF16))
    acc_ref[...] += jnp.dot(jnp.concatenate(acts, axis=1), wd_ref[...], preferred_element_type=F32)
    while riders:
        ride()

    @pl.when(f == pl.num_programs(1) - 1)
    def _():
        out_ref[...] = h_ref[...] + acc_ref[...]


def _ffn_kernel(h_ref, g_ref, wg_ref, wu_ref, wd_ref, out_ref, xn_ref, acc_ref):
    _ffn_body(h_ref, g_ref, wg_ref, wu_ref, wd_ref, out_ref, xn_ref, acc_ref)


def _ffn_paged_kernel(page_ids_ref, h_ref, g_ref, wg_ref, wu_ref, wd_ref, *rest,
                      per_step, pages_per_seq, n_valid_pages):
    wsel_refs = rest[:per_step]
    v_refs = rest[per_step:2 * per_step]
    out_ref, oacc_ref, xn_ref, acc_ref = rest[2 * per_step:]
    step = pl.program_id(0) * pl.num_programs(1) + pl.program_id(1)

    @pl.when(step == 0)
    def _():
        oacc_ref[...] = jnp.zeros(oacc_ref.shape, F32)

    n_groups = 2 * (wg_ref.shape[1] // MXU_COLS) + 1
    size = -(-per_step // n_groups)
    riders = [functools.partial(_paged_values_slots, step, range(lo, min(lo + size, per_step)), wsel_refs, v_refs,
                                oacc_ref, pages_per_seq=pages_per_seq, n_valid_pages=n_valid_pages)
              for lo in range(0, per_step, size)]
    _ffn_body(h_ref, g_ref, wg_ref, wu_ref, wd_ref, out_ref, xn_ref, acc_ref, riders)


def _ffn_specs(d, tm, tf, nf, index):
    return [pl.BlockSpec((tm, d), index(lambda i, f: (i, 0))),
            pl.BlockSpec((1, d), index(lambda i, f: (0, 0))),
            pl.BlockSpec((d, tf), index(lambda i, f: (0, f))),
            pl.BlockSpec((d, tf), index(lambda i, f: (0, nf + f))),
            pl.BlockSpec((tf, d), index(lambda i, f: (f, 0)))]


def _ffn(h, g_ffn, w_gate_up_bf, w_down_bf, tm, tf):
    m, d = h.shape
    nf = w_down_bf.shape[0] // tf
    plain = lambda fn: fn
    return pl.pallas_call(
        _ffn_kernel,
        out_shape=jax.ShapeDtypeStruct((m, d), F32),
        grid=(m // tm, nf),
        in_specs=_ffn_specs(d, tm, tf, nf, plain),
        out_specs=pl.BlockSpec((tm, d), lambda i, f: (i, 0)),
        scratch_shapes=[pltpu.VMEM((tm, d), BF16), pltpu.VMEM((tm, d), F32)],
        compiler_params=_cparams(("arbitrary", "arbitrary")),
        name="ffn",
    )(h, g_ffn, w_gate_up_bf, w_gate_up_bf, w_down_bf)


def _ffn_paged(h, g_ffn, w_gate_up_bf, w_down_bf, tm, tf, page_table, wsel, cache_v, li):
    m, d = h.shape
    nf = w_down_bf.shape[0] // tf
    n_steps = (m // tm) * nf
    db, pages_per_seq = page_table.shape
    page = cache_v.shape[2]
    n_valid = db * pages_per_seq
    per_step = -(-n_valid // n_steps)
    page_ids = page_table.reshape(-1)
    with_ids = lambda fn: (lambda i, f, ids: fn(i, f))
    slot_page = lambda i, f, s: jnp.minimum((i * nf + f) * per_step + s, n_valid - 1)

    def w_spec(s):
        return pl.BlockSpec((None, N_HEADS, page), lambda i, f, ids: (slot_page(i, f, s) // pages_per_seq, 0,
                                                                      slot_page(i, f, s) % pages_per_seq))

    def v_spec(s):
        return pl.BlockSpec((None, None, page, N_HEADS, V_DIM),
                            lambda i, f, ids: (li, ids[slot_page(i, f, s)], 0, 0, 0))

    grid_spec = pltpu.PrefetchScalarGridSpec(
        num_scalar_prefetch=1,
        grid=(m // tm, nf),
        in_specs=_ffn_specs(d, tm, tf, nf, with_ids)
                 + [w_spec(s) for s in range(per_step)]
                 + [v_spec(s) for s in range(per_step)],
        out_specs=[pl.BlockSpec((tm, d), lambda i, f, ids: (i, 0)),
                   pl.BlockSpec((db, N_HEADS, V_DIM), lambda i, f, ids: (0, 0, 0))],
        scratch_shapes=[pltpu.VMEM((tm, d), BF16), pltpu.VMEM((tm, d), F32)],
    )
    return pl.pallas_call(
        functools.partial(_ffn_paged_kernel, per_step=per_step, pages_per_seq=pages_per_seq,
                          n_valid_pages=n_valid),
        out_shape=[jax.ShapeDtypeStruct((m, d), F32), jax.ShapeDtypeStruct((db, N_HEADS, V_DIM), F32)],
        grid_spec=grid_spec,
        compiler_params=_cparams(("arbitrary", "arbitrary")),
        name="ffn_paged",
    )(page_ids, h, g_ffn, w_gate_up_bf, w_gate_up_bf, w_down_bf, *([wsel] * per_step), *([cache_v] * per_step))


def _ple_kernel(h_ref, p_ref, gp_ref, wg_ref, wp_ref, gf_ref, y_ref, *, final_norm):
    h = h_ref[...]
    xn = _rms(h, gp_ref[...]).astype(BF16)
    gate = jax.nn.sigmoid(jnp.dot(xn, wg_ref[...], preferred_element_type=F32))
    proj = jnp.dot(p_ref[...].astype(BF16), wp_ref[...], preferred_element_type=F32)
    y = h + gate * proj
    y_ref[...] = _rms(y, gf_ref[...]) if final_norm else y


def _ple(h, p, g_ple, w_gate_bf, w_proj_bf, g_final, tm, final_norm):
    m, d = h.shape
    row = lambda i: (i, 0)
    const = lambda i: (0, 0)
    return pl.pallas_call(
        functools.partial(_ple_kernel, final_norm=final_norm),
        out_shape=jax.ShapeDtypeStruct((m, d), F32),
        grid=(m // tm,),
        in_specs=[pl.BlockSpec((tm, d), row),
                  pl.BlockSpec((tm, p.shape[1]), row),
                  pl.BlockSpec((1, d), const),
                  pl.BlockSpec(w_gate_bf.shape, const),
                  pl.BlockSpec(w_proj_bf.shape, const),
                  pl.BlockSpec((1, d), const)],
        out_specs=pl.BlockSpec((tm, d), row),
        compiler_params=_cparams(("arbitrary",)),
        name="ple_final",
    )(h, p, g_ple, w_gate_bf, w_proj_bf, g_final)


def _rope_tables(positions):
    half = HEAD_DIM // 2
    inv = ROPE_THETA ** (-jnp.arange(half, dtype=F32) / half)
    ang = positions.astype(F32)[:, None] * inv[None, :]
    cos, sin = jnp.cos(ang), jnp.sin(ang)
    reps = LANES // HEAD_DIM
    return jnp.tile(jnp.concatenate([cos, cos], axis=1), (1, reps)), \
        jnp.tile(jnp.concatenate([-sin, sin], axis=1), (1, reps))


def _row_tile(m, want):
    return want if m % want == 0 else m


def kernel(x_prompt, x_sample, p_prompt, p_sample, cache_k, cache_v, state_pool, page_table, g_mix, w_in, w_out, lambda_q1, lambda_k1, lambda_q2, lambda_k2, g_sub, w_pool, pool_scale, g_ffn, w_gate_up, w_down, g_ple, w_ple_gate, w_ple_proj, g_final):
    batch, seq, d = x_prompt.shape
    db, ds, _ = x_sample.shape
    depth = g_mix.shape[0]
    n_pages = page_table.shape[1]
    page = cache_k.shape[2]
    past = n_pages * page
    assert ds == 1 and d == 2 * ATTN_WIDTH

    hp = x_prompt.reshape(batch * seq, d)
    hs = x_sample.reshape(db * ds, d)
    cos_p, sin_p = _rope_tables(jnp.arange(seq))
    cos_s, sin_s = _rope_tables(jnp.full((db,), past))
    row2 = lambda a: a.reshape(1, -1)

    tm_p = _row_tile(batch * seq, 512)
    tq = _row_tile(seq, 512)
    ts = _row_tile(seq, 512)
    cache_kt = jnp.transpose(cache_k, (0, 1, 3, 4, 2))
    tf = 512
    qk_scale = HEAD_DIM ** -0.5

    outs = {k: [] for k in ("kp", "vp", "pp", "ks", "vs", "ps")}
    for li in range(depth):
        lam_init = 0.8 - 0.6 * math.exp(-0.3 * li)
        w_in_bf = w_in[li].astype(BF16)
        w_pool_bf = w_pool[li].astype(BF16)
        w_pp_bf = w_ple_proj[li].astype(BF16)
        lams = tuple(row2(a[li]) for a in (lambda_q1, lambda_k1, lambda_q2, lambda_k2))
        gsub = row2(g_sub[li])

        qs, kfs, vfs, us = _in_proj_sample(hs, row2(g_mix[li]), w_in_bf, cos_s, sin_s, qk_scale)
        col_sub = jnp.arange(ATTN_WIDTH) // HEAD_DIM
        row_sub = (jnp.arange(N_QK_HEADS) % N_HEADS) * 2 + jnp.arange(N_QK_HEADS) // N_HEADS
        qrows = qs.astype(F32)[:, None, :] * (col_sub[None, :] == row_sub[:, None]).astype(F32)[None]
        qt, kb, kt, vt, vf, u = _in_proj_prompt(hp, row2(g_mix[li]), w_in_bf, cos_p, sin_p, batch, seq, tq,
                                                qk_scale * math.log2(math.e))
        o, wsel, w_new, w_out_bf, w_gu_bf, w_down_bf, w_pg_bf = _prompt_attention(
            qt, kb, vt, lams, g_sub[li].reshape(-1, 1), batch, seq, tq, lam_init,
            page_table, qrows, kfs.reshape(db, 1, ATTN_WIDTH), cache_kt, li,
            [w_out[li], w_gate_up[li], w_down[li], w_ple_gate[li]])
        pool = _pool_prompt(u, w_pool_bf, row2(pool_scale[li]), batch, seq, ts)
        h1 = _out_proj(o, pool, w_out_bf, hp, tm_p)
        h2, oacc = _ffn_paged(h1, row2(g_ffn[li]), w_gu_bf, w_down_bf, tm_p, tf, page_table, wsel, cache_v, li)
        outs["kp"].append(jnp.transpose(kt.reshape(batch, N_QK_HEADS, HEAD_DIM, seq), (0, 3, 1, 2)))
        outs["vp"].append(vf.reshape(batch, seq, N_HEADS, V_DIM))
        outs["pp"].append(u.reshape(batch, seq, -1)[:, seq - POOL_STATE:])
        p_rows = p_prompt[li].reshape(batch * seq, -1)
        last = li == depth - 1
        gfin = row2(g_final)
        hp_next = _ple(h2, p_rows, row2(g_ple[li]), w_pg_bf, w_pp_bf, gfin, tm_p, last)

        o_s = _sample_finish(gsub, oacc, w_new, vfs.reshape(db, N_HEADS, V_DIM), lam_init).reshape(db, -1)
        state = state_pool[li]
        pool_s = _pool_sample(us, jnp.swapaxes(state, 0, 1), w_pool_bf, row2(pool_scale[li]), db)
        h1s = _out_proj(o_s, pool_s, w_out_bf, hs, db)
        h2s = _ffn(h1s, row2(g_ffn[li]), w_gu_bf, w_down_bf, db, tf)
        hs_next = _ple(h2s, p_sample[li].reshape(db * ds, -1), row2(g_ple[li]), w_pg_bf, w_pp_bf, gfin, db, last)
        outs["ks"].append(kfs.reshape(db, ds, N_QK_HEADS, HEAD_DIM))
        outs["vs"].append(vfs.reshape(db, ds, N_HEADS, V_DIM))
        outs["ps"].append(jnp.concatenate([state[:, 1:], us[:, None, :]], axis=1))
        hp, hs = hp_next, hs_next

    y_prompt = hp.reshape(batch, seq, d)
    y_sample = hs.reshape(db, ds, d)
    return (y_prompt, y_sample, jnp.stack(outs["kp"]), jnp.stack(outs["vp"]), jnp.stack(outs["pp"]),
            jnp.stack(outs["ks"]), jnp.stack(outs["vs"]), jnp.stack(outs["ps"]))
```
